```python
import math
import jax, jax.numpy as jnp
from jax import lax
import numpy as np

D_MODEL = 1024
BATCH = 8
SEQ = 2048
DEPTH = 1
DEC_BATCH = 128
DEC_SEQ = 1
PAST_LEN = 16384
PAGE_SIZE = 128

RMS_EPS = 1e-6
SC_DIM = D_MODEL
SC_WIDTH = 3
SSM_D_INNER = 2 * D_MODEL
SSM_HEAD_DIM = 64
SSM_HEADS = SSM_D_INNER // SSM_HEAD_DIM
SSM_STATE = 128
SSM_GROUPS = 4
SSM_CONV = 4
SSM_CHUNK = 128
SSM_CONV_DIM = SSM_D_INNER + 2 * SSM_GROUPS * SSM_STATE
MEM_LEN = 256
ATTN_HEADS = 4
ATTN_HEAD_DIM = D_MODEL // ATTN_HEADS
ATTN_DIM = ATTN_HEADS * ATTN_HEAD_DIM
N_BRANCHES = 3
FFN_HIDDEN = ((8 * D_MODEL // 3 + 255) // 256) * 256
IN_SIZES = (SC_DIM, SC_DIM, SC_DIM, SSM_D_INNER, SSM_CONV_DIM, SSM_HEADS, ATTN_DIM, N_BRANCHES * D_MODEL)
IN_COLS = sum(IN_SIZES)

kernel_name = 'hybrid_gated_conv_ssd_memattn_step'


def rmsnorm(x, w, eps=RMS_EPS):
    xf = x.astype(jnp.float32)
    y = xf * lax.rsqrt(jnp.mean(xf * xf, axis=-1, keepdims=True) + eps)
    return (y * w.astype(jnp.float32)).astype(x.dtype)


def group_rmsnorm(x, w, groups, eps=RMS_EPS):
    xf = x.astype(jnp.float32)
    shp = xf.shape
    xg = xf.reshape(shp[:-1] + (groups, shp[-1] // groups))
    xg = xg * lax.rsqrt(jnp.mean(xg * xg, axis=-1, keepdims=True) + eps)
    return (xg.reshape(shp) * w.astype(jnp.float32)).astype(x.dtype)


def causal_depthwise_conv(u, hist, w):
    width = w.shape[0]
    seq = u.shape[1]
    full = jnp.concatenate([hist.astype(u.dtype), u], axis=1)
    out = full[:, 0:seq] * w[0]
    for k in range(1, width):
        out = out + full[:, k:k + seq] * w[k]
    return out, full[:, seq:]


def ssd_chunked(xs, dt, a, bm, cm, s0):
    bsz, seq, nh, hd = xs.shape
    ng, ns = bm.shape[2], bm.shape[3]
    hpg = nh // ng
    q = SSM_CHUNK
    nc = seq // q
    f32 = jnp.float32
    xdt = (xs.astype(f32) * dt[..., None]).reshape(bsz, nc, q, ng, hpg, hd)
    br = bm.astype(f32).reshape(bsz, nc, q, ng, ns)
    cr = cm.astype(f32).reshape(bsz, nc, q, ng, ns)
    acum = jnp.cumsum((dt * a).reshape(bsz, nc, q, ng, hpg), axis=2)
    causal = jnp.tril(jnp.ones((q, q), dtype=bool))[:, :, None, None]
    seg = acum[:, :, :, None] - acum[:, :, None, :]
    lmat = jnp.exp(jnp.where(causal, seg, -jnp.inf))
    cb = jnp.einsum('bcign,bcjgn->bcijg', cr, br)
    y_diag = jnp.einsum('bcijg,bcijgh,bcjghp->bcighp', cb, lmat, xdt)
    decay_states = jnp.exp(acum[:, :, -1:] - acum)
    chunk_states = jnp.einsum('bcjgn,bcjgh,bcjghp->bcghpn', br, decay_states, xdt)
    chunk_decay = jnp.exp(acum[:, :, -1])

    def carry_step(s, inp):
        st, dec = inp
        return s * dec[..., None, None] + st, s

    s_init = s0.astype(f32).reshape(bsz, ng, hpg, hd, ns)
    s_final, s_prev = lax.scan(carry_step, s_init,
                               (jnp.moveaxis(chunk_states, 1, 0), jnp.moveaxis(chunk_decay, 1, 0)))
    s_prev = jnp.moveaxis(s_prev, 0, 1)
    y_off = jnp.einsum('bcign,bcghpn,bcigh->bcighp', cr, s_prev, jnp.exp(acum))
    y = (y_diag + y_off).reshape(bsz, seq, nh, hd)
    return y.astype(xs.dtype), s_final.reshape(bsz, nh, hd, ns).astype(s0.dtype)


def ssd_recurrent(xs, dt, a, bm, cm, s0):
    hpg = xs.shape[2] // bm.shape[2]
    f32 = jnp.float32

    def step(s, inp):
        xt, dtt, bt, ct = inp
        bh = jnp.repeat(bt, hpg, axis=1)
        ch = jnp.repeat(ct, hpg, axis=1)
        s = s * jnp.exp(dtt * a)[..., None, None] + (dtt[..., None] * xt)[..., None] * bh[:, :, None, :]
        return s, jnp.einsum('bhpn,bhn->bhp', s, ch)

    seqs = (jnp.moveaxis(xs.astype(f32), 1, 0), jnp.moveaxis(dt.astype(f32), 1, 0),
            jnp.moveaxis(bm.astype(f32), 1, 0), jnp.moveaxis(cm.astype(f32), 1, 0))
    s_final, ys = lax.scan(step, s0.astype(f32), seqs)
    return jnp.moveaxis(ys, 0, 1).astype(xs.dtype), s_final.astype(s0.dtype)


def mixer_block(xn, mem_k, mem_v, sc_hist, ssm_hist, ssm_s0, ssd_fn,
                w_in, sc_conv_w, w_sc_out, ssm_conv_w, ssm_conv_b, ssm_dt_bias, ssm_a_log, ssm_d,
                ssm_norm_w, w_ssm_out, w_attn_o, w_merge_o):
    bsz, seq, _ = xn.shape
    splits = [int(v) for v in np.cumsum(IN_SIZES)[:-1]]
    sc_b, sc_c, sc_x, z, xbc, dt_raw, q, gates = jnp.split(xn @ w_in, splits, axis=-1)
    conv_u, sc_new = causal_depthwise_conv(sc_c * sc_x, sc_hist, sc_conv_w)
    y_a = (sc_b * conv_u) @ w_sc_out
    xbc_c, ssm_conv_new = causal_depthwise_conv(xbc, ssm_hist, ssm_conv_w)
    xbc_c = jax.nn.silu(xbc_c + ssm_conv_b)
    xs, bm, cm = jnp.split(xbc_c, [SSM_D_INNER, SSM_D_INNER + SSM_GROUPS * SSM_STATE], axis=-1)
    xs = xs.reshape(bsz, seq, SSM_HEADS, SSM_HEAD_DIM)
    bm = bm.reshape(bsz, seq, SSM_GROUPS, SSM_STATE)
    cm = cm.reshape(bsz, seq, SSM_GROUPS, SSM_STATE)
    dt = jax.nn.softplus(dt_raw.astype(jnp.float32) + ssm_dt_bias.astype(jnp.float32))
    a = -jnp.exp(ssm_a_log.astype(jnp.float32))
    y_ssd, ssm_new = ssd_fn(xs, dt, a, bm, cm, ssm_s0)
    y_ssd = (y_ssd + ssm_d[:, None] * xs).reshape(bsz, seq, SSM_D_INNER) * jax.nn.silu(z)
    y_b = group_rmsnorm(y_ssd, ssm_norm_w, SSM_GROUPS) @ w_ssm_out
    qh = q.reshape(bsz, seq, ATTN_HEADS, ATTN_HEAD_DIM)
    scores = jnp.einsum('blhd,bmhd->bhlm', qh, mem_k).astype(jnp.float32) * (ATTN_HEAD_DIM ** -0.5)
    probs = jax.nn.softmax(scores, axis=-1).astype(mem_v.dtype)
    y_c = jnp.einsum('bhlm,bmhd->blhd', probs, mem_v).reshape(bsz, seq, ATTN_DIM) @ w_attn_o
    g_a, g_b, g_c = jnp.split(jax.nn.sigmoid(gates), N_BRANCHES, axis=-1)
    merged = g_a * y_a + g_b * y_b + g_c * y_c
    return merged @ w_merge_o, (sc_new, ssm_conv_new, ssm_new)


def swiglu(xn, w_gate, w_up, w_down):
    return (jax.nn.silu(xn @ w_gate) * (xn @ w_up)) @ w_down


def setup_inputs(seed: int = 0) -> dict:
    key = jax.random.key(seed)
    ks = jax.random.split(key, 32)
    f32 = jnp.float32

    def nrm(k, shape, scale):
        return jax.random.normal(k, shape, f32) * scale

    dt0 = jnp.exp(jax.random.uniform(ks[13], (DEPTH, SSM_HEADS), f32) * (math.log(0.1) - math.log(0.001)) + math.log(0.001))
    return {
        'x_prompt': nrm(ks[0], (BATCH, SEQ, D_MODEL), 1.0),
        'x_sample': nrm(ks[1], (DEC_BATCH, DEC_SEQ, D_MODEL), 1.0),
        'mem_prompt': nrm(ks[2], (BATCH, MEM_LEN, D_MODEL), 1.0),
        'cache_mem_k': nrm(ks[3], (DEPTH, DEC_BATCH, MEM_LEN, ATTN_HEADS, ATTN_HEAD_DIM), 1.0),
        'cache_mem_v': nrm(ks[4], (DEPTH, DEC_BATCH, MEM_LEN, ATTN_HEADS, ATTN_HEAD_DIM), 1.0),
        'state_conv': nrm(ks[5], (DEPTH, DEC_BATCH, SC_WIDTH - 1, SC_DIM), 1.0),
        'state_ssm_conv': nrm(ks[6], (DEPTH, DEC_BATCH, SSM_CONV - 1, SSM_CONV_DIM), 1.0),
        'state_ssm': nrm(ks[7], (DEPTH, DEC_BATCH, SSM_HEADS, SSM_HEAD_DIM, SSM_STATE), 0.2),
        'norm_mix_w': 1.0 + nrm(ks[8], (DEPTH, D_MODEL), 0.02),
        'w_in': nrm(ks[9], (DEPTH, D_MODEL, IN_COLS), D_MODEL ** -0.5),
        'sc_conv_w': nrm(ks[10], (DEPTH, SC_WIDTH, SC_DIM), SC_WIDTH ** -0.5),
        'w_sc_out': nrm(ks[11], (DEPTH, SC_DIM, D_MODEL), SC_DIM ** -0.5),
        'ssm_conv_w': nrm(ks[12], (DEPTH, SSM_CONV, SSM_CONV_DIM), SSM_CONV ** -0.5),
        'ssm_conv_b': nrm(ks[14], (DEPTH, SSM_CONV_DIM), 0.02),
        'ssm_dt_bias': dt0 + jnp.log(-jnp.expm1(-dt0)),
        'ssm_a_log': jnp.log(jax.random.uniform(ks[15], (DEPTH, SSM_HEADS), f32, 1.0, 16.0)),
        'ssm_d': 1.0 + nrm(ks[16], (DEPTH, SSM_HEADS), 0.1),
        'ssm_norm_w': 1.0 + nrm(ks[17], (DEPTH, SSM_D_INNER), 0.02),
        'w_ssm_out': nrm(ks[18], (DEPTH, SSM_D_INNER, D_MODEL), SSM_D_INNER ** -0.5),
        'norm_mem_w': 1.0 + nrm(ks[19], (DEPTH, D_MODEL), 0.02),
        'w_mem_k': nrm(ks[20], (DEPTH, D_MODEL, ATTN_DIM), D_MODEL ** -0.5),
        'w_mem_v': nrm(ks[21], (DEPTH, D_MODEL, ATTN_DIM), D_MODEL ** -0.5),
        'w_attn_o': nrm(ks[22], (DEPTH, ATTN_DIM, D_MODEL), ATTN_DIM ** -0.5),
        'w_merge_o': nrm(ks[23], (DEPTH, D_MODEL, D_MODEL), D_MODEL ** -0.5),
        'norm_ffn_w': 1.0 + nrm(ks[24], (DEPTH, D_MODEL), 0.02),
        'w_ffn_gate': nrm(ks[25], (DEPTH, D_MODEL, FFN_HIDDEN), D_MODEL ** -0.5),
        'w_ffn_up': nrm(ks[26], (DEPTH, D_MODEL, FFN_HIDDEN), D_MODEL ** -0.5),
        'w_ffn_down': nrm(ks[27], (DEPTH, FFN_HIDDEN, D_MODEL), FFN_HIDDEN ** -0.5),
        'norm_final_w': 1.0 + nrm(ks[28], (D_MODEL,), 0.02),
    }


def reference(x_prompt, x_sample, mem_prompt, cache_mem_k, cache_mem_v, state_conv, state_ssm_conv, state_ssm,
              norm_mix_w, w_in, sc_conv_w, w_sc_out, ssm_conv_w, ssm_conv_b, ssm_dt_bias, ssm_a_log, ssm_d,
              ssm_norm_w, w_ssm_out, norm_mem_w, w_mem_k, w_mem_v, w_attn_o, w_merge_o,
              norm_ffn_w, w_ffn_gate, w_ffn_up, w_ffn_down, norm_final_w):
    xp = x_prompt
    xs = x_sample
    bp = xp.shape[0]
    p_mk, p_mv, p_conv, p_ssmc, p_ssm = [], [], [], [], []
    s_conv, s_ssmc, s_ssm = [], [], []
    for l in range(DEPTH):
        layer_w = (w_in[l], sc_conv_w[l], w_sc_out[l], ssm_conv_w[l], ssm_conv_b[l], ssm_dt_bias[l], ssm_a_log[l],
                   ssm_d[l], ssm_norm_w[l], w_ssm_out[l], w_attn_o[l], w_merge_o[l])
        mem_n = rmsnorm(mem_prompt, norm_mem_w[l])
        mk = (mem_n @ w_mem_k[l]).reshape(bp, MEM_LEN, ATTN_HEADS, ATTN_HEAD_DIM)
        mv = (mem_n @ w_mem_v[l]).reshape(bp, MEM_LEN, ATTN_HEADS, ATTN_HEAD_DIM)
        h, (sc_p, ssmc_p, ssm_p) = mixer_block(
            rmsnorm(xp, norm_mix_w[l]), mk, mv,
            jnp.zeros((bp, SC_WIDTH - 1, SC_DIM), xp.dtype),
            jnp.zeros((bp, SSM_CONV - 1, SSM_CONV_DIM), xp.dtype),
            jnp.zeros((bp, SSM_HEADS, SSM_HEAD_DIM, SSM_STATE), xp.dtype),
            ssd_chunked, *layer_w)
        xp = xp + h
        xp = xp + swiglu(rmsnorm(xp, norm_ffn_w[l]), w_ffn_gate[l], w_ffn_up[l], w_ffn_down[l])
        p_mk.append(mk)
        p_mv.append(mv)
        p_conv.append(sc_p)
        p_ssmc.append(ssmc_p)
        p_ssm.append(ssm_p)
        h, (sc_s, ssmc_s, ssm_s) = mixer_block(
            rmsnorm(xs, norm_mix_w[l]), cache_mem_k[l], cache_mem_v[l],
            state_conv[l], state_ssm_conv[l], state_ssm[l],
            ssd_recurrent, *layer_w)
        xs = xs + h
        xs = xs + swiglu(rmsnorm(xs, norm_ffn_w[l]), w_ffn_gate[l], w_ffn_up[l], w_ffn_down[l])
        s_conv.append(sc_s)
        s_ssmc.append(ssmc_s)
        s_ssm.append(ssm_s)
    y_prompt = rmsnorm(xp, norm_final_w)
    y_sample = rmsnorm(xs, norm_final_w)
    return (y_prompt, y_sample, jnp.stack(p_mk), jnp.stack(p_mv), jnp.stack(p_conv), jnp.stack(p_ssmc),
            jnp.stack(p_ssm), jnp.stack(s_conv), jnp.stack(s_ssmc), jnp.stack(s_ssm))
```

```python
import functools

import jax
import jax.numpy as jnp
from jax import lax
from jax.experimental import pallas as pl
from jax.experimental.pallas import tpu as pltpu

F32 = jnp.float32
BF16 = jnp.bfloat16

D_MODEL = 1024
RMS_EPS = 1e-6
SC_DIM = D_MODEL
SC_WIDTH = 3
SSM_D_INNER = 2 * D_MODEL
SSM_HEAD_DIM = 64
SSM_HEADS = SSM_D_INNER // SSM_HEAD_DIM
SSM_STATE = 128
SSM_GROUPS = 4
SSM_CONV = 4
SSM_CHUNK = 128
SSM_BC = SSM_GROUPS * SSM_STATE
SSM_CONV_DIM = SSM_D_INNER + 2 * SSM_BC
HEADS_PER_GROUP = SSM_HEADS // SSM_GROUPS
GROUP_WIDTH = SSM_D_INNER // SSM_GROUPS
MEM_LEN = 256
ATTN_HEADS = 4
ATTN_HEAD_DIM = D_MODEL // ATTN_HEADS
FFN_HIDDEN = ((8 * D_MODEL // 3 + 255) // 256) * 256

V7X_LANES = 128
V7X_SUBLANES = 8
V7X_VMEM_BYTES = 64 * 1024 * 1024
VMEM_LIMIT_BYTES = V7X_VMEM_BYTES - 8 * 1024 * 1024

SSD_TILE = 256
MIX_TILE = 256
FFN_TILE = 512
SAMPLE_BLOCK = 8
CONV_COLS = 512


def _params(n_grid):
    return pltpu.CompilerParams(
        dimension_semantics=("arbitrary",) * n_grid,
        vmem_limit_bytes=VMEM_LIMIT_BYTES,
    )


def _whole(shape):
    nd = len(shape)
    return pl.BlockSpec(shape, lambda *_: (0,) * nd)


def _mm(a, b):
    return jnp.dot(a, b, preferred_element_type=F32)


def _mm_nt(a, b):
    return lax.dot_general(a, b, (((1,), (1,)), ((), ())), preferred_element_type=F32)


def _split3(x):
    hi = x.astype(BF16)
    r = x - hi.astype(F32)
    mid = r.astype(BF16)
    lo = (r - mid.astype(F32)).astype(BF16)
    return hi, mid, lo


def _mm_sel(sel, x):
    hi, mid, lo = _split3(x)
    return _mm(sel, hi) + _mm(sel, mid) + _mm(sel, lo)


def _mm_expand(x, sel):
    hi, mid, lo = _split3(x)
    return _mm(hi, sel) + _mm(mid, sel) + _mm(lo, sel)


def _rms(x, w):
    return x * lax.rsqrt(jnp.mean(x * x, axis=-1, keepdims=True) + RMS_EPS) * w


def _silu(x):
    return x * jax.nn.sigmoid(x)


def _softplus(x):
    return jnp.maximum(x, 0.0) + jnp.log1p(jnp.exp(-jnp.abs(x)))


def _shift_rows(u, prev8, k):
    r = pltpu.roll(u, k, axis=0)
    p = pltpu.roll(prev8, k, axis=0)
    row = lax.broadcasted_iota(jnp.int32, prev8.shape, 0)
    head = jnp.where(row < k, p, r[:V7X_SUBLANES])
    return jnp.concatenate([head, r[V7X_SUBLANES:]], axis=0)


def _softmax_rows(s):
    m = jnp.max(s, axis=-1, keepdims=True)
    p = jnp.exp(s - m)
    return p / jnp.sum(p, axis=-1, keepdims=True)


def _group_norm(y, w):
    outs = []
    for g in range(SSM_GROUPS):
        cols = slice(g * GROUP_WIDTH, (g + 1) * GROUP_WIDTH)
        yg = y[:, cols]
        ms = jnp.mean(yg * yg, axis=-1, keepdims=True)
        outs.append(yg * lax.rsqrt(ms + RMS_EPS) * w[:, cols])
    return jnp.concatenate(outs, axis=1)


def _mem_kernel(m_ref, nw_ref, wk_ref, wv_ref, k_ref, v_ref, kb_ref, vb_ref):
    mn = _rms(m_ref[...], nw_ref[...]).astype(BF16)
    k = _mm(mn, wk_ref[...])
    v = _mm(mn, wv_ref[...])
    k_ref[...] = k
    v_ref[...] = v
    kb_ref[...] = k.astype(BF16)
    vb_ref[...] = v.astype(BF16)


def _mem_call(mem, nw, wk, wv):
    b = mem.shape[0]
    blk = pl.BlockSpec((None, MEM_LEN, D_MODEL), lambda i: (i, 0, 0))
    return pl.pallas_call(
        _mem_kernel,
        grid=(b,),
        in_specs=[blk, _whole(nw.shape), _whole(wk.shape), _whole(wv.shape)],
        out_specs=[blk, blk, blk, blk],
        out_shape=[
            jax.ShapeDtypeStruct(mem.shape, F32),
            jax.ShapeDtypeStruct(mem.shape, F32),
            jax.ShapeDtypeStruct(mem.shape, BF16),
            jax.ShapeDtypeStruct(mem.shape, BF16),
        ],
        compiler_params=_params(1),
        name="mem_kv",
    )(mem, nw, wk, wv)


def _ssd_kernel(x_ref, nw_ref, wz_ref, wxbc_ref, wdt_ref, wgb_ref, cw_ref, cb_ref,
                dtb_ref, alog_ref, dful_ref, gnw_ref, wout_ref, e_ref,
                ybg_ref, cst_ref, sst_ref,
                hist_ref, st_ref, xn_s, z_s, xs_s, b_s, c_s, dt_s, da_s, yn_s):
    t = pl.program_id(1)
    tile = x_ref.shape[0]
    q = SSM_CHUNK

    @pl.when(t == 0)
    def _():
        hist_ref[...] = jnp.zeros_like(hist_ref)
        st_ref[...] = jnp.zeros_like(st_ref)

    xn = _rms(x_ref[...], nw_ref[...]).astype(BF16)
    xn_s[...] = xn
    z_s[...] = _mm(xn, wz_ref[...])

    for j in range(SSM_CONV_DIM // CONV_COLS):
        cols = slice(j * CONV_COLS, (j + 1) * CONV_COLS)
        u = _mm(xn, wxbc_ref[:, cols])
        prev8 = hist_ref[:, cols]
        hist_ref[:, cols] = u[tile - V7X_SUBLANES:]
        cw = cw_ref[:, cols]
        conv = _shift_rows(u, prev8, 3) * cw[0:1]
        conv = conv + _shift_rows(u, prev8, 2) * cw[1:2]
        conv = conv + _shift_rows(u, prev8, 1) * cw[2:3]
        conv = conv + u * cw[3:4]
        act = _silu(conv + cb_ref[:, cols])
        lo = j * CONV_COLS
        if lo < SSM_D_INNER:
            xs_s[:, cols] = act
        elif lo < SSM_D_INNER + SSM_BC:
            b_s[:, lo - SSM_D_INNER:lo - SSM_D_INNER + CONV_COLS] = act
        else:
            off = lo - SSM_D_INNER - SSM_BC
            c_s[:, off:off + CONV_COLS] = act

    dt = _softplus(_mm(xn, wdt_ref[...]) + dtb_ref[...])
    dt_s[...] = dt
    da_s[...] = dt * (-jnp.exp(alog_ref[...]))

    ri = lax.broadcasted_iota(jnp.int32, (q, q), 0)
    ci = lax.broadcasted_iota(jnp.int32, (q, q), 1)
    causal = ri >= ci
    tri = jnp.where(causal, 1.0, 0.0).astype(BF16)
    lane_lo = ci < SSM_HEAD_DIM
    sub8 = lax.broadcasted_iota(jnp.int32, (V7X_SUBLANES, q), 0)

    def chunk(c, carry):
        rows = pl.ds(pl.multiple_of(c * q, q), q)
        acum = _mm_sel(tri, da_s[rows, :])
        acum_t = acum.T
        last = acum[q - 1:q, :]
        dtc = dt_s[rows, :]
        dt_t = dtc.T
        w = dtc * jnp.exp(last - acum)
        wexp = _mm(w.astype(BF16), e_ref[...])
        xs = xs_s[rows, :]
        xw = (xs * wexp).astype(BF16)
        cd = jnp.where(sub8 == 0, jnp.exp(last), 0.0)
        dec = _mm_expand(cd, e_ref[...])[0:1, :]
        zc = z_s[rows, :]

        for g in range(SSM_GROUPS):
            gcols = slice(g * GROUP_WIDTH, (g + 1) * GROUP_WIDTH)
            bg = b_s[rows, g * SSM_STATE:(g + 1) * SSM_STATE]
            cg = c_s[rows, g * SSM_STATE:(g + 1) * SSM_STATE].astype(BF16)
            st_g = st_ref[:, gcols]
            yo_g = _mm(cg, st_g.astype(BF16))
            st_ref[:, gcols] = st_g * dec[:, gcols] + _mm(bg.T.astype(BF16), xw[:, gcols])
            cb_g = _mm_nt(cg, bg.astype(BF16))
            pairs = []
            for pq in range(HEADS_PER_GROUP // 2):
                h0 = g * HEADS_PER_GROUP + 2 * pq
                h1 = h0 + 1
                col0 = acum[:, h0:h0 + 1]
                col1 = acum[:, h1:h1 + 1]
                l0 = jnp.where(causal, jnp.exp(col0 - acum_t[h0:h0 + 1, :]), 0.0)
                l1 = jnp.where(causal, jnp.exp(col1 - acum_t[h1:h1 + 1, :]), 0.0)
                m0 = cb_g * l0 * dt_t[h0:h0 + 1, :]
                m1 = cb_g * l1 * dt_t[h1:h1 + 1, :]
                lhs = jnp.concatenate([m0, m1], axis=1).astype(BF16)
                pcols = slice(h0 * SSM_HEAD_DIM, (h1 + 1) * SSM_HEAD_DIM)
                xp = xs[:, pcols]
                rhs = jnp.concatenate(
                    [jnp.where(lane_lo, xp, 0.0), jnp.where(lane_lo, 0.0, xp)], axis=0
                ).astype(BF16)
                yd = _mm(lhs, rhs)
                sc = jnp.where(lane_lo, jnp.exp(col0), jnp.exp(col1))
                lc = slice(2 * pq * SSM_HEAD_DIM, (2 * pq + 2) * SSM_HEAD_DIM)
                pairs.append(yd + sc * yo_g[:, lc] + dful_ref[:, pcols] * xp)
            yg = jnp.concatenate(pairs, axis=1) * _silu(zc[:, gcols])
            ms = jnp.mean(yg * yg, axis=-1, keepdims=True)
            yn_s[rows, gcols] = (yg * lax.rsqrt(ms + RMS_EPS) * gnw_ref[:, gcols]).astype(BF16)
        return carry

    lax.fori_loop(0, tile // q, chunk, 0)

    gb = jax.nn.sigmoid(_mm(xn_s[...], wgb_ref[...]))
    ybg_ref[...] = gb * _mm(yn_s[...], wout_ref[...])

    @pl.when(t == pl.num_programs(1) - 1)
    def _():
        cst_ref[...] = pltpu.roll(hist_ref[...], SSM_CONV - 1, axis=0)[0:SSM_CONV - 1]
        for k in range(SSM_D_INNER // V7X_LANES):
            blk = slice(k * V7X_LANES, (k + 1) * V7X_LANES)
            sst_ref[blk, :] = st_ref[:, blk].T


def _ssd_call(x, nw, wz, wxbc, wdt, wgb, cw, cb, dtb, alog, dful, gnw, wout, e):
    b, s, d = x.shape
    tile = SSD_TILE
    consts = (nw, wz, wxbc, wdt, wgb, cw, cb, dtb, alog, dful, gnw, wout, e)
    return pl.pallas_call(
        _ssd_kernel,
        grid=(b, s // tile),
        in_specs=[pl.BlockSpec((None, tile, d), lambda i, j: (i, j, 0))]
        + [_whole(c.shape) for c in consts],
        out_specs=[
            pl.BlockSpec((None, tile, d), lambda i, j: (i, j, 0)),
            pl.BlockSpec((None, SSM_CONV - 1, SSM_CONV_DIM), lambda i, j: (i, 0, 0)),
            pl.BlockSpec((None, SSM_D_INNER, SSM_STATE), lambda i, j: (i, 0, 0)),
        ],
        out_shape=[
            jax.ShapeDtypeStruct((b, s, d), F32),
            jax.ShapeDtypeStruct((b, SSM_CONV - 1, SSM_CONV_DIM), F32),
            jax.ShapeDtypeStruct((b, SSM_D_INNER, SSM_STATE), F32),
        ],
        scratch_shapes=[
            pltpu.VMEM((V7X_SUBLANES, SSM_CONV_DIM), F32),
            pltpu.VMEM((SSM_STATE, SSM_D_INNER), F32),
            pltpu.VMEM((tile, d), BF16),
            pltpu.VMEM((tile, SSM_D_INNER), F32),
            pltpu.VMEM((tile, SSM_D_INNER), F32),
            pltpu.VMEM((tile, SSM_BC), F32),
            pltpu.VMEM((tile, SSM_BC), F32),
            pltpu.VMEM((tile, V7X_LANES), F32),
            pltpu.VMEM((tile, V7X_LANES), F32),
            pltpu.VMEM((tile, SSM_D_INNER), BF16),
        ],
        compiler_params=_params(2),
        name="ssd_prompt",
    )(x, *consts)


def _attention(q, k_ref, v_ref):
    outs = []
    for h in range(ATTN_HEADS):
        cols = slice(h * ATTN_HEAD_DIM, (h + 1) * ATTN_HEAD_DIM)
        s = _mm_nt(q[:, cols].astype(BF16), k_ref[:, cols]) * (ATTN_HEAD_DIM ** -0.5)
        outs.append(_mm(_softmax_rows(s).astype(BF16), v_ref[:, cols]))
    return jnp.concatenate(outs, axis=1)


def _mix_kernel(x_ref, ybg_ref, kb_ref, vb_ref, nw_ref, wa_ref, wq_ref, wg_ref, cw_ref,
                wsc_ref, wao_ref, wmo_ref, x1_ref, cst_ref, hist_ref):
    t = pl.program_id(1)
    tile = x_ref.shape[0]

    @pl.when(t == 0)
    def _():
        hist_ref[...] = jnp.zeros_like(hist_ref)

    x = x_ref[...]
    xn = _rms(x, nw_ref[...]).astype(BF16)

    sc_b = _mm(xn, wa_ref[:, 0:SC_DIM])
    u = _mm(xn, wa_ref[:, SC_DIM:2 * SC_DIM]) * _mm(xn, wa_ref[:, 2 * SC_DIM:3 * SC_DIM])
    prev8 = hist_ref[...]
    hist_ref[...] = u[tile - V7X_SUBLANES:]
    cw = cw_ref[...]
    conv = _shift_rows(u, prev8, 2) * cw[0:1]
    conv = conv + _shift_rows(u, prev8, 1) * cw[1:2]
    conv = conv + u * cw[2:3]
    y_a = _mm((sc_b * conv).astype(BF16), wsc_ref[...])

    att = _attention(_mm(xn, wq_ref[...]), kb_ref, vb_ref)
    y_c = _mm(att.astype(BF16), wao_ref[...])

    g_a = jax.nn.sigmoid(_mm(xn, wg_ref[:, 0:D_MODEL]))
    g_c = jax.nn.sigmoid(_mm(xn, wg_ref[:, D_MODEL:2 * D_MODEL]))
    merged = g_a * y_a + ybg_ref[...] + g_c * y_c
    x1_ref[...] = x + _mm(merged.astype(BF16), wmo_ref[...])

    @pl.when(t == pl.num_programs(1) - 1)
    def _():
        cst_ref[...] = pltpu.roll(hist_ref[...], SC_WIDTH - 1, axis=0)[0:SC_WIDTH - 1]


def _mix_call(x, ybg, kb, vb, nw, wa, wq, wg, cw, wsc, wao, wmo):
    b, s, d = x.shape
    tile = MIX_TILE
    consts = (nw, wa, wq, wg, cw, wsc, wao, wmo)
    tok = pl.BlockSpec((None, tile, d), lambda i, j: (i, j, 0))
    mem = pl.BlockSpec((None, MEM_LEN, d), lambda i, j: (i, 0, 0))
    return pl.pallas_call(
        _mix_kernel,
        grid=(b, s // tile),
        in_specs=[tok, tok, mem, mem] + [_whole(c.shape) for c in consts],
        out_specs=[tok, pl.BlockSpec((None, SC_WIDTH - 1, SC_DIM), lambda i, j: (i, 0, 0))],
        out_shape=[
            jax.ShapeDtypeStruct((b, s, d), F32),
            jax.ShapeDtypeStruct((b, SC_WIDTH - 1, SC_DIM), F32),
        ],
        scratch_shapes=[pltpu.VMEM((V7X_SUBLANES, SC_DIM), F32)],
        compiler_params=_params(2),
        name="mix_prompt",
    )(x, ybg, kb, vb, *consts)


def _ffn_body(x, nw, wg_ref, wu_ref, wd_ref, fw):
    xn = _rms(x, nw).astype(BF16)
    h = _silu(_mm(xn, wg_ref[...])) * _mm(xn, wu_ref[...])
    x2 = x + _mm(h.astype(BF16), wd_ref[...])
    return _rms(x2, fw)


def _ffn_kernel(x_ref, nw_ref, wg_ref, wu_ref, wd_ref, fw_ref, o_ref):
    o_ref[...] = _ffn_body(x_ref[...], nw_ref[...], wg_ref, wu_ref, wd_ref, fw_ref[...])


def _ffn_call(x, nw, wg, wu, wd, fw):
    b, s, d = x.shape
    tile = FFN_TILE
    consts = (nw, wg, wu, wd, fw)
    tok = pl.BlockSpec((None, tile, d), lambda i, j: (i, j, 0))
    return pl.pallas_call(
        _ffn_kernel,
        grid=(b, s // tile),
        in_specs=[tok] + [_whole(c.shape) for c in consts],
        out_specs=tok,
        out_shape=jax.ShapeDtypeStruct((b, s, d), F32),
        compiler_params=_params(2),
        name="ffn_prompt",
    )(x, *consts)


def _sfront_kernel(x_ref, sconv_ref, ssmc_ref, nw_ref, wa_ref, wz_ref, wxbc_ref, wdt_ref,
                   wq_ref, wg_ref, scw_ref, cw_ref, cb_ref, dtb_ref, alog_ref, wsc_ref, e_ref,
                   gaya_ref, gb_ref, gc_ref, z_ref, xs_ref, dtxt_ref, dect_ref, b_ref, c_ref,
                   q_ref, sconv_o, ssmc_o):
    xn = _rms(x_ref[...], nw_ref[...]).astype(BF16)

    sc_b = _mm(xn, wa_ref[:, 0:SC_DIM])
    u = _mm(xn, wa_ref[:, SC_DIM:2 * SC_DIM]) * _mm(xn, wa_ref[:, 2 * SC_DIM:3 * SC_DIM])
    h0 = sconv_ref[:, 0:SC_DIM]
    h1 = sconv_ref[:, SC_DIM:2 * SC_DIM]
    scw = scw_ref[...]
    conv = h0 * scw[0:1] + h1 * scw[1:2] + u * scw[2:3]
    sconv_o[:, 0:SC_DIM] = h1
    sconv_o[:, SC_DIM:2 * SC_DIM] = u
    y_a = _mm((sc_b * conv).astype(BF16), wsc_ref[...])
    gaya_ref[...] = jax.nn.sigmoid(_mm(xn, wg_ref[:, 0:D_MODEL])) * y_a
    gb_ref[...] = jax.nn.sigmoid(_mm(xn, wg_ref[:, D_MODEL:2 * D_MODEL]))
    gc_ref[...] = jax.nn.sigmoid(_mm(xn, wg_ref[:, 2 * D_MODEL:3 * D_MODEL]))
    z_ref[...] = _mm(xn, wz_ref[...])
    q_ref[...] = _mm(xn, wq_ref[...])

    dt = _softplus(_mm(xn, wdt_ref[...]) + dtb_ref[...])
    dect_ref[...] = jnp.exp(dt * (-jnp.exp(alog_ref[...]))).T
    dtexp = _mm_expand(dt, e_ref[...])

    w = SSM_CONV_DIM
    for j in range(SSM_CONV_DIM // CONV_COLS):
        lo = j * CONV_COLS
        cols = slice(lo, lo + CONV_COLS)
        xbc = _mm(xn, wxbc_ref[:, cols])
        p0 = ssmc_ref[:, lo:lo + CONV_COLS]
        p1 = ssmc_ref[:, w + lo:w + lo + CONV_COLS]
        p2 = ssmc_ref[:, 2 * w + lo:2 * w + lo + CONV_COLS]
        cw = cw_ref[:, cols]
        conv = p0 * cw[0:1] + p1 * cw[1:2] + p2 * cw[2:3] + xbc * cw[3:4]
        ssmc_o[:, lo:lo + CONV_COLS] = p1
        ssmc_o[:, w + lo:w + lo + CONV_COLS] = p2
        ssmc_o[:, 2 * w + lo:2 * w + lo + CONV_COLS] = xbc
        act = _silu(conv + cb_ref[:, cols])
        if lo < SSM_D_INNER:
            xs_ref[:, cols] = act
            dtx = act * dtexp[:, cols]
            for k in range(CONV_COLS // V7X_LANES):
                r0 = lo + k * V7X_LANES
                dtxt_ref[r0:r0 + V7X_LANES, :] = dtx[:, k * V7X_LANES:(k + 1) * V7X_LANES].T
        elif lo < SSM_D_INNER + SSM_BC:
            b_ref[:, lo - SSM_D_INNER:lo - SSM_D_INNER + CONV_COLS] = act
        else:
            off = lo - SSM_D_INNER - SSM_BC
            c_ref[:, off:off + CONV_COLS] = act


def _sfront_call(x, sconv, ssmc, nw, wa, wz, wxbc, wdt, wq, wg, scw, cw, cb, dtb, alog, wsc, e):
    n = x.shape[0]
    args = (x, sconv, ssmc, nw, wa, wz, wxbc, wdt, wq, wg, scw, cw, cb, dtb, alog, wsc, e)
    shapes = [
        (n, D_MODEL), (n, D_MODEL), (n, D_MODEL),
        (n, SSM_D_INNER), (n, SSM_D_INNER),
        (SSM_D_INNER, n), (V7X_LANES, n),
        (n, SSM_BC), (n, SSM_BC),
        (n, D_MODEL),
        sconv.shape, ssmc.shape,
    ]
    return pl.pallas_call(
        _sfront_kernel,
        grid=(1,),
        in_specs=[_whole(a.shape) for a in args],
        out_specs=[_whole(s) for s in shapes],
        out_shape=[jax.ShapeDtypeStruct(s, F32) for s in shapes],
        compiler_params=_params(1),
        name="sample_front",
    )(*args)


def _sssm_kernel(s_ref, dtxt_ref, dect_ref, b_ref, c_ref, so_ref, y_ref):
    i = pl.program_id(0)
    n = dtxt_ref.shape[1]
    lane = lax.broadcasted_iota(jnp.int32, (1, n), 1)
    sub = lax.broadcasted_iota(jnp.int32, (V7X_SUBLANES, SSM_STATE), 0)
    for k in range(SAMPLE_BLOCK):
        onehot = (lane == i * SAMPLE_BLOCK + k).astype(F32)
        xcol = jnp.sum(dtxt_ref[...] * onehot, axis=-1, keepdims=True)
        dcol = jnp.sum(dect_ref[0:SSM_HEADS, :] * onehot, axis=-1, keepdims=True)
        for h in range(SSM_HEADS):
            g = h // HEADS_PER_GROUP
            rows = slice(h * SSM_HEAD_DIM, (h + 1) * SSM_HEAD_DIM)
            brow = b_ref[k:k + 1, g * SSM_STATE:(g + 1) * SSM_STATE]
            so_ref[k, rows, :] = s_ref[k, rows, :] * dcol[h:h + 1, :] + xcol[rows, :] * brow
        c8 = jnp.zeros((V7X_SUBLANES, SSM_STATE), F32)
        for g in range(SSM_GROUPS):
            c8 = jnp.where(sub == g, c_ref[k:k + 1, g * SSM_STATE:(g + 1) * SSM_STATE], c8)
        y8 = _mm_nt(c8.astype(BF16), so_ref[k].astype(BF16))
        y_ref[k:k + 1, :] = jnp.concatenate(
            [y8[g:g + 1, g * GROUP_WIDTH:(g + 1) * GROUP_WIDTH] for g in range(SSM_GROUPS)], axis=1)


def _sssm_call(state, dtxt, dect, bm, cm):
    n = state.shape[0]
    blk = SAMPLE_BLOCK
    st = pl.BlockSpec((blk, SSM_D_INNER, SSM_STATE), lambda i: (i, 0, 0))
    return pl.pallas_call(
        _sssm_kernel,
        grid=(n // blk,),
        in_specs=[st, _whole(dtxt.shape), _whole(dect.shape),
                  pl.BlockSpec((blk, SSM_BC), lambda i: (i, 0)),
                  pl.BlockSpec((blk, SSM_BC), lambda i: (i, 0))],
        out_specs=[st, pl.BlockSpec((blk, SSM_D_INNER), lambda i: (i, 0))],
        out_shape=[jax.ShapeDtypeStruct(state.shape, F32),
                   jax.ShapeDtypeStruct((n, SSM_D_INNER), F32)],
        compiler_params=_params(1),
        name="sample_ssm",
    )(state, dtxt, dect, bm, cm)


def _sattn_kernel(q_ref, k_ref, v_ref, o_ref):
    sub = lax.broadcasted_iota(jnp.int32, (SAMPLE_BLOCK, ATTN_HEAD_DIM), 0)
    acc = [jnp.zeros((SAMPLE_BLOCK, ATTN_HEAD_DIM), F32) for _ in range(ATTN_HEADS)]
    for k in range(SAMPLE_BLOCK):
        for h in range(ATTN_HEADS):
            cols = slice(h * ATTN_HEAD_DIM, (h + 1) * ATTN_HEAD_DIM)
            kh = k_ref[k, :, cols].astype(BF16)
            vh = v_ref[k, :, cols].astype(BF16)
            s = _mm_nt(q_ref[:, cols].astype(BF16), kh) * (ATTN_HEAD_DIM ** -0.5)
            o = _mm(_softmax_rows(s).astype(BF16), vh)
            acc[h] = jnp.where(sub == k, o, acc[h])
    o_ref[...] = jnp.concatenate(acc, axis=1)


def _sattn_call(q, ck, cv):
    n = q.shape[0]
    blk = SAMPLE_BLOCK
    kv = pl.BlockSpec((blk, MEM_LEN, D_MODEL), lambda i: (i, 0, 0))
    row = pl.BlockSpec((blk, D_MODEL), lambda i: (i, 0))
    return pl.pallas_call(
        _sattn_kernel,
        grid=(n // blk,),
        in_specs=[row, kv, kv],
        out_specs=row,
        out_shape=jax.ShapeDtypeStruct((n, D_MODEL), F32),
        compiler_params=_params(1),
        name="sample_attn",
    )(q, ck, cv)


def _sback_kernel(x_ref, y_ref, xs_ref, z_ref, gaya_ref, gb_ref, gc_ref, att_ref, dful_ref,
                  gnw_ref, wout_ref, wao_ref, wmo_ref, fnw_ref, wg_ref, wu_ref, wd_ref, fw_ref,
                  o_ref):
    y = (y_ref[...] + dful_ref[...] * xs_ref[...]) * _silu(z_ref[...])
    y_b = _mm(_group_norm(y, gnw_ref[...]).astype(BF16), wout_ref[...])
    y_c = _mm(att_ref[...].astype(BF16), wao_ref[...])
    merged = gaya_ref[...] + gb_ref[...] * y_b + gc_ref[...] * y_c
    x1 = x_ref[...] + _mm(merged.astype(BF16), wmo_ref[...])
    o_ref[...] = _ffn_body(x1, fnw_ref[...], wg_ref, wu_ref, wd_ref, fw_ref[...])


def _sback_call(*args):
    n = args[0].shape[0]
    return pl.pallas_call(
        _sback_kernel,
        grid=(1,),
        in_specs=[_whole(a.shape) for a in args],
        out_specs=_whole((n, D_MODEL)),
        out_shape=jax.ShapeDtypeStruct((n, D_MODEL), F32),
        compiler_params=_params(1),
        name="sample_back",
    )(*args)


def kernel(x_prompt, x_sample, mem_prompt, cache_mem_k, cache_mem_v, state_conv, state_ssm_conv, state_ssm, norm_mix_w, w_in, sc_conv_w, w_sc_out, ssm_conv_w, ssm_conv_b, ssm_dt_bias, ssm_a_log, ssm_d, ssm_norm_w, w_ssm_out, norm_mem_w, w_mem_k, w_mem_v, w_attn_o, w_merge_o, norm_ffn_w, w_ffn_gate, w_ffn_up, w_ffn_down, norm_final_w):
    depth = w_in.shape[0]
    assert depth == 1
    bp = x_prompt.shape[0]
    ns = x_sample.shape[0]

    wi = w_in[0]
    o = 0
    wa = wi[:, o:o + 3 * SC_DIM].astype(BF16); o += 3 * SC_DIM
    wz = wi[:, o:o + SSM_D_INNER].astype(BF16); o += SSM_D_INNER
    wxbc = wi[:, o:o + SSM_CONV_DIM].astype(BF16); o += SSM_CONV_DIM
    wdt = jnp.pad(wi[:, o:o + SSM_HEADS], ((0, 0), (0, V7X_LANES - SSM_HEADS))).astype(BF16); o += SSM_HEADS
    wq = wi[:, o:o + D_MODEL].astype(BF16); o += D_MODEL
    wg = wi[:, o:o + 3 * D_MODEL].astype(BF16)
    wg_ac = jnp.concatenate([wg[:, 0:D_MODEL], wg[:, 2 * D_MODEL:]], axis=1)
    wg_b = wg[:, D_MODEL:2 * D_MODEL]
    row = lambda v: v.reshape(1, -1).astype(F32)
    pad_heads = lambda v: jnp.pad(row(v), ((0, 0), (0, V7X_LANES - SSM_HEADS)))
    nmix = row(norm_mix_w[0])
    dtb = pad_heads(ssm_dt_bias[0])
    alog = pad_heads(ssm_a_log[0])
    dful = row(jnp.repeat(ssm_d[0], SSM_HEAD_DIM))
    gnw = row(ssm_norm_w[0])
    cb = row(ssm_conv_b[0])
    cw = ssm_conv_w[0]
    scw = sc_conv_w[0]
    wsc = w_sc_out[0].astype(BF16)
    wout = w_ssm_out[0].astype(BF16)
    wao = w_attn_o[0].astype(BF16)
    wmo = w_merge_o[0].astype(BF16)
    wfg = w_ffn_gate[0].astype(BF16)
    wfu = w_ffn_up[0].astype(BF16)
    wfd = w_ffn_down[0].astype(BF16)
    nffn = row(norm_ffn_w[0])
    nfin = row(norm_final_w)
    expand = (jnp.arange(V7X_LANES)[:, None] == (jnp.arange(SSM_D_INNER)[None, :] // SSM_HEAD_DIM)).astype(BF16)

    mk, mv, mkb, mvb = _mem_call(mem_prompt, row(norm_mem_w[0]), w_mem_k[0].astype(BF16), w_mem_v[0].astype(BF16))
    ybg, p_ssmc, p_ssm = _ssd_call(x_prompt, nmix, wz, wxbc, wdt, wg_b, cw, cb, dtb, alog, dful, gnw, wout, expand)
    x1, p_conv = _mix_call(x_prompt, ybg, mkb, mvb, nmix, wa, wq, wg_ac, scw, wsc, wao, wmo)
    y_prompt = _ffn_call(x1, nffn, wfg, wfu, wfd, nfin)

    xs2 = x_sample.reshape(ns, D_MODEL)
    (gaya, gb, gc, z, xs, dtxt, dect, bm, cm, q, s_conv, s_ssmc) = _sfront_call(
        xs2, state_conv[0].reshape(ns, -1), state_ssm_conv[0].reshape(ns, -1),
        nmix, wa, wz, wxbc, wdt, wq, wg, scw, cw, cb, dtb, alog, wsc, expand)
    s_ssm, y_s = _sssm_call(state_ssm[0].reshape(ns, SSM_D_INNER, SSM_STATE), dtxt, dect, bm, cm)
    att = _sattn_call(q, cache_mem_k[0].reshape(ns, MEM_LEN, D_MODEL), cache_mem_v[0].reshape(ns, MEM_LEN, D_MODEL))
    y_sample = _sback_call(xs2, y_s, xs, z, gaya, gb, gc, att, dful, gnw, wout, wao, wmo,
                           nffn, wfg, wfu, wfd, nfin)

    kv_shape = (depth, bp, MEM_LEN, ATTN_HEADS, ATTN_HEAD_DIM)
    state_shape = (SSM_HEADS, SSM_HEAD_DIM, SSM_STATE)
    return (
        y_prompt,
        y_sample.reshape(ns, 1, D_MODEL),
        mk.reshape(kv_shape),
        mv.reshape(kv_shape),
        p_conv.reshape(depth, bp, SC_WIDTH - 1, SC_DIM),
        p_ssmc.reshape(depth, bp, SSM_CONV - 1, SSM_CONV_DIM),
        p_ssm.reshape((depth, bp) + state_shape),
        s_conv.reshape(depth, ns, SC_WIDTH - 1, SC_DIM),
        s_ssmc.reshape(depth, ns, SSM_CONV - 1, SSM_CONV_DIM),
        s_ssm.reshape((depth, ns) + state_shape),
    )
```

```python
import functools

import jax
import jax.numpy as jnp
from jax import lax
from jax.experimental import pallas as pl
from jax.experimental.pallas import tpu as pltpu

F32 = jnp.float32
BF16 = jnp.bfloat16

D_MODEL = 1024
RMS_EPS = 1e-6
SC_DIM = D_MODEL
SC_WIDTH = 3
SSM_D_INNER = 2 * D_MODEL
SSM_HEAD_DIM = 64
SSM_HEADS = SSM_D_INNER // SSM_HEAD_DIM
SSM_STATE = 128
SSM_GROUPS = 4
SSM_CONV = 4
SSM_CHUNK = 128
SSM_BC = SSM_GROUPS * SSM_STATE
SSM_CONV_DIM = SSM_D_INNER + 2 * SSM_BC
HEADS_PER_GROUP = SSM_HEADS // SSM_GROUPS
GROUP_WIDTH = SSM_D_INNER // SSM_GROUPS
MEM_LEN = 256
ATTN_HEADS = 4
ATTN_HEAD_DIM = D_MODEL // ATTN_HEADS
FFN_HIDDEN = ((8 * D_MODEL // 3 + 255) // 256) * 256

V7X_LANES = 128
V7X_SUBLANES = 8
V7X_VMEM_BYTES = 64 * 1024 * 1024
VMEM_LIMIT_BYTES = V7X_VMEM_BYTES - 8 * 1024 * 1024
CACHE_ROWS = ATTN_HEADS * ATTN_HEAD_DIM // V7X_LANES

SSD_TILE = 256
MIX_TILE = 256
FFN_TILE = 512
SAMPLE_BLOCK = 8
CONV_COLS = 512


def _params(n_grid):
    return pltpu.CompilerParams(
        dimension_semantics=("arbitrary",) * n_grid,
        vmem_limit_bytes=VMEM_LIMIT_BYTES,
    )


def _whole(shape):
    nd = len(shape)
    return pl.BlockSpec(shape, lambda *_: (0,) * nd)


def _mm(a, b):
    return jnp.dot(a, b, preferred_element_type=F32)


def _mm_nt(a, b):
    return lax.dot_general(a, b, (((1,), (1,)), ((), ())), preferred_element_type=F32)


def _split3(x):
    hi = x.astype(BF16)
    r = x - hi.astype(F32)
    mid = r.astype(BF16)
    lo = (r - mid.astype(F32)).astype(BF16)
    return hi, mid, lo


def _mm_sel(sel, x):
    hi, mid, lo = _split3(x)
    return _mm(sel, hi) + _mm(sel, mid) + _mm(sel, lo)


def _mm_expand(x, sel):
    hi, mid, lo = _split3(x)
    return _mm(hi, sel) + _mm(mid, sel) + _mm(lo, sel)


def _rms(x, w):
    return x * lax.rsqrt(jnp.mean(x * x, axis=-1, keepdims=True) + RMS_EPS) * w


def _silu(x):
    return x * jax.nn.sigmoid(x)


def _softplus(x):
    return jnp.maximum(x, 0.0) + jnp.log1p(jnp.exp(-jnp.abs(x)))


def _shift_rows(u, prev8, k):
    r = pltpu.roll(u, k, axis=0)
    p = pltpu.roll(prev8, k, axis=0)
    row = lax.broadcasted_iota(jnp.int32, prev8.shape, 0)
    head = jnp.where(row < k, p, r[:V7X_SUBLANES])
    return jnp.concatenate([head, r[V7X_SUBLANES:]], axis=0)


def _softmax_rows(s):
    m = jnp.max(s, axis=-1, keepdims=True)
    p = jnp.exp(s - m)
    return p / jnp.sum(p, axis=-1, keepdims=True)


def _group_norm(y, w):
    outs = []
    for g in range(SSM_GROUPS):
        cols = slice(g * GROUP_WIDTH, (g + 1) * GROUP_WIDTH)
        yg = y[:, cols]
        ms = jnp.mean(yg * yg, axis=-1, keepdims=True)
        outs.append(yg * lax.rsqrt(ms + RMS_EPS) * w[:, cols])
    return jnp.concatenate(outs, axis=1)


def _mem_kernel(m_ref, nw_ref, wk_ref, wv_ref, k_ref, v_ref, kb_ref, vb_ref):
    mn = _rms(m_ref[...], nw_ref[...]).astype(BF16)
    k = _mm(mn, wk_ref[...])
    v = _mm(mn, wv_ref[...])
    kb_ref[...] = k.astype(BF16)
    vb_ref[...] = v.astype(BF16)
    for j in range(ATTN_HEAD_DIM // V7X_LANES):
        for h in range(ATTN_HEADS):
            rows = pl.ds(j * ATTN_HEADS + h, MEM_LEN, stride=CACHE_ROWS)
            lo = h * ATTN_HEAD_DIM + j * V7X_LANES
            k_ref[rows, :] = k[:, lo:lo + V7X_LANES]
            v_ref[rows, :] = v[:, lo:lo + V7X_LANES]


def _mem_call(mem, nw, wk, wv):
    b = mem.shape[0]
    blk = pl.BlockSpec((None, MEM_LEN, D_MODEL), lambda i: (i, 0, 0))
    rows = pl.BlockSpec((None, MEM_LEN * CACHE_ROWS, V7X_LANES), lambda i: (i, 0, 0))
    rows_shape = jax.ShapeDtypeStruct((b, MEM_LEN * CACHE_ROWS, V7X_LANES), F32)
    return pl.pallas_call(
        _mem_kernel,
        grid=(b,),
        in_specs=[blk, _whole(nw.shape), _whole(wk.shape), _whole(wv.shape)],
        out_specs=[rows, rows, blk, blk],
        out_shape=[
            rows_shape,
            rows_shape,
            jax.ShapeDtypeStruct(mem.shape, BF16),
            jax.ShapeDtypeStruct(mem.shape, BF16),
        ],
        compiler_params=_params(1),
        name="mem_kv",
    )(mem, nw, wk, wv)


def _ssd_kernel(x_ref, nw_ref, wz_ref, wxbc_ref, wdt_ref, wgb_ref, cw_ref, cb_ref,
                dtb_ref, alog_ref, dful_ref, gnw_ref, wout_ref, e_ref,
                ybg_ref, cst_ref, sst_ref,
                hist_ref, st_ref, xn_s, z_s, xs_s, b_s, c_s, dt_s, da_s, yn_s):
    t = pl.program_id(1)
    tile = x_ref.shape[0]
    q = SSM_CHUNK

    @pl.when(t == 0)
    def _():
        hist_ref[...] = jnp.zeros_like(hist_ref)
        st_ref[...] = jnp.zeros_like(st_ref)

    xn = _rms(x_ref[...], nw_ref[...]).astype(BF16)
    xn_s[...] = xn
    z_s[...] = _mm(xn, wz_ref[...])

    for j in range(SSM_CONV_DIM // CONV_COLS):
        cols = slice(j * CONV_COLS, (j + 1) * CONV_COLS)
        u = _mm(xn, wxbc_ref[:, cols])
        prev8 = hist_ref[:, cols]
        hist_ref[:, cols] = u[tile - V7X_SUBLANES:]
        cw = cw_ref[:, cols]
        conv = _shift_rows(u, prev8, 3) * cw[0:1]
        conv = conv + _shift_rows(u, prev8, 2) * cw[1:2]
        conv = conv + _shift_rows(u, prev8, 1) * cw[2:3]
        conv = conv + u * cw[3:4]
        act = _silu(conv + cb_ref[:, cols])
        lo = j * CONV_COLS
        if lo < SSM_D_INNER:
            xs_s[:, cols] = act
        elif lo < SSM_D_INNER + SSM_BC:
            b_s[:, lo - SSM_D_INNER:lo - SSM_D_INNER + CONV_COLS] = act
        else:
            off = lo - SSM_D_INNER - SSM_BC
            c_s[:, off:off + CONV_COLS] = act

    dt = _softplus(_mm(xn, wdt_ref[...]) + dtb_ref[...])
    dt_s[...] = dt
    da_s[...] = dt * (-jnp.exp(alog_ref[...]))

    ri = lax.broadcasted_iota(jnp.int32, (q, q), 0)
    ci = lax.broadcasted_iota(jnp.int32, (q, q), 1)
    causal = ri >= ci
    tri = jnp.where(causal, 1.0, 0.0).astype(BF16)
    lane_lo = ci < SSM_HEAD_DIM
    sub8 = lax.broadcasted_iota(jnp.int32, (V7X_SUBLANES, q), 0)

    def chunk(c, carry):
        rows = pl.ds(pl.multiple_of(c * q, q), q)
        acum = _mm_sel(tri, da_s[rows, :])
        acum_t = acum.T
        last = acum[q - 1:q, :]
        dtc = dt_s[rows, :]
        dt_t = dtc.T
        w = dtc * jnp.exp(last - acum)
        wexp = _mm(w.astype(BF16), e_ref[...])
        xs = xs_s[rows, :]
        xw = (xs * wexp).astype(BF16)
        cd = jnp.where(sub8 == 0, jnp.exp(last), 0.0)
        dec = _mm_expand(cd, e_ref[...])[0:1, :]
        zc = z_s[rows, :]

        for g in range(SSM_GROUPS):
            gcols = slice(g * GROUP_WIDTH, (g + 1) * GROUP_WIDTH)
            bg = b_s[rows, g * SSM_STATE:(g + 1) * SSM_STATE]
            cg = c_s[rows, g * SSM_STATE:(g + 1) * SSM_STATE].astype(BF16)
            st_g = st_ref[:, gcols]
            yo_g = _mm(cg, st_g.astype(BF16))
            st_ref[:, gcols] = st_g * dec[:, gcols] + _mm(bg.T.astype(BF16), xw[:, gcols])
            cb_g = _mm_nt(cg, bg.astype(BF16))
            pairs = []
            for pq in range(HEADS_PER_GROUP // 2):
                h0 = g * HEADS_PER_GROUP + 2 * pq
                h1 = h0 + 1
                col0 = acum[:, h0:h0 + 1]
                col1 = acum[:, h1:h1 + 1]
                l0 = jnp.where(causal, jnp.exp(col0 - acum_t[h0:h0 + 1, :]), 0.0)
                l1 = jnp.where(causal, jnp.exp(col1 - acum_t[h1:h1 + 1, :]), 0.0)
                m0 = cb_g * l0 * dt_t[h0:h0 + 1, :]
                m1 = cb_g * l1 * dt_t[h1:h1 + 1, :]
                lhs = jnp.concatenate([m0, m1], axis=1).astype(BF16)
                pcols = slice(h0 * SSM_HEAD_DIM, (h1 + 1) * SSM_HEAD_DIM)
                xp = xs[:, pcols]
                rhs = jnp.concatenate(
                    [jnp.where(lane_lo, xp, 0.0), jnp.where(lane_lo, 0.0, xp)], axis=0
                ).astype(BF16)
                yd = _mm(lhs, rhs)
                sc = jnp.where(lane_lo, jnp.exp(col0), jnp.exp(col1))
                lc = slice(2 * pq * SSM_HEAD_DIM, (2 * pq + 2) * SSM_HEAD_DIM)
                pairs.append(yd + sc * yo_g[:, lc] + dful_ref[:, pcols] * xp)
            yg = jnp.concatenate(pairs, axis=1) * _silu(zc[:, gcols])
            ms = jnp.mean(yg * yg, axis=-1, keepdims=True)
            yn_s[rows, gcols] = (yg * lax.rsqrt(ms + RMS_EPS) * gnw_ref[:, gcols]).astype(BF16)
        return carry

    lax.fori_loop(0, tile // q, chunk, 0)

    gb = jax.nn.sigmoid(_mm(xn_s[...], wgb_ref[...]))
    ybg_ref[...] = gb * _mm(yn_s[...], wout_ref[...])

    @pl.when(t == pl.num_programs(1) - 1)
    def _():
        cst_ref[...] = pltpu.roll(hist_ref[...], SSM_CONV - 1, axis=0)[0:SSM_CONV - 1]
        for k in range(SSM_D_INNER // V7X_LANES):
            blk = slice(k * V7X_LANES, (k + 1) * V7X_LANES)
            sst_ref[blk, :] = st_ref[:, blk].T


def _ssd_call(x, nw, wz, wxbc, wdt, wgb, cw, cb, dtb, alog, dful, gnw, wout, e):
    b, s, d = x.shape
    tile = SSD_TILE
    consts = (nw, wz, wxbc, wdt, wgb, cw, cb, dtb, alog, dful, gnw, wout, e)
    return pl.pallas_call(
        _ssd_kernel,
        grid=(b, s // tile),
        in_specs=[pl.BlockSpec((None, tile, d), lambda i, j: (i, j, 0))]
        + [_whole(c.shape) for c in consts],
        out_specs=[
            pl.BlockSpec((None, tile, d), lambda i, j: (i, j, 0)),
            pl.BlockSpec((None, SSM_CONV - 1, SSM_CONV_DIM), lambda i, j: (i, 0, 0)),
            pl.BlockSpec((None, SSM_D_INNER, SSM_STATE), lambda i, j: (i, 0, 0)),
        ],
        out_shape=[
            jax.ShapeDtypeStruct((b, s, d), F32),
            jax.ShapeDtypeStruct((b, SSM_CONV - 1, SSM_CONV_DIM), F32),
            jax.ShapeDtypeStruct((b, SSM_D_INNER, SSM_STATE), F32),
        ],
        scratch_shapes=[
            pltpu.VMEM((V7X_SUBLANES, SSM_CONV_DIM), F32),
            pltpu.VMEM((SSM_STATE, SSM_D_INNER), F32),
            pltpu.VMEM((tile, d), BF16),
            pltpu.VMEM((tile, SSM_D_INNER), F32),
            pltpu.VMEM((tile, SSM_D_INNER), F32),
            pltpu.VMEM((tile, SSM_BC), F32),
            pltpu.VMEM((tile, SSM_BC), F32),
            pltpu.VMEM((tile, V7X_LANES), F32),
            pltpu.VMEM((tile, V7X_LANES), F32),
            pltpu.VMEM((tile, SSM_D_INNER), BF16),
        ],
        compiler_params=_params(2),
        name="ssd_prompt",
    )(x, *consts)


def _attention(q, k_ref, v_ref):
    outs = []
    for h in range(ATTN_HEADS):
        cols = slice(h * ATTN_HEAD_DIM, (h + 1) * ATTN_HEAD_DIM)
        s = _mm_nt(q[:, cols].astype(BF16), k_ref[:, cols]) * (ATTN_HEAD_DIM ** -0.5)
        outs.append(_mm(_softmax_rows(s).astype(BF16), v_ref[:, cols]))
    return jnp.concatenate(outs, axis=1)


def _mix_kernel(x_ref, ybg_ref, kb_ref, vb_ref, nw_ref, wa_ref, wq_ref, wg_ref, cw_ref,
                wsc_ref, wao_ref, wmo_ref, x1_ref, cst_ref, hist_ref):
    t = pl.program_id(1)
    tile = x_ref.shape[0]

    @pl.when(t == 0)
    def _():
        hist_ref[...] = jnp.zeros_like(hist_ref)

    x = x_ref[...]
    xn = _rms(x, nw_ref[...]).astype(BF16)

    sc_b = _mm(xn, wa_ref[:, 0:SC_DIM])
    u = _mm(xn, wa_ref[:, SC_DIM:2 * SC_DIM]) * _mm(xn, wa_ref[:, 2 * SC_DIM:3 * SC_DIM])
    prev8 = hist_ref[...]
    hist_ref[...] = u[tile - V7X_SUBLANES:]
    cw = cw_ref[...]
    conv = _shift_rows(u, prev8, 2) * cw[0:1]
    conv = conv + _shift_rows(u, prev8, 1) * cw[1:2]
    conv = conv + u * cw[2:3]
    y_a = _mm((sc_b * conv).astype(BF16), wsc_ref[...])

    att = _attention(_mm(xn, wq_ref[...]), kb_ref, vb_ref)
    y_c = _mm(att.astype(BF16), wao_ref[...])

    g_a = jax.nn.sigmoid(_mm(xn, wg_ref[:, 0:D_MODEL]))
    g_c = jax.nn.sigmoid(_mm(xn, wg_ref[:, D_MODEL:2 * D_MODEL]))
    merged = g_a * y_a + ybg_ref[...] + g_c * y_c
    x1_ref[...] = x + _mm(merged.astype(BF16), wmo_ref[...])

    @pl.when(t == pl.num_programs(1) - 1)
    def _():
        cst_ref[...] = pltpu.roll(hist_ref[...], SC_WIDTH - 1, axis=0)[0:SC_WIDTH - 1]


def _mix_call(x, ybg, kb, vb, nw, wa, wq, wg, cw, wsc, wao, wmo):
    b, s, d = x.shape
    tile = MIX_TILE
    consts = (nw, wa, wq, wg, cw, wsc, wao, wmo)
    tok = pl.BlockSpec((None, tile, d), lambda i, j: (i, j, 0))
    mem = pl.BlockSpec((None, MEM_LEN, d), lambda i, j: (i, 0, 0))
    return pl.pallas_call(
        _mix_kernel,
        grid=(b, s // tile),
        in_specs=[tok, tok, mem, mem] + [_whole(c.shape) for c in consts],
        out_specs=[tok, pl.BlockSpec((None, SC_WIDTH - 1, SC_DIM), lambda i, j: (i, 0, 0))],
        out_shape=[
            jax.ShapeDtypeStruct((b, s, d), F32),
            jax.ShapeDtypeStruct((b, SC_WIDTH - 1, SC_DIM), F32),
        ],
        scratch_shapes=[pltpu.VMEM((V7X_SUBLANES, SC_DIM), F32)],
        compiler_params=_params(2),
        name="mix_prompt",
    )(x, ybg, kb, vb, *consts)


def _ffn_body(x, nw, wg_ref, wu_ref, wd_ref, fw):
    xn = _rms(x, nw).astype(BF16)
    h = _silu(_mm(xn, wg_ref[...])) * _mm(xn, wu_ref[...])
    x2 = x + _mm(h.astype(BF16), wd_ref[...])
    return _rms(x2, fw)


def _ffn_kernel(x_ref, nw_ref, wg_ref, wu_ref, wd_ref, fw_ref, o_ref):
    o_ref[...] = _ffn_body(x_ref[...], nw_ref[...], wg_ref, wu_ref, wd_ref, fw_ref[...])


def _ffn_call(x, nw, wg, wu, wd, fw):
    b, s, d = x.shape
    tile = FFN_TILE
    consts = (nw, wg, wu, wd, fw)
    tok = pl.BlockSpec((None, tile, d), lambda i, j: (i, j, 0))
    return pl.pallas_call(
        _ffn_kernel,
        grid=(b, s // tile),
        in_specs=[tok] + [_whole(c.shape) for c in consts],
        out_specs=tok,
        out_shape=jax.ShapeDtypeStruct((b, s, d), F32),
        compiler_params=_params(2),
        name="ffn_prompt",
    )(x, *consts)


def _sfront_kernel(x_ref, sconv_ref, ssmc_ref, nw_ref, wa_ref, wz_ref, wxbc_ref, wdt_ref,
                   wq_ref, wg_ref, scw_ref, cw_ref, cb_ref, dtb_ref, alog_ref, wsc_ref, e_ref,
                   gaya_ref, gb_ref, gc_ref, z_ref, xs_ref, dtxt_ref, dect_ref, b_ref, c_ref,
                   q_ref, sconv_o, ssmc_o):
    xn = _rms(x_ref[...], nw_ref[...]).astype(BF16)

    sc_b = _mm(xn, wa_ref[:, 0:SC_DIM])
    u = _mm(xn, wa_ref[:, SC_DIM:2 * SC_DIM]) * _mm(xn, wa_ref[:, 2 * SC_DIM:3 * SC_DIM])
    h0 = sconv_ref[:, 0:SC_DIM]
    h1 = sconv_ref[:, SC_DIM:2 * SC_DIM]
    scw = scw_ref[...]
    conv = h0 * scw[0:1] + h1 * scw[1:2] + u * scw[2:3]
    sconv_o[:, 0:SC_DIM] = h1
    sconv_o[:, SC_DIM:2 * SC_DIM] = u
    y_a = _mm((sc_b * conv).astype(BF16), wsc_ref[...])
    gaya_ref[...] = jax.nn.sigmoid(_mm(xn, wg_ref[:, 0:D_MODEL])) * y_a
    gb_ref[...] = jax.nn.sigmoid(_mm(xn, wg_ref[:, D_MODEL:2 * D_MODEL]))
    gc_ref[...] = jax.nn.sigmoid(_mm(xn, wg_ref[:, 2 * D_MODEL:3 * D_MODEL]))
    z_ref[...] = _mm(xn, wz_ref[...])
    q_ref[...] = _mm(xn, wq_ref[...])

    dt = _softplus(_mm(xn, wdt_ref[...]) + dtb_ref[...])
    dect_ref[...] = jnp.exp(dt * (-jnp.exp(alog_ref[...]))).T
    dtexp = _mm_expand(dt, e_ref[...])

    w = SSM_CONV_DIM
    for j in range(SSM_CONV_DIM // CONV_COLS):
        lo = j * CONV_COLS
        cols = slice(lo, lo + CONV_COLS)
        xbc = _mm(xn, wxbc_ref[:, cols])
        p0 = ssmc_ref[:, lo:lo + CONV_COLS]
        p1 = ssmc_ref[:, w + lo:w + lo + CONV_COLS]
        p2 = ssmc_ref[:, 2 * w + lo:2 * w + lo + CONV_COLS]
        cw = cw_ref[:, cols]
        conv = p0 * cw[0:1] + p1 * cw[1:2] + p2 * cw[2:3] + xbc * cw[3:4]
        ssmc_o[:, lo:lo + CONV_COLS] = p1
        ssmc_o[:, w + lo:w + lo + CONV_COLS] = p2
        ssmc_o[:, 2 * w + lo:2 * w + lo + CONV_COLS] = xbc
        act = _silu(conv + cb_ref[:, cols])
        if lo < SSM_D_INNER:
            xs_ref[:, cols] = act
            dtx = act * dtexp[:, cols]
            for k in range(CONV_COLS // V7X_LANES):
                r0 = lo + k * V7X_LANES
                dtxt_ref[r0:r0 + V7X_LANES, :] = dtx[:, k * V7X_LANES:(k + 1) * V7X_LANES].T
        elif lo < SSM_D_INNER + SSM_BC:
            b_ref[:, lo - SSM_D_INNER:lo - SSM_D_INNER + CONV_COLS] = act
        else:
            off = lo - SSM_D_INNER - SSM_BC
            c_ref[:, off:off + CONV_COLS] = act


def _sfront_call(x, sconv, ssmc, nw, wa, wz, wxbc, wdt, wq, wg, scw, cw, cb, dtb, alog, wsc, e):
    n = x.shape[0]
    args = (x, sconv, ssmc, nw, wa, wz, wxbc, wdt, wq, wg, scw, cw, cb, dtb, alog, wsc, e)
    shapes = [
        (n, D_MODEL), (n, D_MODEL), (n, D_MODEL),
        (n, SSM_D_INNER), (n, SSM_D_INNER),
        (SSM_D_INNER, n), (V7X_LANES, n),
        (n, SSM_BC), (n, SSM_BC),
        (n, D_MODEL),
        sconv.shape, ssmc.shape,
    ]
    return pl.pallas_call(
        _sfront_kernel,
        grid=(1,),
        in_specs=[_whole(a.shape) for a in args],
        out_specs=[_whole(s) for s in shapes],
        out_shape=[jax.ShapeDtypeStruct(s, F32) for s in shapes],
        compiler_params=_params(1),
        name="sample_front",
    )(*args)


def _sssm_kernel(s_ref, dtxt_ref, dect_ref, b_ref, c_ref, so_ref, y_ref):
    i = pl.program_id(0)
    n = dtxt_ref.shape[1]
    lane = lax.broadcasted_iota(jnp.int32, (1, n), 1)
    sub = lax.broadcasted_iota(jnp.int32, (V7X_SUBLANES, SSM_STATE), 0)
    for k in range(SAMPLE_BLOCK):
        onehot = (lane == i * SAMPLE_BLOCK + k).astype(F32)
        xcol = jnp.sum(dtxt_ref[...] * onehot, axis=-1, keepdims=True)
        dcol = jnp.sum(dect_ref[0:SSM_HEADS, :] * onehot, axis=-1, keepdims=True)
        for h in range(SSM_HEADS):
            g = h // HEADS_PER_GROUP
            rows = slice(h * SSM_HEAD_DIM, (h + 1) * SSM_HEAD_DIM)
            brow = b_ref[k:k + 1, g * SSM_STATE:(g + 1) * SSM_STATE]
            so_ref[k, rows, :] = s_ref[k, rows, :] * dcol[h:h + 1, :] + xcol[rows, :] * brow
        c8 = jnp.zeros((V7X_SUBLANES, SSM_STATE), F32)
        for g in range(SSM_GROUPS):
            c8 = jnp.where(sub == g, c_ref[k:k + 1, g * SSM_STATE:(g + 1) * SSM_STATE], c8)
        y8 = _mm_nt(c8.astype(BF16), so_ref[k].astype(BF16))
        y_ref[k:k + 1, :] = jnp.concatenate(
            [y8[g:g + 1, g * GROUP_WIDTH:(g + 1) * GROUP_WIDTH] for g in range(SSM_GROUPS)], axis=1)


def _sssm_call(state, dtxt, dect, bm, cm):
    n = state.shape[0]
    blk = SAMPLE_BLOCK
    st = pl.BlockSpec((blk, SSM_D_INNER, SSM_STATE), lambda i: (i, 0, 0))
    return pl.pallas_call(
        _sssm_kernel,
        grid=(n // blk,),
        in_specs=[st, _whole(dtxt.shape), _whole(dect.shape),
                  pl.BlockSpec((blk, SSM_BC), lambda i: (i, 0)),
                  pl.BlockSpec((blk, SSM_BC), lambda i: (i, 0))],
        out_specs=[st, pl.BlockSpec((blk, SSM_D_INNER), lambda i: (i, 0))],
        out_shape=[jax.ShapeDtypeStruct(state.shape, F32),
                   jax.ShapeDtypeStruct((n, SSM_D_INNER), F32)],
        compiler_params=_params(1),
        name="sample_ssm",
    )(state, dtxt, dect, bm, cm)


def _cache_rows(c):
    n = c.shape[1]
    c = c.reshape(n, MEM_LEN, ATTN_HEADS, ATTN_HEAD_DIM // V7X_LANES, V7X_LANES)
    return c.transpose(0, 1, 3, 2, 4).reshape(n * MEM_LEN * CACHE_ROWS, V7X_LANES)


def _cache_head(ref, k, h):
    base = k * MEM_LEN * CACHE_ROWS
    halves = [ref[pl.ds(base + j * ATTN_HEADS + h, MEM_LEN, stride=CACHE_ROWS), :]
              for j in range(ATTN_HEAD_DIM // V7X_LANES)]
    return jnp.concatenate(halves, axis=1).astype(BF16)


def _sattn_kernel(q_ref, k_ref, v_ref, o_ref):
    sub = lax.broadcasted_iota(jnp.int32, (SAMPLE_BLOCK, MEM_LEN), 0)
    sub_o = lax.broadcasted_iota(jnp.int32, (SAMPLE_BLOCK, ATTN_HEAD_DIM), 0)
    outs = []
    for h in range(ATTN_HEADS):
        qh = q_ref[:, h * ATTN_HEAD_DIM:(h + 1) * ATTN_HEAD_DIM].astype(BF16)
        s = jnp.zeros((SAMPLE_BLOCK, MEM_LEN), F32)
        for k in range(SAMPLE_BLOCK):
            s = jnp.where(sub == k, _mm_nt(qh, _cache_head(k_ref, k, h)), s)
        p = _softmax_rows(s * (ATTN_HEAD_DIM ** -0.5)).astype(BF16)
        o = jnp.zeros((SAMPLE_BLOCK, ATTN_HEAD_DIM), F32)
        for k in range(SAMPLE_BLOCK):
            o = jnp.where(sub_o == k, _mm(p, _cache_head(v_ref, k, h)), o)
        outs.append(o)
    o_ref[...] = jnp.concatenate(outs, axis=1)


def _sattn_call(q, ck, cv):
    n = q.shape[0]
    blk = SAMPLE_BLOCK
    kv = pl.BlockSpec((blk * MEM_LEN * CACHE_ROWS, V7X_LANES), lambda i: (i, 0))
    row = pl.BlockSpec((blk, D_MODEL), lambda i: (i, 0))
    return pl.pallas_call(
        _sattn_kernel,
        grid=(n // blk,),
        in_specs=[row, kv, kv],
        out_specs=row,
        out_shape=jax.ShapeDtypeStruct((n, D_MODEL), F32),
        compiler_params=_params(1),
        name="sample_attn",
    )(q, _cache_rows(ck), _cache_rows(cv))


def _sback_kernel(x_ref, y_ref, xs_ref, z_ref, gaya_ref, gb_ref, gc_ref, att_ref, dful_ref,
                  gnw_ref, wout_ref, wao_ref, wmo_ref, fnw_ref, wg_ref, wu_ref, wd_ref, fw_ref,
                  o_ref):
    y = (y_ref[...] + dful_ref[...] * xs_ref[...]) * _silu(z_ref[...])
    y_b = _mm(_group_norm(y, gnw_ref[...]).astype(BF16), wout_ref[...])
    y_c = _mm(att_ref[...].astype(BF16), wao_ref[...])
    merged = gaya_ref[...] + gb_ref[...] * y_b + gc_ref[...] * y_c
    x1 = x_ref[...] + _mm(merged.astype(BF16), wmo_ref[...])
    o_ref[...] = _ffn_body(x1, fnw_ref[...], wg_ref, wu_ref, wd_ref, fw_ref[...])


def _sback_call(*args):
    n = args[0].shape[0]
    return pl.pallas_call(
        _sback_kernel,
        grid=(1,),
        in_specs=[_whole(a.shape) for a in args],
        out_specs=_whole((n, D_MODEL)),
        out_shape=jax.ShapeDtypeStruct((n, D_MODEL), F32),
        compiler_params=_params(1),
        name="sample_back",
    )(*args)


def kernel(x_prompt, x_sample, mem_prompt, cache_mem_k, cache_mem_v, state_conv, state_ssm_conv, state_ssm, norm_mix_w, w_in, sc_conv_w, w_sc_out, ssm_conv_w, ssm_conv_b, ssm_dt_bias, ssm_a_log, ssm_d, ssm_norm_w, w_ssm_out, norm_mem_w, w_mem_k, w_mem_v, w_attn_o, w_merge_o, norm_ffn_w, w_ffn_gate, w_ffn_up, w_ffn_down, norm_final_w):
    depth = w_in.shape[0]
    assert depth == 1
    bp = x_prompt.shape[0]
    ns = x_sample.shape[0]

    wi = w_in[0]
    o = 0
    wa = wi[:, o:o + 3 * SC_DIM].astype(BF16); o += 3 * SC_DIM
    wz = wi[:, o:o + SSM_D_INNER].astype(BF16); o += SSM_D_INNER
    wxbc = wi[:, o:o + SSM_CONV_DIM].astype(BF16); o += SSM_CONV_DIM
    wdt = jnp.pad(wi[:, o:o + SSM_HEADS], ((0, 0), (0, V7X_LANES - SSM_HEADS))).astype(BF16); o += SSM_HEADS
    wq = wi[:, o:o + D_MODEL].astype(BF16); o += D_MODEL
    wg = wi[:, o:o + 3 * D_MODEL].astype(BF16)
    wg_ac = jnp.concatenate([wg[:, 0:D_MODEL], wg[:, 2 * D_MODEL:]], axis=1)
    wg_b = wg[:, D_MODEL:2 * D_MODEL]
    row = lambda v: v.reshape(1, -1).astype(F32)
    pad_heads = lambda v: jnp.pad(row(v), ((0, 0), (0, V7X_LANES - SSM_HEADS)))
    nmix = row(norm_mix_w[0])
    dtb = pad_heads(ssm_dt_bias[0])
    alog = pad_heads(ssm_a_log[0])
    dful = row(jnp.repeat(ssm_d[0], SSM_HEAD_DIM))
    gnw = row(ssm_norm_w[0])
    cb = row(ssm_conv_b[0])
    cw = ssm_conv_w[0]
    scw = sc_conv_w[0]
    wsc = w_sc_out[0].astype(BF16)
    wout = w_ssm_out[0].astype(BF16)
    wao = w_attn_o[0].astype(BF16)
    wmo = w_merge_o[0].astype(BF16)
    wfg = w_ffn_gate[0].astype(BF16)
    wfu = w_ffn_up[0].astype(BF16)
    wfd = w_ffn_down[0].astype(BF16)
    nffn = row(norm_ffn_w[0])
    nfin = row(norm_final_w)
    expand = (jnp.arange(V7X_LANES)[:, None] == (jnp.arange(SSM_D_INNER)[None, :] // SSM_HEAD_DIM)).astype(BF16)

    mk, mv, mkb, mvb = _mem_call(mem_prompt, row(norm_mem_w[0]), w_mem_k[0].astype(BF16), w_mem_v[0].astype(BF16))
    ybg, p_ssmc, p_ssm = _ssd_call(x_prompt, nmix, wz, wxbc, wdt, wg_b, cw, cb, dtb, alog, dful, gnw, wout, expand)
    x1, p_conv = _mix_call(x_prompt, ybg, mkb, mvb, nmix, wa, wq, wg_ac, scw, wsc, wao, wmo)
    y_prompt = _ffn_call(x1, nffn, wfg, wfu, wfd, nfin)

    xs2 = x_sample.reshape(ns, D_MODEL)
    (gaya, gb, gc, z, xs, dtxt, dect, bm, cm, q, s_conv, s_ssmc) = _sfront_call(
        xs2, state_conv[0].reshape(ns, -1), state_ssm_conv[0].reshape(ns, -1),
        nmix, wa, wz, wxbc, wdt, wq, wg, scw, cw, cb, dtb, alog, wsc, expand)
    s_ssm, y_s = _sssm_call(state_ssm[0].reshape(ns, SSM_D_INNER, SSM_STATE), dtxt, dect, bm, cm)
    att = _sattn_call(q, cache_mem_k, cache_mem_v)
    y_sample = _sback_call(xs2, y_s, xs, z, gaya, gb, gc, att, dful, gnw, wout, wao, wmo,
                           nffn, wfg, wfu, wfd, nfin)

    def from_rows(r):
        r = r.reshape(bp, MEM_LEN, ATTN_HEAD_DIM // V7X_LANES, ATTN_HEADS, V7X_LANES)
        return r.transpose(0, 1, 3, 2, 4).reshape(depth, bp, MEM_LEN, ATTN_HEADS, ATTN_HEAD_DIM)

    state_shape = (SSM_HEADS, SSM_HEAD_DIM, SSM_STATE)
    return (
        y_prompt,
        y_sample.reshape(ns, 1, D_MODEL),
        from_rows(mk),
        from_rows(mv),
        p_conv.reshape(depth, bp, SC_WIDTH - 1, SC_DIM),
        p_ssmc.reshape(depth, bp, SSM_CONV - 1, SSM_CONV_DIM),
        p_ssm.reshape((depth, bp) + state_shape),
        s_conv.reshape(depth, ns, SC_WIDTH - 1, SC_DIM),
        s_ssmc.reshape(depth, ns, SSM_CONV - 1, SSM_CONV_DIM),
        s_ssm.reshape((depth, ns) + state_shape),
    )
```

```python
import functools

import jax
import jax.numpy as jnp
from jax import lax
from jax.experimental import pallas as pl
from jax.experimental.pallas import tpu as pltpu

F32 = jnp.float32
BF16 = jnp.bfloat16

D_MODEL = 1024
RMS_EPS = 1e-6
LOG2_E = 1.4426950408889634
SC_DIM = D_MODEL
SC_WIDTH = 3
SSM_D_INNER = 2 * D_MODEL
SSM_HEAD_DIM = 64
SSM_HEADS = SSM_D_INNER // SSM_HEAD_DIM
SSM_STATE = 128
SSM_GROUPS = 4
SSM_CONV = 4
SSM_CHUNK = 128
SSM_BC = SSM_GROUPS * SSM_STATE
SSM_CONV_DIM = SSM_D_INNER + 2 * SSM_BC
HEADS_PER_GROUP = SSM_HEADS // SSM_GROUPS
GROUP_WIDTH = SSM_D_INNER // SSM_GROUPS
MEM_LEN = 256
ATTN_HEADS = 4
ATTN_HEAD_DIM = D_MODEL // ATTN_HEADS
FFN_HIDDEN = ((8 * D_MODEL // 3 + 255) // 256) * 256

V7X_LANES = 128
V7X_SUBLANES = 8
V7X_VMEM_BYTES = 64 * 1024 * 1024
VMEM_LIMIT_BYTES = V7X_VMEM_BYTES - 8 * 1024 * 1024
CACHE_ROWS = ATTN_HEADS * ATTN_HEAD_DIM // V7X_LANES

SSD_TILE = 256
SSD_SUB = 256
MIX_TILE = 512
FFN_TILE = 512
SAMPLE_BLOCK = 8
CONV_COLS = 512


def _params(n_grid, flags=None):
    return pltpu.CompilerParams(
        dimension_semantics=("arbitrary",) * n_grid,
        vmem_limit_bytes=VMEM_LIMIT_BYTES,
        flags=flags,
    )


def _whole(shape):
    nd = len(shape)
    return pl.BlockSpec(shape, lambda *_: (0,) * nd)


def _mm(a, b):
    return jnp.dot(a, b, preferred_element_type=F32)


def _mm_nt(a, b):
    return lax.dot_general(a, b, (((1,), (1,)), ((), ())), preferred_element_type=F32)


def _split3(x):
    hi = x.astype(BF16)
    r = x - hi.astype(F32)
    mid = r.astype(BF16)
    lo = (r - mid.astype(F32)).astype(BF16)
    return hi, mid, lo


def _mm_sel(sel, x):
    hi, mid, lo = _split3(x)
    return _mm(sel, hi) + _mm(sel, mid) + _mm(sel, lo)


def _mm_expand(x, sel):
    hi, mid, lo = _split3(x)
    return _mm(hi, sel) + _mm(mid, sel) + _mm(lo, sel)


def _rms(x, w):
    return x * lax.rsqrt(jnp.mean(x * x, axis=-1, keepdims=True) + RMS_EPS) * w


def _sigmoid(x):
    return 1.0 / (1.0 + jnp.exp2(x * (-LOG2_E)))


def _silu(x):
    return x * _sigmoid(x)


def _softplus(x):
    return jnp.maximum(x, 0.0) + jnp.log1p(jnp.exp(-jnp.abs(x)))


def _shift_rows(u, prev8, k):
    r = pltpu.roll(u, k, axis=0)
    p = pltpu.roll(prev8, k, axis=0)
    row = lax.broadcasted_iota(jnp.int32, prev8.shape, 0)
    head = jnp.where(row < k, p, r[:V7X_SUBLANES])
    return jnp.concatenate([head, r[V7X_SUBLANES:]], axis=0)


def _softmax_rows(s):
    m = jnp.max(s, axis=-1, keepdims=True)
    p = jnp.exp(s - m)
    return p / jnp.sum(p, axis=-1, keepdims=True)


def _group_norm(y, w):
    outs = []
    for g in range(SSM_GROUPS):
        cols = slice(g * GROUP_WIDTH, (g + 1) * GROUP_WIDTH)
        yg = y[:, cols]
        ms = jnp.mean(yg * yg, axis=-1, keepdims=True)
        outs.append(yg * lax.rsqrt(ms + RMS_EPS) * w[:, cols])
    return jnp.concatenate(outs, axis=1)


def _mem_kernel(m_ref, nw_ref, wk_ref, wv_ref, k_ref, v_ref, kb_ref, vb_ref):
    mn = _rms(m_ref[...], nw_ref[...]).astype(BF16)
    k = _mm(mn, wk_ref[...])
    v = _mm(mn, wv_ref[...])
    kb_ref[...] = k.astype(BF16)
    vb_ref[...] = v.astype(BF16)
    for j in range(ATTN_HEAD_DIM // V7X_LANES):
        for h in range(ATTN_HEADS):
            rows = pl.ds(j * ATTN_HEADS + h, MEM_LEN, stride=CACHE_ROWS)
            lo = h * ATTN_HEAD_DIM + j * V7X_LANES
            k_ref[rows, :] = k[:, lo:lo + V7X_LANES]
            v_ref[rows, :] = v[:, lo:lo + V7X_LANES]


def _mem_call(mem, nw, wk, wv):
    b = mem.shape[0]
    blk = pl.BlockSpec((None, MEM_LEN, D_MODEL), lambda i: (i, 0, 0))
    rows = pl.BlockSpec((None, MEM_LEN * CACHE_ROWS, V7X_LANES), lambda i: (i, 0, 0))
    rows_shape = jax.ShapeDtypeStruct((b, MEM_LEN * CACHE_ROWS, V7X_LANES), F32)
    return pl.pallas_call(
        _mem_kernel,
        grid=(b,),
        in_specs=[blk, _whole(nw.shape), _whole(wk.shape), _whole(wv.shape)],
        out_specs=[rows, rows, blk, blk],
        out_shape=[
            rows_shape,
            rows_shape,
            jax.ShapeDtypeStruct(mem.shape, BF16),
            jax.ShapeDtypeStruct(mem.shape, BF16),
        ],
        compiler_params=_params(1),
        name="mem_kv",
    )(mem, nw, wk, wv)


def _ssd_kernel(x_ref, nw_ref, wz_ref, wxbc_ref, wdt_ref, wgb_ref, cw_ref, cb_ref,
                dtb_ref, alog_ref, dful_ref, gnw_ref, wout_ref, e_ref,
                ybg_ref, cst_ref, sst_ref,
                hist_ref, st_ref, xn_s, xs_s, b_s, c_s, dt_s, da_s, yn_s):
    t = pl.program_id(1)
    tile = x_ref.shape[0]
    q = SSM_CHUNK

    @pl.when(t == 0)
    def _():
        hist_ref[...] = jnp.zeros_like(hist_ref)
        st_ref[...] = jnp.zeros_like(st_ref)

    def project(r0):
        rows = slice(r0, r0 + SSD_SUB)
        xn_s[rows, :] = _rms(x_ref[rows, :], nw_ref[...]).astype(BF16)
        for j in range(SSM_CONV_DIM // CONV_COLS):
            cols = slice(j * CONV_COLS, (j + 1) * CONV_COLS)
            u = _mm(xn_s[rows, :], wxbc_ref[:, cols])
            prev8 = hist_ref[:, cols]
            hist_ref[:, cols] = u[SSD_SUB - V7X_SUBLANES:]
            cw = cw_ref[:, cols]
            conv = _shift_rows(u, prev8, 3) * cw[0:1]
            conv = conv + _shift_rows(u, prev8, 2) * cw[1:2]
            conv = conv + _shift_rows(u, prev8, 1) * cw[2:3]
            conv = conv + u * cw[3:4]
            act = _silu(conv + cb_ref[:, cols])
            lo = j * CONV_COLS
            if lo < SSM_D_INNER:
                xs_s[rows, cols] = act
            elif lo < SSM_D_INNER + SSM_BC:
                b_s[rows, lo - SSM_D_INNER:lo - SSM_D_INNER + CONV_COLS] = act
            else:
                off = lo - SSM_D_INNER - SSM_BC
                c_s[rows, off:off + CONV_COLS] = act
        dt = _softplus(_mm(xn_s[rows, :], wdt_ref[...]) + dtb_ref[...])
        dt_s[rows, :] = dt
        da_s[rows, :] = dt * (-jnp.exp(alog_ref[...]))

    ri = lax.broadcasted_iota(jnp.int32, (q, q), 0)
    ci = lax.broadcasted_iota(jnp.int32, (q, q), 1)
    causal = ri >= ci
    tri = jnp.where(causal, 1.0, 0.0).astype(BF16)
    lane_lo = ci < SSM_HEAD_DIM
    sub8 = lax.broadcasted_iota(jnp.int32, (V7X_SUBLANES, q), 0)

    def chunk(c, carry):
        rows = pl.ds(c * q, q)
        acum = _mm_sel(tri, da_s[rows, :])
        acum_t = acum.T
        last = acum[q - 1:q, :]
        dtc = dt_s[rows, :]
        dt_t = dtc.T
        w =(dtc * jnp.exp(last - acum)).astype(BF16)
        cd = jnp.where(sub8 == 0, jnp.exp(last), 0.0)

        for g in range(SSM_GROUPS):
            gcols = slice(g * GROUP_WIDTH, (g + 1) * GROUP_WIDTH)
            bg = b_s[rows, g * SSM_STATE:(g + 1) * SSM_STATE]
            cg = c_s[rows, g * SSM_STATE:(g + 1) * SSM_STATE].astype(BF16)
            st_g = st_ref[:, gcols]
            yo_g = _mm(cg, st_g.astype(BF16))
            xw = (xs_s[rows, gcols] * _mm(w, e_ref[:, gcols])).astype(BF16)
            dec = _mm_expand(cd, e_ref[:, gcols])[0:1, :]
            st_ref[:, gcols] = st_g * dec + _mm(bg.T.astype(BF16), xw)
            cb_g = _mm_nt(cg, bg.astype(BF16))
            pairs = []
            for pq in range(HEADS_PER_GROUP // 2):
                h0 = g * HEADS_PER_GROUP + 2 * pq
                h1 = h0 + 1
                col0 = acum[:, h0:h0 + 1]
                col1 = acum[:, h1:h1 + 1]
                l0 = jnp.where(causal, jnp.exp(col0 - acum_t[h0:h0 + 1, :]), 0.0)
                l1 = jnp.where(causal, jnp.exp(col1 - acum_t[h1:h1 + 1, :]), 0.0)
                m0 = cb_g * l0 * dt_t[h0:h0 + 1, :]
                m1 = cb_g * l1 * dt_t[h1:h1 + 1, :]
                lhs = jnp.concatenate([m0, m1], axis=1).astype(BF16)
                pcols = slice(h0 * SSM_HEAD_DIM, (h1 + 1) * SSM_HEAD_DIM)
                xp = xs_s[rows, pcols]
                rhs = jnp.concatenate(
                    [jnp.where(lane_lo, xp, 0.0), jnp.where(lane_lo, 0.0, xp)], axis=0
                ).astype(BF16)
                yd = _mm(lhs, rhs)
                sc = jnp.where(lane_lo, jnp.exp(col0), jnp.exp(col1))
                lc = slice(2 * pq * SSM_HEAD_DIM, (2 * pq + 2) * SSM_HEAD_DIM)
                pairs.append(yd + sc * yo_g[:, lc] + dful_ref[:, pcols] * xp)
            z_g = _mm(xn_s[rows, :], wz_ref[:, gcols])
            yg = jnp.concatenate(pairs, axis=1) * _silu(z_g)
            ms = jnp.mean(yg * yg, axis=-1, keepdims=True)
            yn_s[rows, gcols] = (yg * lax.rsqrt(ms + RMS_EPS) * gnw_ref[:, gcols]).astype(BF16)
        gb = _sigmoid(_mm(xn_s[rows, :], wgb_ref[...]))
        ybg_ref[rows, :] = gb * _mm(yn_s[rows, :], wout_ref[...])
        return carry

    for s in range(tile // SSD_SUB):
        project(s * SSD_SUB)
    for c in range(tile // q):
        chunk(c, 0)

    @pl.when(t == pl.num_programs(1) - 1)
    def _():
        cst_ref[...] = pltpu.roll(hist_ref[...], SSM_CONV - 1, axis=0)[0:SSM_CONV - 1]
        for k in range(SSM_D_INNER // V7X_LANES):
            blk = slice(k * V7X_LANES, (k + 1) * V7X_LANES)
            sst_ref[blk, :] = st_ref[:, blk].T


def _ssd_call(x, nw, wz, wxbc, wdt, wgb, cw, cb, dtb, alog, dful, gnw, wout, e):
    b, s, d = x.shape
    tile = SSD_TILE
    consts = (nw, wz, wxbc, wdt, wgb, cw, cb, dtb, alog, dful, gnw, wout, e)
    return pl.pallas_call(
        _ssd_kernel,
        grid=(b, s // tile),
        in_specs=[pl.BlockSpec((None, tile, d), lambda i, j: (i, j, 0))]
        + [_whole(c.shape) for c in consts],
        out_specs=[
            pl.BlockSpec((None, tile, d), lambda i, j: (i, j, 0)),
            pl.BlockSpec((None, SSM_CONV - 1, SSM_CONV_DIM), lambda i, j: (i, 0, 0)),
            pl.BlockSpec((None, SSM_D_INNER, SSM_STATE), lambda i, j: (i, 0, 0)),
        ],
        out_shape=[
            jax.ShapeDtypeStruct((b, s, d), F32),
            jax.ShapeDtypeStruct((b, SSM_CONV - 1, SSM_CONV_DIM), F32),
            jax.ShapeDtypeStruct((b, SSM_D_INNER, SSM_STATE), F32),
        ],
        scratch_shapes=[
            pltpu.VMEM((V7X_SUBLANES, SSM_CONV_DIM), F32),
            pltpu.VMEM((SSM_STATE, SSM_D_INNER), F32),
            pltpu.VMEM((tile, d), BF16),
            pltpu.VMEM((tile, SSM_D_INNER), F32),
            pltpu.VMEM((tile, SSM_BC), F32),
            pltpu.VMEM((tile, SSM_BC), F32),
            pltpu.VMEM((tile, V7X_LANES), F32),
            pltpu.VMEM((tile, V7X_LANES), F32),
            pltpu.VMEM((tile, SSM_D_INNER), BF16),
        ],
        compiler_params=_params(2),
        name="ssd_prompt",
    )(x, *consts)


def _attention(q, k_ref, v_ref):
    outs = []
    for h in range(ATTN_HEADS):
        cols = slice(h * ATTN_HEAD_DIM, (h + 1) * ATTN_HEAD_DIM)
        s = _mm_nt(q[:, cols].astype(BF16), k_ref[:, cols]) * (ATTN_HEAD_DIM ** -0.5)
        outs.append(_mm(_softmax_rows(s).astype(BF16), v_ref[:, cols]))
    return jnp.concatenate(outs, axis=1)


def _mix_kernel(x_ref, ybg_ref, kb_ref, vb_ref, nw_ref, wa_ref, wq_ref, wg_ref, cw_ref,
                wsc_ref, wao_ref, wmo_ref, x1_ref, cst_ref, hist_ref):
    t = pl.program_id(1)
    tile = x_ref.shape[0]

    @pl.when(t == 0)
    def _():
        hist_ref[...] = jnp.zeros_like(hist_ref)

    x = x_ref[...]
    xn = _rms(x, nw_ref[...]).astype(BF16)

    sc_b = _mm(xn, wa_ref[:, 0:SC_DIM])
    u = _mm(xn, wa_ref[:, SC_DIM:2 * SC_DIM]) * _mm(xn, wa_ref[:, 2 * SC_DIM:3 * SC_DIM])
    prev8 = hist_ref[...]
    hist_ref[...] = u[tile - V7X_SUBLANES:]
    cw = cw_ref[...]
    conv = _shift_rows(u, prev8, 2) * cw[0:1]
    conv = conv + _shift_rows(u, prev8, 1) * cw[1:2]
    conv = conv + u * cw[2:3]
    y_a = _mm((sc_b * conv).astype(BF16), wsc_ref[...])

    att = _attention(_mm(xn, wq_ref[...]), kb_ref, vb_ref)
    y_c = _mm(att.astype(BF16), wao_ref[...])

    g_a = _sigmoid(_mm(xn, wg_ref[:, 0:D_MODEL]))
    g_c = _sigmoid(_mm(xn, wg_ref[:, D_MODEL:2 * D_MODEL]))
    merged = g_a * y_a + ybg_ref[...] + g_c * y_c
    x1_ref[...] = x + _mm(merged.astype(BF16), wmo_ref[...])

    @pl.when(t == pl.num_programs(1) - 1)
    def _():
        cst_ref[...] = pltpu.roll(hist_ref[...], SC_WIDTH - 1, axis=0)[0:SC_WIDTH - 1]


def _mix_call(x, ybg, kb, vb, nw, wa, wq, wg, cw, wsc, wao, wmo):
    b, s, d = x.shape
    tile = MIX_TILE
    consts = (nw, wa, wq, wg, cw, wsc, wao, wmo)
    tok = pl.BlockSpec((None, tile, d), lambda i, j: (i, j, 0))
    mem = pl.BlockSpec((None, MEM_LEN, d), lambda i, j: (i, 0, 0))
    return pl.pallas_call(
        _mix_kernel,
        grid=(b, s // tile),
        in_specs=[tok, tok, mem, mem] + [_whole(c.shape) for c in consts],
        out_specs=[tok, pl.BlockSpec((None, SC_WIDTH - 1, SC_DIM), lambda i, j: (i, 0, 0))],
        out_shape=[
            jax.ShapeDtypeStruct((b, s, d), F32),
            jax.ShapeDtypeStruct((b, SC_WIDTH - 1, SC_DIM), F32),
        ],
        scratch_shapes=[pltpu.VMEM((V7X_SUBLANES, SC_DIM), F32)],
        compiler_params=_params(2),
        name="mix_prompt",
    )(x, ybg, kb, vb, *consts)


def _ffn_body(x, nw, wg_ref, wu_ref, wd_ref, fw):
    xn = _rms(x, nw).astype(BF16)
    h = _silu(_mm(xn, wg_ref[...])) * _mm(xn, wu_ref[...])
    x2 = x + _mm(h.astype(BF16), wd_ref[...])
    return _rms(x2, fw)


def _ffn_kernel(x_ref, nw_ref, wg_ref, wu_ref, wd_ref, fw_ref, o_ref):
    o_ref[...] = _ffn_body(x_ref[...], nw_ref[...], wg_ref, wu_ref, wd_ref, fw_ref[...])


def _ffn_call(x, nw, wg, wu, wd, fw):
    b, s, d = x.shape
    tile = FFN_TILE
    consts = (nw, wg, wu, wd, fw)
    tok = pl.BlockSpec((None, tile, d), lambda i, j: (i, j, 0))
    return pl.pallas_call(
        _ffn_kernel,
        grid=(b, s // tile),
        in_specs=[tok] + [_whole(c.shape) for c in consts],
        out_specs=tok,
        out_shape=jax.ShapeDtypeStruct((b, s, d), F32),
        compiler_params=_params(2),
        name="ffn_prompt",
    )(x, *consts)


def _sfront_kernel(x_ref, sconv_ref, ssmc_ref, nw_ref, wa_ref, wz_ref, wxbc_ref, wdt_ref,
                   wq_ref, wg_ref, scw_ref, cw_ref, cb_ref, dtb_ref, alog_ref, wsc_ref, e_ref,
                   gaya_ref, gb_ref, gc_ref, z_ref, xs_ref, dtxt_ref, dect_ref, b_ref, c_ref,
                   q_ref, sconv_o, ssmc_o):
    xn = _rms(x_ref[...], nw_ref[...]).astype(BF16)

    sc_b = _mm(xn, wa_ref[:, 0:SC_DIM])
    u = _mm(xn, wa_ref[:, SC_DIM:2 * SC_DIM]) * _mm(xn, wa_ref[:, 2 * SC_DIM:3 * SC_DIM])
    h0 = sconv_ref[:, 0:SC_DIM]
    h1 = sconv_ref[:, SC_DIM:2 * SC_DIM]
    scw = scw_ref[...]
    conv = h0 * scw[0:1] + h1 * scw[1:2] + u * scw[2:3]
    sconv_o[:, 0:SC_DIM] = h1
    sconv_o[:, SC_DIM:2 * SC_DIM] = u
    y_a = _mm((sc_b * conv).astype(BF16), wsc_ref[...])
    gaya_ref[...] = _sigmoid(_mm(xn, wg_ref[:, 0:D_MODEL])) * y_a
    gb_ref[...] = _sigmoid(_mm(xn, wg_ref[:, D_MODEL:2 * D_MODEL]))
    gc_ref[...] = _sigmoid(_mm(xn, wg_ref[:, 2 * D_MODEL:3 * D_MODEL]))
    z_ref[...] = _mm(xn, wz_ref[...])
    q_ref[...] = _mm(xn, wq_ref[...])

    dt = _softplus(_mm(xn, wdt_ref[...]) + dtb_ref[...])
    dect_ref[...] = jnp.exp(dt * (-jnp.exp(alog_ref[...]))).T
    dtexp = _mm_expand(dt, e_ref[...])

    w = SSM_CONV_DIM
    for j in range(SSM_CONV_DIM // CONV_COLS):
        lo = j * CONV_COLS
        cols = slice(lo, lo + CONV_COLS)
        xbc = _mm(xn, wxbc_ref[:, cols])
        p0 = ssmc_ref[:, lo:lo + CONV_COLS]
        p1 = ssmc_ref[:, w + lo:w + lo + CONV_COLS]
        p2 = ssmc_ref[:, 2 * w + lo:2 * w + lo + CONV_COLS]
        cw = cw_ref[:, cols]
        conv = p0 * cw[0:1] + p1 * cw[1:2] + p2 * cw[2:3] + xbc * cw[3:4]
        ssmc_o[:, lo:lo + CONV_COLS] = p1
        ssmc_o[:, w + lo:w + lo + CONV_COLS] = p2
        ssmc_o[:, 2 * w + lo:2 * w + lo + CONV_COLS] = xbc
        act = _silu(conv + cb_ref[:, cols])
        if lo < SSM_D_INNER:
            xs_ref[:, cols] = act
            dtx = act * dtexp[:, cols]
            for k in range(CONV_COLS // V7X_LANES):
                r0 = lo + k * V7X_LANES
                dtxt_ref[r0:r0 + V7X_LANES, :] = dtx[:, k * V7X_LANES:(k + 1) * V7X_LANES].T
        elif lo < SSM_D_INNER + SSM_BC:
            b_ref[:, lo - SSM_D_INNER:lo - SSM_D_INNER + CONV_COLS] = act
        else:
            off = lo - SSM_D_INNER - SSM_BC
            c_ref[:, off:off + CONV_COLS] = act


def _sfront_call(x, sconv, ssmc, nw, wa, wz, wxbc, wdt, wq, wg, scw, cw, cb, dtb, alog, wsc, e):
    n = x.shape[0]
    args = (x, sconv, ssmc, nw, wa, wz, wxbc, wdt, wq, wg, scw, cw, cb, dtb, alog, wsc, e)
    shapes = [
        (n, D_MODEL), (n, D_MODEL), (n, D_MODEL),
        (n, SSM_D_INNER), (n, SSM_D_INNER),
        (SSM_D_INNER, n), (V7X_LANES, n),
        (n, SSM_BC), (n, SSM_BC),
        (n, D_MODEL),
        sconv.shape, ssmc.shape,
    ]
    return pl.pallas_call(
        _sfront_kernel,
        grid=(1,),
        in_specs=[_whole(a.shape) for a in args],
        out_specs=[_whole(s) for s in shapes],
        out_shape=[jax.ShapeDtypeStruct(s, F32) for s in shapes],
        compiler_params=_params(1),
        name="sample_front",
    )(*args)


def _sssm_kernel(s_ref, dtxt_ref, dect_ref, b_ref, c_ref, so_ref, y_ref):
    i = pl.program_id(0)
    n = dtxt_ref.shape[1]
    lane = lax.broadcasted_iota(jnp.int32, (1, n), 1)
    sub = lax.broadcasted_iota(jnp.int32, (V7X_SUBLANES, SSM_STATE), 0)
    for k in range(SAMPLE_BLOCK):
        onehot = (lane == i * SAMPLE_BLOCK + k).astype(F32)
        xcol = jnp.sum(dtxt_ref[...] * onehot, axis=-1, keepdims=True)
        dcol = jnp.sum(dect_ref[0:SSM_HEADS, :] * onehot, axis=-1, keepdims=True)
        for h in range(SSM_HEADS):
            g = h // HEADS_PER_GROUP
            rows = slice(h * SSM_HEAD_DIM, (h + 1) * SSM_HEAD_DIM)
            brow = b_ref[k:k + 1, g * SSM_STATE:(g + 1) * SSM_STATE]
            so_ref[k, rows, :] = s_ref[k, rows, :] * dcol[h:h + 1, :] + xcol[rows, :] * brow
        c8 = jnp.zeros((V7X_SUBLANES, SSM_STATE), F32)
        for g in range(SSM_GROUPS):
            c8 = jnp.where(sub == g, c_ref[k:k + 1, g * SSM_STATE:(g + 1) * SSM_STATE], c8)
        y8 = _mm_nt(c8.astype(BF16), so_ref[k].astype(BF16))
        y_ref[k:k + 1, :] = jnp.concatenate(
            [y8[g:g + 1, g * GROUP_WIDTH:(g + 1) * GROUP_WIDTH] for g in range(SSM_GROUPS)], axis=1)


def _sssm_call(state, dtxt, dect, bm, cm):
    n = state.shape[0]
    blk = SAMPLE_BLOCK
    st = pl.BlockSpec((blk, SSM_D_INNER, SSM_STATE), lambda i: (i, 0, 0))
    return pl.pallas_call(
        _sssm_kernel,
        grid=(n // blk,),
        in_specs=[st, _whole(dtxt.shape), _whole(dect.shape),
                  pl.BlockSpec((blk, SSM_BC), lambda i: (i, 0)),
                  pl.BlockSpec((blk, SSM_BC), lambda i: (i, 0))],
        out_specs=[st, pl.BlockSpec((blk, SSM_D_INNER), lambda i: (i, 0))],
        out_shape=[jax.ShapeDtypeStruct(state.shape, F32),
                   jax.ShapeDtypeStruct((n, SSM_D_INNER), F32)],
        compiler_params=_params(1),
        name="sample_ssm",
    )(state, dtxt, dect, bm, cm)


def _cache_rows(c):
    n = c.shape[1]
    c = c.reshape(n, MEM_LEN, ATTN_HEADS, ATTN_HEAD_DIM // V7X_LANES, V7X_LANES)
    return c.transpose(0, 1, 3, 2, 4).reshape(n * MEM_LEN * CACHE_ROWS, V7X_LANES)


def _cache_head(ref, k, h):
    base = k * MEM_LEN * CACHE_ROWS
    halves = [ref[pl.ds(base + j * ATTN_HEADS + h, MEM_LEN, stride=CACHE_ROWS), :]
              for j in range(ATTN_HEAD_DIM // V7X_LANES)]
    return jnp.concatenate(halves, axis=1).astype(BF16)


def _sattn_kernel(q_ref, k_ref, v_ref, o_ref):
    sub = lax.broadcasted_iota(jnp.int32, (SAMPLE_BLOCK, MEM_LEN), 0)
    sub_o = lax.broadcasted_iota(jnp.int32, (SAMPLE_BLOCK, ATTN_HEAD_DIM), 0)
    outs = []
    for h in range(ATTN_HEADS):
        qh = q_ref[:, h * ATTN_HEAD_DIM:(h + 1) * ATTN_HEAD_DIM].astype(BF16)
        s = jnp.zeros((SAMPLE_BLOCK, MEM_LEN), F32)
        for k in range(SAMPLE_BLOCK):
            s = jnp.where(sub == k, _mm_nt(qh, _cache_head(k_ref, k, h)), s)
        p = _softmax_rows(s * (ATTN_HEAD_DIM ** -0.5)).astype(BF16)
        o = jnp.zeros((SAMPLE_BLOCK, ATTN_HEAD_DIM), F32)
        for k in range(SAMPLE_BLOCK):
            o = jnp.where(sub_o == k, _mm(p, _cache_head(v_ref, k, h)), o)
        outs.append(o)
    o_ref[...] = jnp.concatenate(outs, axis=1)


def _sattn_call(q, ck, cv):
    n = q.shape[0]
    blk = SAMPLE_BLOCK
    kv = pl.BlockSpec((blk * MEM_LEN * CACHE_ROWS, V7X_LANES), lambda i: (i, 0))
    row = pl.BlockSpec((blk, D_MODEL), lambda i: (i, 0))
    return pl.pallas_call(
        _sattn_kernel,
        grid=(n // blk,),
        in_specs=[row, kv, kv],
        out_specs=row,
        out_shape=jax.ShapeDtypeStruct((n, D_MODEL), F32),
        compiler_params=_params(1),
        name="sample_attn",
    )(q, _cache_rows(ck), _cache_rows(cv))


def _sback_kernel(x_ref, y_ref, xs_ref, z_ref, gaya_ref, gb_ref, gc_ref, att_ref, dful_ref,
                  gnw_ref, wout_ref, wao_ref, wmo_ref, fnw_ref, wg_ref, wu_ref, wd_ref, fw_ref,
                  o_ref):
    y = (y_ref[...] + dful_ref[...] * xs_ref[...]) * _silu(z_ref[...])
    y_b = _mm(_group_norm(y, gnw_ref[...]).astype(BF16), wout_ref[...])
    y_c = _mm(att_ref[...].astype(BF16), wao_ref[...])
    merged = gaya_ref[...] + gb_ref[...] * y_b + gc_ref[...] * y_c
    x1 = x_ref[...] + _mm(merged.astype(BF16), wmo_ref[...])
    o_ref[...] = _ffn_body(x1, fnw_ref[...], wg_ref, wu_ref, wd_ref, fw_ref[...])


def _sback_call(*args):
    n = args[0].shape[0]
    return pl.pallas_call(
        _sback_kernel,
        grid=(1,),
        in_specs=[_whole(a.shape) for a in args],
        out_specs=_whole((n, D_MODEL)),
        out_shape=jax.ShapeDtypeStruct((n, D_MODEL), F32),
        compiler_params=_params(1),
        name="sample_back",
    )(*args)


def kernel(x_prompt, x_sample, mem_prompt, cache_mem_k, cache_mem_v, state_conv, state_ssm_conv, state_ssm, norm_mix_w, w_in, sc_conv_w, w_sc_out, ssm_conv_w, ssm_conv_b, ssm_dt_bias, ssm_a_log, ssm_d, ssm_norm_w, w_ssm_out, norm_mem_w, w_mem_k, w_mem_v, w_attn_o, w_merge_o, norm_ffn_w, w_ffn_gate, w_ffn_up, w_ffn_down, norm_final_w):
    depth = w_in.shape[0]
    assert depth == 1
    bp = x_prompt.shape[0]
    ns = x_sample.shape[0]

    wi = w_in[0]
    o = 0
    wa = wi[:, o:o + 3 * SC_DIM].astype(BF16); o += 3 * SC_DIM
    wz = wi[:, o:o + SSM_D_INNER].astype(BF16); o += SSM_D_INNER
    wxbc = wi[:, o:o + SSM_CONV_DIM].astype(BF16); o += SSM_CONV_DIM
    wdt = jnp.pad(wi[:, o:o + SSM_HEADS], ((0, 0), (0, V7X_LANES - SSM_HEADS))).astype(BF16); o += SSM_HEADS
    wq = wi[:, o:o + D_MODEL].astype(BF16); o += D_MODEL
    wg = wi[:, o:o + 3 * D_MODEL].astype(BF16)
    wg_ac = jnp.concatenate([wg[:, 0:D_MODEL], wg[:, 2 * D_MODEL:]], axis=1)
    wg_b = wg[:, D_MODEL:2 * D_MODEL]
    row = lambda v: v.reshape(1, -1).astype(F32)
    pad_heads = lambda v: jnp.pad(row(v), ((0, 0), (0, V7X_LANES - SSM_HEADS)))
    nmix = row(norm_mix_w[0])
    dtb = pad_heads(ssm_dt_bias[0])
    alog = pad_heads(ssm_a_log[0])
    dful = row(jnp.repeat(ssm_d[0], SSM_HEAD_DIM))
    gnw = row(ssm_norm_w[0])
    cb = row(ssm_conv_b[0])
    cw = ssm_conv_w[0]
    scw = sc_conv_w[0]
    wsc = w_sc_out[0].astype(BF16)
    wout = w_ssm_out[0].astype(BF16)
    wao = w_attn_o[0].astype(BF16)
    wmo = w_merge_o[0].astype(BF16)
    wfg = w_ffn_gate[0].astype(BF16)
    wfu = w_ffn_up[0].astype(BF16)
    wfd = w_ffn_down[0].astype(BF16)
    nffn = row(norm_ffn_w[0])
    nfin = row(norm_final_w)
    expand = (jnp.arange(V7X_LANES)[:, None] == (jnp.arange(SSM_D_INNER)[None, :] // SSM_HEAD_DIM)).astype(BF16)

    mk, mv, mkb, mvb = _mem_call(mem_prompt, row(norm_mem_w[0]), w_mem_k[0].astype(BF16), w_mem_v[0].astype(BF16))
    ybg, p_ssmc, p_ssm = _ssd_call(x_prompt, nmix, wz, wxbc, wdt, wg_b, cw, cb, dtb, alog, dful, gnw, wout, expand)
    x1, p_conv = _mix_call(x_prompt, ybg, mkb, mvb, nmix, wa, wq, wg_ac, scw, wsc, wao, wmo)
    y_prompt = _ffn_call(x1, nffn, wfg, wfu, wfd, nfin)

    xs2 = x_sample.reshape(ns, D_MODEL)
    (gaya, gb, gc, z, xs, dtxt, dect, bm, cm, q, s_conv, s_ssmc) = _sfront_call(
        xs2, state_conv[0].reshape(ns, -1), state_ssm_conv[0].reshape(ns, -1),
        nmix, wa, wz, wxbc, wdt, wq, wg, scw, cw, cb, dtb, alog, wsc, expand)
    s_ssm, y_s = _sssm_call(state_ssm[0].reshape(ns, SSM_D_INNER, SSM_STATE), dtxt, dect, bm, cm)
    att = _sattn_call(q, cache_mem_k, cache_mem_v)
    y_sample = _sback_call(xs2, y_s, xs, z, gaya, gb, gc, att, dful, gnw, wout, wao, wmo,
                           nffn, wfg, wfu, wfd, nfin)

    def from_rows(r):
        r = r.reshape(bp, MEM_LEN, ATTN_HEAD_DIM // V7X_LANES, ATTN_HEADS, V7X_LANES)
        return r.transpose(0, 1, 3, 2, 4).reshape(depth, bp, MEM_LEN, ATTN_HEADS, ATTN_HEAD_DIM)

    state_shape = (SSM_HEADS, SSM_HEAD_DIM, SSM_STATE)
    return (
        y_prompt,
        y_sample.reshape(ns, 1, D_MODEL),
        from_rows(mk),
        from_rows(mv),
        p_conv.reshape(depth, bp, SC_WIDTH - 1, SC_DIM),
        p_ssmc.reshape(depth, bp, SSM_CONV - 1, SSM_CONV_DIM),
        p_ssm.reshape((depth, bp) + state_shape),
        s_conv.reshape(depth, ns, SC_WIDTH - 1, SC_DIM),
        s_ssmc.reshape(depth, ns, SSM_CONV - 1, SSM_CONV_DIM),
        s_ssm.reshape((depth, ns) + state_shape),
    )
```

```python
import jax
import jax.numpy as jnp
from jax import lax
from jax.experimental import pallas as pl
from jax.experimental.pallas import tpu as pltpu

F32 = jnp.float32
BF16 = jnp.bfloat16

D_MODEL = 1024
RMS_EPS = 1e-6
LOG2_E = 1.4426950408889634
SC_DIM = D_MODEL
SC_WIDTH = 3
SSM_D_INNER = 2 * D_MODEL
SSM_HEAD_DIM = 64
SSM_HEADS = SSM_D_INNER // SSM_HEAD_DIM
SSM_STATE = 128
SSM_GROUPS = 4
SSM_CONV = 4
SSM_CHUNK = 128
SSM_BC = SSM_GROUPS * SSM_STATE
SSM_CONV_DIM = SSM_D_INNER + 2 * SSM_BC
HEADS_PER_GROUP = SSM_HEADS // SSM_GROUPS
GROUP_WIDTH = SSM_D_INNER // SSM_GROUPS
MEM_LEN = 256
ATTN_HEADS = 4
ATTN_HEAD_DIM = D_MODEL // ATTN_HEADS
FFN_HIDDEN = ((8 * D_MODEL // 3 + 255) // 256) * 256

V7X_LANES = 128
V7X_SUBLANES = 8
V7X_VMEM_BYTES = 64 * 1024 * 1024
VMEM_LIMIT_BYTES = V7X_VMEM_BYTES - 8 * 1024 * 1024
CACHE_ROWS = ATTN_HEADS * ATTN_HEAD_DIM // V7X_LANES

SSD_TILE = 256
MIX_TILE = 512
FFN_TILE = 512
SAMPLE_BLOCK = 8
CONV_COLS = 512


def _params(n_grid, flags=None):
    return pltpu.CompilerParams(
        dimension_semantics=("arbitrary",) * n_grid,
        vmem_limit_bytes=VMEM_LIMIT_BYTES,
        flags=flags,
    )


def _whole(shape):
    nd = len(shape)
    return pl.BlockSpec(shape, lambda *_: (0,) * nd)


def _mm(a, b):
    return jnp.dot(a, b, preferred_element_type=F32)


def _mm_nt(a, b):
    return lax.dot_general(a, b, (((1,), (1,)), ((), ())), preferred_element_type=F32)


def _split3(x):
    hi = x.astype(BF16)
    r = x - hi.astype(F32)
    mid = r.astype(BF16)
    lo = (r - mid.astype(F32)).astype(BF16)
    return hi, mid, lo


def _mm_sel(sel, x):
    hi, mid, lo = _split3(x)
    return _mm(sel, hi) + _mm(sel, mid) + _mm(sel, lo)


def _mm_expand(x, sel):
    hi, mid, lo = _split3(x)
    return _mm(hi, sel) + _mm(mid, sel) + _mm(lo, sel)


def _rms(x, w):
    return x * lax.rsqrt(jnp.mean(x * x, axis=-1, keepdims=True) + RMS_EPS) * w


def _sigmoid(x):
    return 1.0 / (1.0 + jnp.exp2(x * (-LOG2_E)))


def _silu(x):
    return x * _sigmoid(x)


def _softplus(x):
    return jnp.maximum(x, 0.0) + jnp.log1p(jnp.exp(-jnp.abs(x)))


def _shift_rows(u, prev8, k):
    r = pltpu.roll(u, k, axis=0)
    p = pltpu.roll(prev8, k, axis=0)
    row = lax.broadcasted_iota(jnp.int32, prev8.shape, 0)
    head = jnp.where(row < k, p, r[:V7X_SUBLANES])
    return jnp.concatenate([head, r[V7X_SUBLANES:]], axis=0)


def _softmax_rows(s):
    m = jnp.max(s, axis=-1, keepdims=True)
    p = jnp.exp(s - m)
    return p / jnp.sum(p, axis=-1, keepdims=True)


def _group_norm(y, w):
    outs = []
    for g in range(SSM_GROUPS):
        cols = slice(g * GROUP_WIDTH, (g + 1) * GROUP_WIDTH)
        yg = y[:, cols]
        ms = jnp.mean(yg * yg, axis=-1, keepdims=True)
        outs.append(yg * lax.rsqrt(ms + RMS_EPS) * w[:, cols])
    return jnp.concatenate(outs, axis=1)


def _mem_kernel(m_ref, nw_ref, wk_ref, wv_ref, k_ref, v_ref, kb_ref, vb_ref):
    mn = _rms(m_ref[...], nw_ref[...]).astype(BF16)
    k = _mm(mn, wk_ref[...])
    v = _mm(mn, wv_ref[...])
    kb_ref[...] = k.astype(BF16)
    vb_ref[...] = v.astype(BF16)
    for j in range(ATTN_HEAD_DIM // V7X_LANES):
        for h in range(ATTN_HEADS):
            rows = pl.ds(j * ATTN_HEADS + h, MEM_LEN, stride=CACHE_ROWS)
            lo = h * ATTN_HEAD_DIM + j * V7X_LANES
            k_ref[rows, :] = k[:, lo:lo + V7X_LANES]
            v_ref[rows, :] = v[:, lo:lo + V7X_LANES]


def _mem_call(mem, nw, wk, wv):
    b = mem.shape[0]
    blk = pl.BlockSpec((None, MEM_LEN, D_MODEL), lambda i: (i, 0, 0))
    rows = pl.BlockSpec((None, MEM_LEN * CACHE_ROWS, V7X_LANES), lambda i: (i, 0, 0))
    rows_shape = jax.ShapeDtypeStruct((b, MEM_LEN * CACHE_ROWS, V7X_LANES), F32)
    return pl.pallas_call(
        _mem_kernel,
        grid=(b,),
        in_specs=[blk, _whole(nw.shape), _whole(wk.shape), _whole(wv.shape)],
        out_specs=[rows, rows, blk, blk],
        out_shape=[
            rows_shape,
            rows_shape,
            jax.ShapeDtypeStruct(mem.shape, BF16),
            jax.ShapeDtypeStruct(mem.shape, BF16),
        ],
        compiler_params=_params(1),
        name="mem_kv",
    )(mem, nw, wk, wv)


def _ssd_kernel(x_ref, nw_ref, wz_in, wxbc_in, wdt_in, wgb_in, cw_ref, cb_ref,
                dtb_ref, alog_ref, dful_ref, gnw_ref, wout_in, e_in,
                ybg_ref, cst_ref, sst_ref,
                hist_ref, st_ref, xn_s, z_s, xs_s, b_s, c_s, dt_s, da_s, yn_s,
                wz_ref, wxbc_ref, wdt_ref, wgb_ref, wout_ref, e_ref):
    t = pl.program_id(1)
    tile = x_ref.shape[0]
    q = SSM_CHUNK

    @pl.when((pl.program_id(0) == 0) & (t == 0))
    def _():
        for dst, src in ((wz_ref, wz_in), (wxbc_ref, wxbc_in), (wdt_ref, wdt_in),
                         (wgb_ref, wgb_in), (wout_ref, wout_in), (e_ref, e_in)):
            dst[...] = src[...]

    @pl.when(t == 0)
    def _():
        hist_ref[...] = jnp.zeros_like(hist_ref)
        st_ref[...] = jnp.zeros_like(st_ref)

    xn_s[...] = _rms(x_ref[...], nw_ref[...]).astype(BF16)
    for j in range(SSM_CONV_DIM // CONV_COLS):
        cols = slice(j * CONV_COLS, (j + 1) * CONV_COLS)
        u = _mm(xn_s[...], wxbc_ref[:, cols])
        prev8 = hist_ref[:, cols]
        hist_ref[:, cols] = u[tile - V7X_SUBLANES:]
        cw = cw_ref[:, cols]
        conv = _shift_rows(u, prev8, 3) * cw[0:1]
        conv = conv + _shift_rows(u, prev8, 2) * cw[1:2]
        conv = conv + _shift_rows(u, prev8, 1) * cw[2:3]
        conv = conv + u * cw[3:4]
        act = _silu(conv + cb_ref[:, cols])
        lo = j * CONV_COLS
        if lo < SSM_D_INNER:
            xs_s[:, cols] = act
        elif lo < SSM_D_INNER + SSM_BC:
            b_s[:, lo - SSM_D_INNER:lo - SSM_D_INNER + CONV_COLS] = act
        else:
            off = lo - SSM_D_INNER - SSM_BC
            c_s[:, off:off + CONV_COLS] = act
    dt = _softplus(_mm(xn_s[...], wdt_ref[...]) + dtb_ref[...])
    dt_s[...] = dt
    da_s[...] = dt * (-jnp.exp(alog_ref[...]))
    z_s[...] = _mm(xn_s[...], wz_ref[...])

    ri = lax.broadcasted_iota(jnp.int32, (q, q), 0)
    ci = lax.broadcasted_iota(jnp.int32, (q, q), 1)
    causal = ri >= ci
    tri = jnp.where(causal, 1.0, 0.0).astype(BF16)
    lane_lo = ci < SSM_HEAD_DIM
    sub8 = lax.broadcasted_iota(jnp.int32, (V7X_SUBLANES, q), 0)

    def chunk(c):
        rows = pl.ds(c * q, q)
        acum = _mm_sel(tri, da_s[rows, :])
        acum_t = acum.T
        last = acum[q - 1:q, :]
        dtc = dt_s[rows, :]
        dt_t = dtc.T
        w = (dtc * jnp.exp(last - acum)).astype(BF16)
        cd = jnp.where(sub8 == 0, jnp.exp(last), 0.0)

        for g in range(SSM_GROUPS):
            gcols = slice(g * GROUP_WIDTH, (g + 1) * GROUP_WIDTH)
            bg = b_s[rows, g * SSM_STATE:(g + 1) * SSM_STATE]
            cg = c_s[rows, g * SSM_STATE:(g + 1) * SSM_STATE].astype(BF16)
            st_g = st_ref[:, gcols]
            yo_g = _mm(cg, st_g.astype(BF16))
            xw = (xs_s[rows, gcols] * _mm(w, e_ref[:, gcols])).astype(BF16)
            dec = _mm_expand(cd, e_ref[:, gcols])[0:1, :]
            st_ref[:, gcols] = st_g * dec + _mm(bg.T.astype(BF16), xw)
            cb_g = _mm_nt(cg, bg.astype(BF16))
            pairs = []
            for pq in range(HEADS_PER_GROUP // 2):
                h0 = g * HEADS_PER_GROUP + 2 * pq
                h1 = h0 + 1
                col0 = acum[:, h0:h0 + 1]
                col1 = acum[:, h1:h1 + 1]
                l0 = jnp.where(causal, jnp.exp(col0 - acum_t[h0:h0 + 1, :]), 0.0)
                l1 = jnp.where(causal, jnp.exp(col1 - acum_t[h1:h1 + 1, :]), 0.0)
                m0 = cb_g * l0 * dt_t[h0:h0 + 1, :]
                m1 = cb_g * l1 * dt_t[h1:h1 + 1, :]
                lhs = jnp.concatenate([m0, m1], axis=1).astype(BF16)
                pcols = slice(h0 * SSM_HEAD_DIM, (h1 + 1) * SSM_HEAD_DIM)
                xp = xs_s[rows, pcols]
                rhs = jnp.concatenate(
                    [jnp.where(lane_lo, xp, 0.0), jnp.where(lane_lo, 0.0, xp)], axis=0
                ).astype(BF16)
                yd = _mm(lhs, rhs)
                sc = jnp.where(lane_lo, jnp.exp(col0), jnp.exp(col1))
                lc = slice(2 * pq * SSM_HEAD_DIM, (2 * pq + 2) * SSM_HEAD_DIM)
                pairs.append(yd + sc * yo_g[:, lc] + dful_ref[:, pcols] * xp)
            yg = jnp.concatenate(pairs, axis=1) * _silu(z_s[rows, gcols])
            ms = jnp.mean(yg * yg, axis=-1, keepdims=True)
            yn_s[rows, gcols] = (yg * lax.rsqrt(ms + RMS_EPS) * gnw_ref[:, gcols]).astype(BF16)

    for c in range(tile // q):
        chunk(c)

    gb = _sigmoid(_mm(xn_s[...], wgb_ref[...]))
    ybg_ref[...] = gb * _mm(yn_s[...], wout_ref[...])

    @pl.when(t == pl.num_programs(1) - 1)
    def _():
        cst_ref[...] = pltpu.roll(hist_ref[...], SSM_CONV - 1, axis=0)[0:SSM_CONV - 1]
        for k in range(SSM_D_INNER // V7X_LANES):
            blk = slice(k * V7X_LANES, (k + 1) * V7X_LANES)
            sst_ref[blk, :] = st_ref[:, blk].T


def _ssd_call(x, nw, wz, wxbc, wdt, wgb, cw, cb, dtb, alog, dful, gnw, wout, e):
    b, s, d = x.shape
    tile = SSD_TILE
    consts = (nw, wz, wxbc, wdt, wgb, cw, cb, dtb, alog, dful, gnw, wout, e)
    return pl.pallas_call(
        _ssd_kernel,
        grid=(b, s // tile),
        in_specs=[pl.BlockSpec((None, tile, d), lambda i, j: (i, j, 0))]
        + [_whole(c.shape) for c in consts],
        out_specs=[
            pl.BlockSpec((None, tile, d), lambda i, j: (i, j, 0)),
            pl.BlockSpec((None, SSM_CONV - 1, SSM_CONV_DIM), lambda i, j: (i, 0, 0)),
            pl.BlockSpec((None, SSM_D_INNER, SSM_STATE), lambda i, j: (i, 0, 0)),
        ],
        out_shape=[
            jax.ShapeDtypeStruct((b, s, d), F32),
            jax.ShapeDtypeStruct((b, SSM_CONV - 1, SSM_CONV_DIM), F32),
            jax.ShapeDtypeStruct((b, SSM_D_INNER, SSM_STATE), F32),
        ],
        scratch_shapes=[
            pltpu.VMEM((V7X_SUBLANES, SSM_CONV_DIM), F32),
            pltpu.VMEM((SSM_STATE, SSM_D_INNER), F32),
            pltpu.VMEM((tile, d), BF16),
            pltpu.VMEM((tile, SSM_D_INNER), F32),
            pltpu.VMEM((tile, SSM_D_INNER), F32),
            pltpu.VMEM((tile, SSM_BC), F32),
            pltpu.VMEM((tile, SSM_BC), F32),
            pltpu.VMEM((tile, V7X_LANES), F32),
            pltpu.VMEM((tile, V7X_LANES), F32),
            pltpu.VMEM((tile, SSM_D_INNER), BF16),
        ] + [pltpu.VMEM(w.shape, BF16) for w in (wz, wxbc, wdt, wgb, wout, e)],
        compiler_params=_params(2),
        name="ssd_prompt",
    )(x, *consts)


def _attention(q, k_ref, v_ref):
    outs = []
    for h in range(ATTN_HEADS):
        cols = slice(h * ATTN_HEAD_DIM, (h + 1) * ATTN_HEAD_DIM)
        s = _mm_nt(q[:, cols].astype(BF16), k_ref[:, cols]) * (ATTN_HEAD_DIM ** -0.5)
        outs.append(_mm(_softmax_rows(s).astype(BF16), v_ref[:, cols]))
    return jnp.concatenate(outs, axis=1)


def _mix_kernel(x_ref, ybg_ref, kb_ref, vb_ref, nw_ref, wa_ref, wq_ref, wg_ref, cw_ref,
                wsc_ref, wao_ref, wmo_ref, x1_ref, cst_ref, hist_ref):
    t = pl.program_id(1)
    tile = x_ref.shape[0]

    @pl.when(t == 0)
    def _():
        hist_ref[...] = jnp.zeros_like(hist_ref)

    x = x_ref[...]
    xn = _rms(x, nw_ref[...]).astype(BF16)

    sc_b = _mm(xn, wa_ref[:, 0:SC_DIM])
    u = _mm(xn, wa_ref[:, SC_DIM:2 * SC_DIM]) * _mm(xn, wa_ref[:, 2 * SC_DIM:3 * SC_DIM])
    prev8 = hist_ref[...]
    hist_ref[...] = u[tile - V7X_SUBLANES:]
    cw = cw_ref[...]
    conv = _shift_rows(u, prev8, 2) * cw[0:1]
    conv = conv + _shift_rows(u, prev8, 1) * cw[1:2]
    conv = conv + u * cw[2:3]
    y_a = _mm((sc_b * conv).astype(BF16), wsc_ref[...])

    att = _attention(_mm(xn, wq_ref[...]), kb_ref, vb_ref)
    y_c = _mm(att.astype(BF16), wao_ref[...])

    g_a = _sigmoid(_mm(xn, wg_ref[:, 0:D_MODEL]))
    g_c = _sigmoid(_mm(xn, wg_ref[:, D_MODEL:2 * D_MODEL]))
    merged = g_a * y_a + ybg_ref[...] + g_c * y_c
    x1_ref[...] = x + _mm(merged.astype(BF16), wmo_ref[...])

    @pl.when(t == pl.num_programs(1) - 1)
    def _():
        cst_ref[...] = pltpu.roll(hist_ref[...], SC_WIDTH - 1, axis=0)[0:SC_WIDTH - 1]


def _mix_call(x, ybg, kb, vb, nw, wa, wq, wg, cw, wsc, wao, wmo):
    b, s, d = x.shape
    tile = MIX_TILE
    consts = (nw, wa, wq, wg, cw, wsc, wao, wmo)
    tok = pl.BlockSpec((None, tile, d), lambda i, j: (i, j, 0))
    mem = pl.BlockSpec((None, MEM_LEN, d), lambda i, j: (i, 0, 0))
    return pl.pallas_call(
        _mix_kernel,
        grid=(b, s // tile),
        in_specs=[tok, tok, mem, mem] + [_whole(c.shape) for c in consts],
        out_specs=[tok, pl.BlockSpec((None, SC_WIDTH - 1, SC_DIM), lambda i, j: (i, 0, 0))],
        out_shape=[
            jax.ShapeDtypeStruct((b, s, d), F32),
            jax.ShapeDtypeStruct((b, SC_WIDTH - 1, SC_DIM), F32),
        ],
        scratch_shapes=[pltpu.VMEM((V7X_SUBLANES, SC_DIM), F32)],
        compiler_params=_params(2),
        name="mix_prompt",
    )(x, ybg, kb, vb, *consts)


def _ffn_body(x, nw, wg_ref, wu_ref, wd_ref, fw):
    xn = _rms(x, nw).astype(BF16)
    h = _silu(_mm(xn, wg_ref[...])) * _mm(xn, wu_ref[...])
    x2 = x + _mm(h.astype(BF16), wd_ref[...])
    return _rms(x2, fw)


def _ffn_kernel(x_ref, nw_ref, wg_ref, wu_ref, wd_ref, fw_ref, o_ref):
    o_ref[...] = _ffn_body(x_ref[...], nw_ref[...], wg_ref, wu_ref, wd_ref, fw_ref[...])


def _ffn_call(x, nw, wg, wu, wd, fw):
    b, s, d = x.shape
    tile = FFN_TILE
    consts = (nw, wg, wu, wd, fw)
    tok = pl.BlockSpec((None, tile, d), lambda i, j: (i, j, 0))
    return pl.pallas_call(
        _ffn_kernel,
        grid=(b, s // tile),
        in_specs=[tok] + [_whole(c.shape) for c in consts],
        out_specs=tok,
        out_shape=jax.ShapeDtypeStruct((b, s, d), F32),
        compiler_params=_params(2),
        name="ffn_prompt",
    )(x, *consts)


def _sfront_kernel(x_ref, sconv_ref, ssmc_ref, nw_ref, wa_ref, wz_ref, wxbc_ref, wdt_ref,
                   wq_ref, wg_ref, scw_ref, cw_ref, cb_ref, dtb_ref, alog_ref, wsc_ref, e_ref,
                   gaya_ref, gb_ref, gc_ref, z_ref, xs_ref, dtxt_ref, dect_ref, b_ref, c_ref,
                   q_ref, sconv_o, ssmc_o):
    xn = _rms(x_ref[...], nw_ref[...]).astype(BF16)

    sc_b = _mm(xn, wa_ref[:, 0:SC_DIM])
    u = _mm(xn, wa_ref[:, SC_DIM:2 * SC_DIM]) * _mm(xn, wa_ref[:, 2 * SC_DIM:3 * SC_DIM])
    h0 = sconv_ref[:, 0:SC_DIM]
    h1 = sconv_ref[:, SC_DIM:2 * SC_DIM]
    scw = scw_ref[...]
    conv = h0 * scw[0:1] + h1 * scw[1:2] + u * scw[2:3]
    sconv_o[:, 0:SC_DIM] = h1
    sconv_o[:, SC_DIM:2 * SC_DIM] = u
    y_a = _mm((sc_b * conv).astype(BF16), wsc_ref[...])
    gaya_ref[...] = _sigmoid(_mm(xn, wg_ref[:, 0:D_MODEL])) * y_a
    gb_ref[...] = _sigmoid(_mm(xn, wg_ref[:, D_MODEL:2 * D_MODEL]))
    gc_ref[...] = _sigmoid(_mm(xn, wg_ref[:, 2 * D_MODEL:3 * D_MODEL]))
    z_ref[...] = _mm(xn, wz_ref[...])
    q_ref[...] = _mm(xn, wq_ref[...])

    dt = _softplus(_mm(xn, wdt_ref[...]) + dtb_ref[...])
    dect_ref[...] = jnp.exp(dt * (-jnp.exp(alog_ref[...]))).T
    dtexp = _mm_expand(dt, e_ref[...])

    for j in range(SSM_CONV_DIM // CONV_COLS):
        lo = j * CONV_COLS
        cols = slice(lo, lo + CONV_COLS)
        xbc = _mm(xn, wxbc_ref[:, cols])
        p0 = ssmc_ref[0, :, cols]
        p1 = ssmc_ref[1, :, cols]
        p2 = ssmc_ref[2, :, cols]
        cw = cw_ref[:, cols]
        conv = p0 * cw[0:1] + p1 * cw[1:2] + p2 * cw[2:3] + xbc * cw[3:4]
        ssmc_o[0, :, cols] = p1
        ssmc_o[1, :, cols] = p2
        ssmc_o[2, :, cols] = xbc
        act = _silu(conv + cb_ref[:, cols])
        if lo < SSM_D_INNER:
            xs_ref[:, cols] = act
            dtx = act * dtexp[:, cols]
            for k in range(CONV_COLS // V7X_LANES):
                r0 = lo + k * V7X_LANES
                dtxt_ref[r0:r0 + V7X_LANES, :] = dtx[:, k * V7X_LANES:(k + 1) * V7X_LANES].T
        elif lo < SSM_D_INNER + SSM_BC:
            b_ref[:, lo - SSM_D_INNER:lo - SSM_D_INNER + CONV_COLS] = act
        else:
            off = lo - SSM_D_INNER - SSM_BC
            c_ref[:, off:off + CONV_COLS] = act


def _sfront_call(x, sconv, ssmc, nw, wa, wz, wxbc, wdt, wq, wg, scw, cw, cb, dtb, alog, wsc, e):
    n = x.shape[0]
    args = (x, sconv, ssmc, nw, wa, wz, wxbc, wdt, wq, wg, scw, cw, cb, dtb, alog, wsc, e)
    shapes = [
        (n, D_MODEL), (n, D_MODEL), (n, D_MODEL),
        (n, SSM_D_INNER), (n, SSM_D_INNER),
        (SSM_D_INNER, n), (V7X_LANES, n),
        (n, SSM_BC), (n, SSM_BC),
        (n, D_MODEL),
        sconv.shape, ssmc.shape,
    ]
    return pl.pallas_call(
        _sfront_kernel,
        grid=(1,),
        in_specs=[_whole(a.shape) for a in args],
        out_specs=[_whole(s) for s in shapes],
        out_shape=[jax.ShapeDtypeStruct(s, F32) for s in shapes],
        compiler_params=_params(1),
        name="sample_front",
    )(*args)


def _sssm_kernel(s_ref, dtxt_ref, dect_ref, b_ref, c_ref, so_ref, y_ref):
    i = pl.program_id(0)
    n = dtxt_ref.shape[1]
    lane = lax.broadcasted_iota(jnp.int32, (1, n), 1)
    sub = lax.broadcasted_iota(jnp.int32, (V7X_SUBLANES, SSM_STATE), 0)
    for k in range(SAMPLE_BLOCK):
        onehot = (lane == i * SAMPLE_BLOCK + k).astype(F32)
        xcol = jnp.sum(dtxt_ref[...] * onehot, axis=-1, keepdims=True)
        dcol = jnp.sum(dect_ref[0:SSM_HEADS, :] * onehot, axis=-1, keepdims=True)
        for h in range(SSM_HEADS):
            g = h // HEADS_PER_GROUP
            rows = slice(h * SSM_HEAD_DIM, (h + 1) * SSM_HEAD_DIM)
            brow = b_ref[k:k + 1, g * SSM_STATE:(g + 1) * SSM_STATE]
            so_ref[k, rows, :] = s_ref[k, rows, :] * dcol[h:h + 1, :] + xcol[rows, :] * brow
        c8 = jnp.zeros((V7X_SUBLANES, SSM_STATE), F32)
        for g in range(SSM_GROUPS):
            c8 = jnp.where(sub == g, c_ref[k:k + 1, g * SSM_STATE:(g + 1) * SSM_STATE], c8)
        y8 = _mm_nt(c8.astype(BF16), so_ref[k].astype(BF16))
        y_ref[k:k + 1, :] = jnp.concatenate(
            [y8[g:g + 1, g * GROUP_WIDTH:(g + 1) * GROUP_WIDTH] for g in range(SSM_GROUPS)], axis=1)


def _sssm_call(state, dtxt, dect, bm, cm):
    n = state.shape[0]
    blk = SAMPLE_BLOCK
    st = pl.BlockSpec((blk, SSM_D_INNER, SSM_STATE), lambda i: (i, 0, 0))
    return pl.pallas_call(
        _sssm_kernel,
        grid=(n // blk,),
        in_specs=[st, _whole(dtxt.shape), _whole(dect.shape),
                  pl.BlockSpec((blk, SSM_BC), lambda i: (i, 0)),
                  pl.BlockSpec((blk, SSM_BC), lambda i: (i, 0))],
        out_specs=[st, pl.BlockSpec((blk, SSM_D_INNER), lambda i: (i, 0))],
        out_shape=[jax.ShapeDtypeStruct(state.shape, F32),
                   jax.ShapeDtypeStruct((n, SSM_D_INNER), F32)],
        compiler_params=_params(1),
        name="sample_ssm",
    )(state, dtxt, dect, bm, cm)


def _cache_rows(c):
    n = c.shape[1]
    c = c.reshape(n, MEM_LEN, ATTN_HEADS, ATTN_HEAD_DIM // V7X_LANES, V7X_LANES)
    return c.transpose(0, 1, 3, 2, 4).reshape(n * MEM_LEN * CACHE_ROWS, V7X_LANES)


def _cache_head(ref, k, h):
    base = k * MEM_LEN * CACHE_ROWS
    halves = [ref[pl.ds(base + j * ATTN_HEADS + h, MEM_LEN, stride=CACHE_ROWS), :]
              for j in range(ATTN_HEAD_DIM // V7X_LANES)]
    return jnp.concatenate(halves, axis=1).astype(BF16)


def _sattn_kernel(q_ref, k_ref, v_ref, o_ref):
    sub = lax.broadcasted_iota(jnp.int32, (SAMPLE_BLOCK, MEM_LEN), 0)
    sub_o = lax.broadcasted_iota(jnp.int32, (SAMPLE_BLOCK, ATTN_HEAD_DIM), 0)
    outs = []
    for h in range(ATTN_HEADS):
        qh = q_ref[:, h * ATTN_HEAD_DIM:(h + 1) * ATTN_HEAD_DIM].astype(BF16)
        s = jnp.zeros((SAMPLE_BLOCK, MEM_LEN), F32)
        for k in range(SAMPLE_BLOCK):
            s = jnp.where(sub == k, _mm_nt(qh, _cache_head(k_ref, k, h)), s)
        p = _softmax_rows(s * (ATTN_HEAD_DIM ** -0.5)).astype(BF16)
        o = jnp.zeros((SAMPLE_BLOCK, ATTN_HEAD_DIM), F32)
        for k in range(SAMPLE_BLOCK):
            o = jnp.where(sub_o == k, _mm(p, _cache_head(v_ref, k, h)), o)
        outs.append(o)
    o_ref[...] = jnp.concatenate(outs, axis=1)


def _sattn_call(q, ck, cv):
    n = q.shape[0]
    blk = SAMPLE_BLOCK
    kv = pl.BlockSpec((blk * MEM_LEN * CACHE_ROWS, V7X_LANES), lambda i: (i, 0))
    row = pl.BlockSpec((blk, D_MODEL), lambda i: (i, 0))
    return pl.pallas_call(
        _sattn_kernel,
        grid=(n // blk,),
        in_specs=[row, kv, kv],
        out_specs=row,
        out_shape=jax.ShapeDtypeStruct((n, D_MODEL), F32),
        compiler_params=_params(1),
        name="sample_attn",
    )(q, _cache_rows(ck), _cache_rows(cv))


def _sback_kernel(x_ref, y_ref, xs_ref, z_ref, gaya_ref, gb_ref, gc_ref, att_ref, dful_ref,
                  gnw_ref, wout_ref, wao_ref, wmo_ref, fnw_ref, wg_ref, wu_ref, wd_ref, fw_ref,
                  o_ref):
    y = (y_ref[...] + dful_ref[...] * xs_ref[...]) * _silu(z_ref[...])
    y_b = _mm(_group_norm(y, gnw_ref[...]).astype(BF16), wout_ref[...])
    y_c = _mm(att_ref[...].astype(BF16), wao_ref[...])
    merged = gaya_ref[...] + gb_ref[...] * y_b + gc_ref[...] * y_c
    x1 = x_ref[...] + _mm(merged.astype(BF16), wmo_ref[...])
    o_ref[...] = _ffn_body(x1, fnw_ref[...], wg_ref, wu_ref, wd_ref, fw_ref[...])


def _sback_call(*args):
    n = args[0].shape[0]
    return pl.pallas_call(
        _sback_kernel,
        grid=(1,),
        in_specs=[_whole(a.shape) for a in args],
        out_specs=_whole((n, D_MODEL)),
        out_shape=jax.ShapeDtypeStruct((n, D_MODEL), F32),
        compiler_params=_params(1),
        name="sample_back",
    )(*args)


def kernel(x_prompt, x_sample, mem_prompt, cache_mem_k, cache_mem_v, state_conv, state_ssm_conv, state_ssm, norm_mix_w, w_in, sc_conv_w, w_sc_out, ssm_conv_w, ssm_conv_b, ssm_dt_bias, ssm_a_log, ssm_d, ssm_norm_w, w_ssm_out, norm_mem_w, w_mem_k, w_mem_v, w_attn_o, w_merge_o, norm_ffn_w, w_ffn_gate, w_ffn_up, w_ffn_down, norm_final_w):
    depth = w_in.shape[0]
    assert depth == 1
    bp = x_prompt.shape[0]
    ns = x_sample.shape[0]

    wi = w_in[0]
    o = 0
    wa = wi[:, o:o + 3 * SC_DIM].astype(BF16); o += 3 * SC_DIM
    wz = wi[:, o:o + SSM_D_INNER].astype(BF16); o += SSM_D_INNER
    wxbc = wi[:, o:o + SSM_CONV_DIM].astype(BF16); o += SSM_CONV_DIM
    wdt = jnp.pad(wi[:, o:o + SSM_HEADS], ((0, 0), (0, V7X_LANES - SSM_HEADS))).astype(BF16); o += SSM_HEADS
    wq = wi[:, o:o + D_MODEL].astype(BF16); o += D_MODEL
    wg = wi[:, o:o + 3 * D_MODEL].astype(BF16)
    wg_ac = jnp.concatenate([wg[:, 0:D_MODEL], wg[:, 2 * D_MODEL:]], axis=1)
    wg_b = wg[:, D_MODEL:2 * D_MODEL]
    row = lambda v: v.reshape(1, -1).astype(F32)
    pad_heads = lambda v: jnp.pad(row(v), ((0, 0), (0, V7X_LANES - SSM_HEADS)))
    nmix = row(norm_mix_w[0])
    dtb = pad_heads(ssm_dt_bias[0])
    alog = pad_heads(ssm_a_log[0])
    dful = row(jnp.repeat(ssm_d[0], SSM_HEAD_DIM))
    gnw = row(ssm_norm_w[0])
    cb = row(ssm_conv_b[0])
    cw = ssm_conv_w[0]
    scw = sc_conv_w[0]
    wsc = w_sc_out[0].astype(BF16)
    wout = w_ssm_out[0].astype(BF16)
    wao = w_attn_o[0].astype(BF16)
    wmo = w_merge_o[0].astype(BF16)
    wfg = w_ffn_gate[0].astype(BF16)
    wfu = w_ffn_up[0].astype(BF16)
    wfd = w_ffn_down[0].astype(BF16)
    nffn = row(norm_ffn_w[0])
    nfin = row(norm_final_w)
    expand = (jnp.arange(V7X_LANES)[:, None] == (jnp.arange(SSM_D_INNER)[None, :] // SSM_HEAD_DIM)).astype(BF16)

    mk, mv, mkb, mvb = _mem_call(mem_prompt, row(norm_mem_w[0]), w_mem_k[0].astype(BF16), w_mem_v[0].astype(BF16))
    ybg, p_ssmc, p_ssm = _ssd_call(x_prompt, nmix, wz, wxbc, wdt, wg_b, cw, cb, dtb, alog, dful, gnw, wout, expand)
    x1, p_conv = _mix_call(x_prompt, ybg, mkb, mvb, nmix, wa, wq, wg_ac, scw, wsc, wao, wmo)
    y_prompt = _ffn_call(x1, nffn, wfg, wfu, wfd, nfin)

    xs2 = x_sample.reshape(ns, D_MODEL)
    (gaya, gb, gc, z, xs, dtxt, dect, bm, cm, q, s_conv, s_ssmc) = _sfront_call(
        xs2, state_conv[0].reshape(ns, -1), jnp.swapaxes(state_ssm_conv[0], 0, 1),
        nmix, wa, wz, wxbc, wdt, wq, wg, scw, cw, cb, dtb, alog, wsc, expand)
    s_ssm, y_s = _sssm_call(state_ssm[0].reshape(ns, SSM_D_INNER, SSM_STATE), dtxt, dect, bm, cm)
    att = _sattn_call(q, cache_mem_k, cache_mem_v)
    y_sample = _sback_call(xs2, y_s, xs, z, gaya, gb, gc, att, dful, gnw, wout, wao, wmo,
                           nffn, wfg, wfu, wfd, nfin)

    def from_rows(r):
        r = r.reshape(bp, MEM_LEN, ATTN_HEAD_DIM // V7X_LANES, ATTN_HEADS, V7X_LANES)
        return r.transpose(0, 1, 3, 2, 4).reshape(depth, bp, MEM_LEN, ATTN_HEADS, ATTN_HEAD_DIM)

    state_shape = (SSM_HEADS, SSM_HEAD_DIM, SSM_STATE)
    return (
        y_prompt,
        y_sample.reshape(ns, 1, D_MODEL),
        from_rows(mk),
        from_rows(mv),
        p_conv.reshape(depth, bp, SC_WIDTH - 1, SC_DIM),
        p_ssmc.reshape(depth, bp, SSM_CONV - 1, SSM_CONV_DIM),
        p_ssm.reshape((depth, bp) + state_shape),
        s_conv.reshape(depth, ns, SC_WIDTH - 1, SC_DIM),
        jnp.swapaxes(s_ssmc, 0, 1).reshape(depth, ns, SSM_CONV - 1, SSM_CONV_DIM),
        s_ssm.reshape((depth, ns) + state_shape),
    )
```

```python
import jax
import jax.numpy as jnp
from jax import lax
from jax.experimental import pallas as pl
from jax.experimental.pallas import tpu as pltpu

F32 = jnp.float32
BF16 = jnp.bfloat16

D_MODEL = 1024
RMS_EPS = 1e-6
LOG2_E = 1.4426950408889634
SC_DIM = D_MODEL
SC_WIDTH = 3
SSM_D_INNER = 2 * D_MODEL
SSM_HEAD_DIM = 64
SSM_HEADS = SSM_D_INNER // SSM_HEAD_DIM
SSM_STATE = 128
SSM_GROUPS = 4
SSM_CONV = 4
SSM_CHUNK = 128
SSM_BC = SSM_GROUPS * SSM_STATE
SSM_CONV_DIM = SSM_D_INNER + 2 * SSM_BC
HEADS_PER_GROUP = SSM_HEADS // SSM_GROUPS
GROUP_WIDTH = SSM_D_INNER // SSM_GROUPS
MEM_LEN = 256
ATTN_HEADS = 4
ATTN_HEAD_DIM = D_MODEL // ATTN_HEADS
FFN_HIDDEN = ((8 * D_MODEL // 3 + 255) // 256) * 256

V7X_LANES = 128
V7X_SUBLANES = 8
V7X_VMEM_BYTES = 64 * 1024 * 1024
VMEM_LIMIT_BYTES = V7X_VMEM_BYTES - 8 * 1024 * 1024
CACHE_ROWS = ATTN_HEADS * ATTN_HEAD_DIM // V7X_LANES

SSD_TILE = 256
MIX_TILE = 512
FFN_TILE = 512
SAMPLE_BLOCK = 8
CONV_COLS = 512


def _params(n_grid, flags=None):
    return pltpu.CompilerParams(
        dimension_semantics=("arbitrary",) * n_grid,
        vmem_limit_bytes=VMEM_LIMIT_BYTES,
        flags=flags,
    )


def _whole(shape):
    nd = len(shape)
    return pl.BlockSpec(shape, lambda *_: (0,) * nd)


def _mm(a, b):
    return jnp.dot(a, b, preferred_element_type=F32)


def _mm_nt(a, b):
    return lax.dot_general(a, b, (((1,), (1,)), ((), ())), preferred_element_type=F32)


def _split3(x):
    hi = x.astype(BF16)
    r = x - hi.astype(F32)
    mid = r.astype(BF16)
    lo = (r - mid.astype(F32)).astype(BF16)
    return hi, mid, lo


def _mm_sel(sel, x):
    hi, mid, lo = _split3(x)
    return _mm(sel, hi) + _mm(sel, mid) + _mm(sel, lo)


def _mm_expand(x, sel):
    hi, mid, lo = _split3(x)
    return _mm(hi, sel) + _mm(mid, sel) + _mm(lo, sel)


def _rms(x, w):
    return x * lax.rsqrt(jnp.mean(x * x, axis=-1, keepdims=True) + RMS_EPS) * w


def _sigmoid(x):
    return 1.0 / (1.0 + jnp.exp2(x * (-LOG2_E)))


def _silu(x):
    return x * _sigmoid(x)


def _softplus(x):
    return jnp.maximum(x, 0.0) + jnp.log1p(jnp.exp(-jnp.abs(x)))


def _shift_rows(u, prev8, k):
    r = pltpu.roll(u, k, axis=0)
    p = pltpu.roll(prev8, k, axis=0)
    row = lax.broadcasted_iota(jnp.int32, prev8.shape, 0)
    head = jnp.where(row < k, p, r[:V7X_SUBLANES])
    return jnp.concatenate([head, r[V7X_SUBLANES:]], axis=0)


def _softmax_rows(s):
    m = jnp.max(s, axis=-1, keepdims=True)
    p = jnp.exp(s - m)
    return p / jnp.sum(p, axis=-1, keepdims=True)


def _group_norm(y, w):
    outs = []
    for g in range(SSM_GROUPS):
        cols = slice(g * GROUP_WIDTH, (g + 1) * GROUP_WIDTH)
        yg = y[:, cols]
        ms = jnp.mean(yg * yg, axis=-1, keepdims=True)
        outs.append(yg * lax.rsqrt(ms + RMS_EPS) * w[:, cols])
    return jnp.concatenate(outs, axis=1)


def _mem_kernel(m_ref, nw_ref, wk_ref, wv_ref, k_ref, v_ref, kb_ref, vb_ref):
    mn = _rms(m_ref[...], nw_ref[...]).astype(BF16)
    k = _mm(mn, wk_ref[...])
    v = _mm(mn, wv_ref[...])
    kb_ref[...] = k.astype(BF16)
    vb_ref[...] = v.astype(BF16)
    for j in range(ATTN_HEAD_DIM // V7X_LANES):
        for h in range(ATTN_HEADS):
            rows = pl.ds(j * ATTN_HEADS + h, MEM_LEN, stride=CACHE_ROWS)
            lo = h * ATTN_HEAD_DIM + j * V7X_LANES
            k_ref[rows, :] = k[:, lo:lo + V7X_LANES]
            v_ref[rows, :] = v[:, lo:lo + V7X_LANES]


def _mem_call(mem, nw, wk, wv):
    b = mem.shape[0]
    blk = pl.BlockSpec((None, MEM_LEN, D_MODEL), lambda i: (i, 0, 0))
    rows = pl.BlockSpec((None, MEM_LEN * CACHE_ROWS, V7X_LANES), lambda i: (i, 0, 0))
    rows_shape = jax.ShapeDtypeStruct((b, MEM_LEN * CACHE_ROWS, V7X_LANES), F32)
    return pl.pallas_call(
        _mem_kernel,
        grid=(b,),
        in_specs=[blk, _whole(nw.shape), _whole(wk.shape), _whole(wv.shape)],
        out_specs=[rows, rows, blk, blk],
        out_shape=[
            rows_shape,
            rows_shape,
            jax.ShapeDtypeStruct(mem.shape, BF16),
            jax.ShapeDtypeStruct(mem.shape, BF16),
        ],
        compiler_params=_params(1),
        name="mem_kv",
    )(mem, nw, wk, wv)


def _ssd_kernel(x_ref, nw_ref, wz_in, wxbc_in, wdt_in, wgb_in, cw_ref, cb_ref,
                dtb_ref, alog_ref, dful_ref, gnw_ref, wout_in, e_in,
                ybg_ref, cst_ref, sst_ref,
                hist_ref, st_ref, xn_s, z_s, xs_s, b_s, c_s, dt_s, da_s, yn_s,
                wz_ref, wxbc_ref, wdt_ref, wgb_ref, wout_ref, e_ref):
    t = pl.program_id(1)
    tile = x_ref.shape[0]
    q = SSM_CHUNK

    @pl.when((pl.program_id(0) == 0) & (t == 0))
    def _():
        for dst, src in ((wz_ref, wz_in), (wxbc_ref, wxbc_in), (wdt_ref, wdt_in),
                         (wgb_ref, wgb_in), (wout_ref, wout_in), (e_ref, e_in)):
            dst[...] = src[...]

    @pl.when(t == 0)
    def _():
        hist_ref[...] = jnp.zeros_like(hist_ref)
        st_ref[...] = jnp.zeros_like(st_ref)

    xn_s[...] = _rms(x_ref[...], nw_ref[...]).astype(BF16)
    for j in range(SSM_CONV_DIM // CONV_COLS):
        cols = slice(j * CONV_COLS, (j + 1) * CONV_COLS)
        u = _mm(xn_s[...], wxbc_ref[:, cols])
        prev8 = hist_ref[:, cols]
        hist_ref[:, cols] = u[tile - V7X_SUBLANES:]
        cw = cw_ref[:, cols]
        conv = _shift_rows(u, prev8, 3) * cw[0:1]
        conv = conv + _shift_rows(u, prev8, 2) * cw[1:2]
        conv = conv + _shift_rows(u, prev8, 1) * cw[2:3]
        conv = conv + u * cw[3:4]
        act = _silu(conv + cb_ref[:, cols])
        lo = j * CONV_COLS
        if lo < SSM_D_INNER:
            xs_s[:, cols] = act
        elif lo < SSM_D_INNER + SSM_BC:
            b_s[:, lo - SSM_D_INNER:lo - SSM_D_INNER + CONV_COLS] = act
        else:
            off = lo - SSM_D_INNER - SSM_BC
            c_s[:, off:off + CONV_COLS] = act
    dt = _softplus(_mm(xn_s[...], wdt_ref[...]) + dtb_ref[...])
    dt_s[...] = dt
    da_s[...] = dt * (-jnp.exp(alog_ref[...]))
    z_s[...] = _silu(_mm(xn_s[...], wz_ref[...]))

    ri = lax.broadcasted_iota(jnp.int32, (q, q), 0)
    ci = lax.broadcasted_iota(jnp.int32, (q, q), 1)
    causal = ri >= ci
    tri = jnp.where(causal, 1.0, 0.0).astype(BF16)
    lane_lo = ci < SSM_HEAD_DIM
    keep_lo = jnp.where(lane_lo, 1.0, 0.0).astype(BF16)
    keep_hi = jnp.where(lane_lo, 0.0, 1.0).astype(BF16)
    sub8 =lax.broadcasted_iota(jnp.int32, (V7X_SUBLANES, q), 0)

    def chunk(c):
        rows = pl.ds(c * q, q)
        acum = _mm_sel(tri, da_s[rows, :])
        acum_t = acum.T
        last = acum[q - 1:q, :]
        dtc = dt_s[rows, :]
        row_t = acum_t - jnp.log(dtc.T)
        w = (dtc * jnp.exp(last - acum)).astype(BF16)
        cd = jnp.where(sub8 == 0, jnp.exp(last), 0.0)

        for g in range(SSM_GROUPS):
            gcols = slice(g * GROUP_WIDTH, (g + 1) * GROUP_WIDTH)
            bg = b_s[rows, g * SSM_STATE:(g + 1) * SSM_STATE]
            cg = c_s[rows, g * SSM_STATE:(g + 1) * SSM_STATE].astype(BF16)
            st_g = st_ref[:, gcols]
            yo_g = _mm(cg, st_g.astype(BF16))
            xw = (xs_s[rows, gcols] * _mm(w, e_ref[:, gcols])).astype(BF16)
            dec = _mm_expand(cd, e_ref[:, gcols])[0:1, :]
            st_ref[:, gcols] = st_g * dec + _mm(bg.T.astype(BF16), xw)
            cb_g = _mm_nt(cg, bg.astype(BF16)).astype(BF16)
            pairs = []
            for pq in range(HEADS_PER_GROUP // 2):
                h0 = g * HEADS_PER_GROUP + 2 * pq
                h1 = h0 + 1
                col0 = acum[:, h0:h0 + 1]
                col1 = acum[:, h1:h1 + 1]
                l0 = jnp.where(causal, jnp.exp(col0 - row_t[h0:h0 + 1, :]), 0.0).astype(BF16)
                l1 = jnp.where(causal, jnp.exp(col1 - row_t[h1:h1 + 1, :]), 0.0).astype(BF16)
                lhs = jnp.concatenate([cb_g * l0, cb_g * l1], axis=1)
                pcols = slice(h0 * SSM_HEAD_DIM, (h1 + 1) * SSM_HEAD_DIM)
                xp = xs_s[rows, pcols]
                xpb = xp.astype(BF16)
                rhs = jnp.concatenate([xpb * keep_lo, xpb * keep_hi], axis=0)
                yd = _mm(lhs, rhs)
                sc = jnp.where(lane_lo, jnp.exp(col0), jnp.exp(col1))
                lc = slice(2 * pq * SSM_HEAD_DIM, (2 * pq + 2) * SSM_HEAD_DIM)
                pairs.append(yd + sc * yo_g[:, lc] + dful_ref[:, pcols] * xp)
            yg = jnp.concatenate(pairs, axis=1) * z_s[rows, gcols]
            ms = jnp.mean(yg * yg, axis=-1, keepdims=True)
            yn_s[rows, gcols] = (yg * lax.rsqrt(ms + RMS_EPS) * gnw_ref[:, gcols]).astype(BF16)

    for c in range(tile // q):
        chunk(c)

    gb = _sigmoid(_mm(xn_s[...], wgb_ref[...]))
    ybg_ref[...] = gb * _mm(yn_s[...], wout_ref[...])

    @pl.when(t == pl.num_programs(1) - 1)
    def _():
        cst_ref[...] = pltpu.roll(hist_ref[...], SSM_CONV - 1, axis=0)[0:SSM_CONV - 1]
        for k in range(SSM_D_INNER // V7X_LANES):
            blk = slice(k * V7X_LANES, (k + 1) * V7X_LANES)
            sst_ref[blk, :] = st_ref[:, blk].T


def _ssd_call(x, nw, wz, wxbc, wdt, wgb, cw, cb, dtb, alog, dful, gnw, wout, e):
    b, s, d = x.shape
    tile = SSD_TILE
    consts = (nw, wz, wxbc, wdt, wgb, cw, cb, dtb, alog, dful, gnw, wout, e)
    return pl.pallas_call(
        _ssd_kernel,
        grid=(b, s // tile),
        in_specs=[pl.BlockSpec((None, tile, d), lambda i, j: (i, j, 0))]
        + [_whole(c.shape) for c in consts],
        out_specs=[
            pl.BlockSpec((None, tile, d), lambda i, j: (i, j, 0)),
            pl.BlockSpec((None, SSM_CONV - 1, SSM_CONV_DIM), lambda i, j: (i, 0, 0)),
            pl.BlockSpec((None, SSM_D_INNER, SSM_STATE), lambda i, j: (i, 0, 0)),
        ],
        out_shape=[
            jax.ShapeDtypeStruct((b, s, d), F32),
            jax.ShapeDtypeStruct((b, SSM_CONV - 1, SSM_CONV_DIM), F32),
            jax.ShapeDtypeStruct((b, SSM_D_INNER, SSM_STATE), F32),
        ],
        scratch_shapes=[
            pltpu.VMEM((V7X_SUBLANES, SSM_CONV_DIM), F32),
            pltpu.VMEM((SSM_STATE, SSM_D_INNER), F32),
            pltpu.VMEM((tile, d), BF16),
            pltpu.VMEM((tile, SSM_D_INNER), F32),
            pltpu.VMEM((tile, SSM_D_INNER), F32),
            pltpu.VMEM((tile, SSM_BC), F32),
            pltpu.VMEM((tile, SSM_BC), F32),
            pltpu.VMEM((tile, V7X_LANES), F32),
            pltpu.VMEM((tile, V7X_LANES), F32),
            pltpu.VMEM((tile, SSM_D_INNER), BF16),
        ] + [pltpu.VMEM(w.shape, BF16) for w in (wz, wxbc, wdt, wgb, wout, e)],
        compiler_params=_params(2),
        name="ssd_prompt",
    )(x, *consts)


def _attention(q, k_ref, v_ref):
    outs = []
    for h in range(ATTN_HEADS):
        cols = slice(h * ATTN_HEAD_DIM, (h + 1) * ATTN_HEAD_DIM)
        s = _mm_nt(q[:, cols].astype(BF16), k_ref[:, cols]) * (ATTN_HEAD_DIM ** -0.5)
        outs.append(_mm(_softmax_rows(s).astype(BF16), v_ref[:, cols]))
    return jnp.concatenate(outs, axis=1)


def _mix_kernel(x_ref, ybg_ref, kb_ref, vb_ref, nw_ref, wa_ref, wq_ref, wg_ref, cw_ref,
                wsc_ref, wao_ref, wmo_ref, x1_ref, cst_ref, hist_ref):
    t = pl.program_id(1)
    tile = x_ref.shape[0]

    @pl.when(t == 0)
    def _():
        hist_ref[...] = jnp.zeros_like(hist_ref)

    x = x_ref[...]
    xn = _rms(x, nw_ref[...]).astype(BF16)

    sc_b = _mm(xn, wa_ref[:, 0:SC_DIM])
    u = _mm(xn, wa_ref[:, SC_DIM:2 * SC_DIM]) * _mm(xn, wa_ref[:, 2 * SC_DIM:3 * SC_DIM])
    prev8 = hist_ref[...]
    hist_ref[...] = u[tile - V7X_SUBLANES:]
    cw = cw_ref[...]
    conv = _shift_rows(u, prev8, 2) * cw[0:1]
    conv = conv + _shift_rows(u, prev8, 1) * cw[1:2]
    conv = conv + u * cw[2:3]
    y_a = _mm((sc_b * conv).astype(BF16), wsc_ref[...])

    att = _attention(_mm(xn, wq_ref[...]), kb_ref, vb_ref)
    y_c = _mm(att.astype(BF16), wao_ref[...])

    g_a = _sigmoid(_mm(xn, wg_ref[:, 0:D_MODEL]))
    g_c = _sigmoid(_mm(xn, wg_ref[:, D_MODEL:2 * D_MODEL]))
    merged = g_a * y_a + ybg_ref[...] + g_c * y_c
    x1_ref[...] = x + _mm(merged.astype(BF16), wmo_ref[...])

    @pl.when(t == pl.num_programs(1) - 1)
    def _():
        cst_ref[...] = pltpu.roll(hist_ref[...], SC_WIDTH - 1, axis=0)[0:SC_WIDTH - 1]


def _mix_call(x, ybg, kb, vb, nw, wa, wq, wg, cw, wsc, wao, wmo):
    b, s, d = x.shape
    tile = MIX_TILE
    consts = (nw, wa, wq, wg, cw, wsc, wao, wmo)
    tok = pl.BlockSpec((None, tile, d), lambda i, j: (i, j, 0))
    mem = pl.BlockSpec((None, MEM_LEN, d), lambda i, j: (i, 0, 0))
    return pl.pallas_call(
        _mix_kernel,
        grid=(b, s // tile),
        in_specs=[tok, tok, mem, mem] + [_whole(c.shape) for c in consts],
        out_specs=[tok, pl.BlockSpec((None, SC_WIDTH - 1, SC_DIM), lambda i, j: (i, 0, 0))],
        out_shape=[
            jax.ShapeDtypeStruct((b, s, d), F32),
            jax.ShapeDtypeStruct((b, SC_WIDTH - 1, SC_DIM), F32),
        ],
        scratch_shapes=[pltpu.VMEM((V7X_SUBLANES, SC_DIM), F32)],
        compiler_params=_params(2),
        name="mix_prompt",
    )(x, ybg, kb, vb, *consts)


def _ffn_body(x, nw, wg_ref, wu_ref, wd_ref, fw):
    xn = _rms(x, nw).astype(BF16)
    h = _silu(_mm(xn, wg_ref[...])) * _mm(xn, wu_ref[...])
    x2 = x + _mm(h.astype(BF16), wd_ref[...])
    return _rms(x2, fw)


def _ffn_kernel(x_ref, nw_ref, wg_ref, wu_ref, wd_ref, fw_ref, o_ref):
    o_ref[...] = _ffn_body(x_ref[...], nw_ref[...], wg_ref, wu_ref, wd_ref, fw_ref[...])


def _ffn_call(x, nw, wg, wu, wd, fw):
    b, s, d = x.shape
    tile = FFN_TILE
    consts = (nw, wg, wu, wd, fw)
    tok = pl.BlockSpec((None, tile, d), lambda i, j: (i, j, 0))
    return pl.pallas_call(
        _ffn_kernel,
        grid=(b, s // tile),
        in_specs=[tok] + [_whole(c.shape) for c in consts],
        out_specs=tok,
        out_shape=jax.ShapeDtypeStruct((b, s, d), F32),
        compiler_params=_params(2),
        name="ffn_prompt",
    )(x, *consts)


def _sfront_kernel(x_ref, sconv_ref, ssmc_ref, nw_ref, wa_ref, wz_ref, wxbc_ref, wdt_ref,
                   wq_ref, wg_ref, scw_ref, cw_ref, cb_ref, dtb_ref, alog_ref, wsc_ref, e_ref,
                   gaya_ref, gb_ref, gc_ref, z_ref, xs_ref, dtxt_ref, dect_ref, b_ref, c_ref,
                   q_ref, sconv_o, ssmc_o):
    xn = _rms(x_ref[...], nw_ref[...]).astype(BF16)

    sc_b = _mm(xn, wa_ref[:, 0:SC_DIM])
    u = _mm(xn, wa_ref[:, SC_DIM:2 * SC_DIM]) * _mm(xn, wa_ref[:, 2 * SC_DIM:3 * SC_DIM])
    h0 = sconv_ref[:, 0:SC_DIM]
    h1 = sconv_ref[:, SC_DIM:2 * SC_DIM]
    scw = scw_ref[...]
    conv = h0 * scw[0:1] + h1 * scw[1:2] + u * scw[2:3]
    sconv_o[:, 0:SC_DIM] = h1
    sconv_o[:, SC_DIM:2 * SC_DIM] = u
    y_a = _mm((sc_b * conv).astype(BF16), wsc_ref[...])
    gaya_ref[...] = _sigmoid(_mm(xn, wg_ref[:, 0:D_MODEL])) * y_a
    gb_ref[...] = _sigmoid(_mm(xn, wg_ref[:, D_MODEL:2 * D_MODEL]))
    gc_ref[...] = _sigmoid(_mm(xn, wg_ref[:, 2 * D_MODEL:3 * D_MODEL]))
    z_ref[...] = _mm(xn, wz_ref[...])
    q_ref[...] = _mm(xn, wq_ref[...])

    dt = _softplus(_mm(xn, wdt_ref[...]) + dtb_ref[...])
    dect_ref[...] = jnp.exp(dt * (-jnp.exp(alog_ref[...]))).T
    dtexp = _mm_expand(dt, e_ref[...])

    for j in range(SSM_CONV_DIM // CONV_COLS):
        lo = j * CONV_COLS
        cols = slice(lo, lo + CONV_COLS)
        xbc = _mm(xn, wxbc_ref[:, cols])
        p0 = ssmc_ref[0, :, cols]
        p1 = ssmc_ref[1, :, cols]
        p2 = ssmc_ref[2, :, cols]
        cw = cw_ref[:, cols]
        conv = p0 * cw[0:1] + p1 * cw[1:2] + p2 * cw[2:3] + xbc * cw[3:4]
        ssmc_o[0, :, cols] = p1
        ssmc_o[1, :, cols] = p2
        ssmc_o[2, :, cols] = xbc
        act = _silu(conv + cb_ref[:, cols])
        if lo < SSM_D_INNER:
            xs_ref[:, cols] = act
            dtx = act * dtexp[:, cols]
            for k in range(CONV_COLS // V7X_LANES):
                r0 = lo + k * V7X_LANES
                dtxt_ref[r0:r0 + V7X_LANES, :] = dtx[:, k * V7X_LANES:(k + 1) * V7X_LANES].T
        elif lo < SSM_D_INNER + SSM_BC:
            b_ref[:, lo - SSM_D_INNER:lo - SSM_D_INNER + CONV_COLS] = act
        else:
            off = lo - SSM_D_INNER - SSM_BC
            c_ref[:, off:off + CONV_COLS] = act


def _sfront_call(x, sconv, ssmc, nw, wa, wz, wxbc, wdt, wq, wg, scw, cw, cb, dtb, alog, wsc, e):
    n = x.shape[0]
    args = (x, sconv, ssmc, nw, wa, wz, wxbc, wdt, wq, wg, scw, cw, cb, dtb, alog, wsc, e)
    shapes = [
        (n, D_MODEL), (n, D_MODEL), (n, D_MODEL),
        (n, SSM_D_INNER), (n, SSM_D_INNER),
        (SSM_D_INNER, n), (V7X_LANES, n),
        (n, SSM_BC), (n, SSM_BC),
        (n, D_MODEL),
        sconv.shape, ssmc.shape,
    ]
    return pl.pallas_call(
        _sfront_kernel,
        grid=(1,),
        in_specs=[_whole(a.shape) for a in args],
        out_specs=[_whole(s) for s in shapes],
        out_shape=[jax.ShapeDtypeStruct(s, F32) for s in shapes],
        compiler_params=_params(1),
        name="sample_front",
    )(*args)


def _sssm_kernel(s_ref, dtxt_ref, dect_ref, b_ref, c_ref, so_ref, y_ref):
    i = pl.program_id(0)
    n = dtxt_ref.shape[1]
    lane = lax.broadcasted_iota(jnp.int32, (1, n), 1)
    sub = lax.broadcasted_iota(jnp.int32, (V7X_SUBLANES, SSM_STATE), 0)
    for k in range(SAMPLE_BLOCK):
        onehot = (lane == i * SAMPLE_BLOCK + k).astype(F32)
        xcol = jnp.sum(dtxt_ref[...] * onehot, axis=-1, keepdims=True)
        dcol = jnp.sum(dect_ref[0:SSM_HEADS, :] * onehot, axis=-1, keepdims=True)
        for h in range(SSM_HEADS):
            g = h // HEADS_PER_GROUP
            rows = slice(h * SSM_HEAD_DIM, (h + 1) * SSM_HEAD_DIM)
            brow = b_ref[k:k + 1, g * SSM_STATE:(g + 1) * SSM_STATE]
            so_ref[k, rows, :] = s_ref[k, rows, :] * dcol[h:h + 1, :] + xcol[rows, :] * brow
        c8 = jnp.zeros((V7X_SUBLANES, SSM_STATE), F32)
        for g in range(SSM_GROUPS):
            c8 = jnp.where(sub == g, c_ref[k:k + 1, g * SSM_STATE:(g + 1) * SSM_STATE], c8)
        y8 = _mm_nt(c8.astype(BF16), so_ref[k].astype(BF16))
        y_ref[k:k + 1, :] = jnp.concatenate(
            [y8[g:g + 1, g * GROUP_WIDTH:(g + 1) * GROUP_WIDTH] for g in range(SSM_GROUPS)], axis=1)


def _sssm_call(state, dtxt, dect, bm, cm):
    n = state.shape[0]
    blk = SAMPLE_BLOCK
    st = pl.BlockSpec((blk, SSM_D_INNER, SSM_STATE), lambda i: (i, 0, 0))
    return pl.pallas_call(
        _sssm_kernel,
        grid=(n // blk,),
        in_specs=[st, _whole(dtxt.shape), _whole(dect.shape),
                  pl.BlockSpec((blk, SSM_BC), lambda i: (i, 0)),
                  pl.BlockSpec((blk, SSM_BC), lambda i: (i, 0))],
        out_specs=[st, pl.BlockSpec((blk, SSM_D_INNER), lambda i: (i, 0))],
        out_shape=[jax.ShapeDtypeStruct(state.shape, F32),
                   jax.ShapeDtypeStruct((n, SSM_D_INNER), F32)],
        compiler_params=_params(1),
        name="sample_ssm",
    )(state, dtxt, dect, bm, cm)


def _cache_rows(c):
    n = c.shape[1]
    c = c.reshape(n, MEM_LEN, ATTN_HEADS, ATTN_HEAD_DIM // V7X_LANES, V7X_LANES)
    return c.transpose(0, 1, 3, 2, 4).reshape(n * MEM_LEN * CACHE_ROWS, V7X_LANES)


def _cache_head(ref, k, h):
    base = k * MEM_LEN * CACHE_ROWS
    halves = [ref[pl.ds(base + j * ATTN_HEADS + h, MEM_LEN, stride=CACHE_ROWS), :]
              for j in range(ATTN_HEAD_DIM // V7X_LANES)]
    return jnp.concatenate(halves, axis=1).astype(BF16)


def _sattn_kernel(q_ref, k_ref, v_ref, o_ref):
    sub = lax.broadcasted_iota(jnp.int32, (SAMPLE_BLOCK, MEM_LEN), 0)
    sub_o = lax.broadcasted_iota(jnp.int32, (SAMPLE_BLOCK, ATTN_HEAD_DIM), 0)
    outs = []
    for h in range(ATTN_HEADS):
        qh = q_ref[:, h * ATTN_HEAD_DIM:(h + 1) * ATTN_HEAD_DIM].astype(BF16)
        s = jnp.zeros((SAMPLE_BLOCK, MEM_LEN), F32)
        for k in range(SAMPLE_BLOCK):
            s = jnp.where(sub == k, _mm_nt(qh, _cache_head(k_ref, k, h)), s)
        p = _softmax_rows(s * (ATTN_HEAD_DIM ** -0.5)).astype(BF16)
        o = jnp.zeros((SAMPLE_BLOCK, ATTN_HEAD_DIM), F32)
        for k in range(SAMPLE_BLOCK):
            o = jnp.where(sub_o == k, _mm(p, _cache_head(v_ref, k, h)), o)
        outs.append(o)
    o_ref[...] = jnp.concatenate(outs, axis=1)


def _sattn_call(q, ck, cv):
    n = q.shape[0]
    blk = SAMPLE_BLOCK
    kv = pl.BlockSpec((blk * MEM_LEN * CACHE_ROWS, V7X_LANES), lambda i: (i, 0))
    row = pl.BlockSpec((blk, D_MODEL), lambda i: (i, 0))
    return pl.pallas_call(
        _sattn_kernel,
        grid=(n // blk,),
        in_specs=[row, kv, kv],
        out_specs=row,
        out_shape=jax.ShapeDtypeStruct((n, D_MODEL), F32),
        compiler_params=_params(1),
        name="sample_attn",
    )(q, _cache_rows(ck), _cache_rows(cv))


def _sback_kernel(x_ref, y_ref, xs_ref, z_ref, gaya_ref, gb_ref, gc_ref, att_ref, dful_ref,
                  gnw_ref, wout_ref, wao_ref, wmo_ref, fnw_ref, wg_ref, wu_ref, wd_ref, fw_ref,
                  o_ref):
    y = (y_ref[...] + dful_ref[...] * xs_ref[...]) * _silu(z_ref[...])
    y_b = _mm(_group_norm(y, gnw_ref[...]).astype(BF16), wout_ref[...])
    y_c = _mm(att_ref[...].astype(BF16), wao_ref[...])
    merged = gaya_ref[...] + gb_ref[...] * y_b + gc_ref[...] * y_c
    x1 = x_ref[...] + _mm(merged.astype(BF16), wmo_ref[...])
    o_ref[...] = _ffn_body(x1, fnw_ref[...], wg_ref, wu_ref, wd_ref, fw_ref[...])


def _sback_call(*args):
    n = args[0].shape[0]
    return pl.pallas_call(
        _sback_kernel,
        grid=(1,),
        in_specs=[_whole(a.shape) for a in args],
        out_specs=_whole((n, D_MODEL)),
        out_shape=jax.ShapeDtypeStruct((n, D_MODEL), F32),
        compiler_params=_params(1),
        name="sample_back",
    )(*args)


def kernel(x_prompt, x_sample, mem_prompt, cache_mem_k, cache_mem_v, state_conv, state_ssm_conv, state_ssm, norm_mix_w, w_in, sc_conv_w, w_sc_out, ssm_conv_w, ssm_conv_b, ssm_dt_bias, ssm_a_log, ssm_d, ssm_norm_w, w_ssm_out, norm_mem_w, w_mem_k, w_mem_v, w_attn_o, w_merge_o, norm_ffn_w, w_ffn_gate, w_ffn_up, w_ffn_down, norm_final_w):
    depth = w_in.shape[0]
    assert depth == 1
    bp = x_prompt.shape[0]
    ns = x_sample.shape[0]

    wi = w_in[0]
    o = 0
    wa = wi[:, o:o + 3 * SC_DIM].astype(BF16); o += 3 * SC_DIM
    wz = wi[:, o:o + SSM_D_INNER].astype(BF16); o += SSM_D_INNER
    wxbc = wi[:, o:o + SSM_CONV_DIM].astype(BF16); o += SSM_CONV_DIM
    wdt = jnp.pad(wi[:, o:o + SSM_HEADS], ((0, 0), (0, V7X_LANES - SSM_HEADS))).astype(BF16); o += SSM_HEADS
    wq = wi[:, o:o + D_MODEL].astype(BF16); o += D_MODEL
    wg = wi[:, o:o + 3 * D_MODEL].astype(BF16)
    wg_ac = jnp.concatenate([wg[:, 0:D_MODEL], wg[:, 2 * D_MODEL:]], axis=1)
    wg_b = wg[:, D_MODEL:2 * D_MODEL]
    row = lambda v: v.reshape(1, -1).astype(F32)
    pad_heads = lambda v: jnp.pad(row(v), ((0, 0), (0, V7X_LANES - SSM_HEADS)))
    nmix = row(norm_mix_w[0])
    dtb = pad_heads(ssm_dt_bias[0])
    alog = pad_heads(ssm_a_log[0])
    dful = row(jnp.repeat(ssm_d[0], SSM_HEAD_DIM))
    gnw = row(ssm_norm_w[0])
    cb = row(ssm_conv_b[0])
    cw = ssm_conv_w[0]
    scw = sc_conv_w[0]
    wsc = w_sc_out[0].astype(BF16)
    wout = w_ssm_out[0].astype(BF16)
    wao = w_attn_o[0].astype(BF16)
    wmo = w_merge_o[0].astype(BF16)
    wfg = w_ffn_gate[0].astype(BF16)
    wfu = w_ffn_up[0].astype(BF16)
    wfd = w_ffn_down[0].astype(BF16)
    nffn = row(norm_ffn_w[0])
    nfin = row(norm_final_w)
    expand = (jnp.arange(V7X_LANES)[:, None] == (jnp.arange(SSM_D_INNER)[None, :] // SSM_HEAD_DIM)).astype(BF16)

    mk, mv, mkb, mvb = _mem_call(mem_prompt, row(norm_mem_w[0]), w_mem_k[0].astype(BF16), w_mem_v[0].astype(BF16))
    ybg, p_ssmc, p_ssm = _ssd_call(x_prompt, nmix, wz, wxbc, wdt, wg_b, cw, cb, dtb, alog, dful, gnw, wout, expand)
    x1, p_conv = _mix_call(x_prompt, ybg, mkb, mvb, nmix, wa, wq, wg_ac, scw, wsc, wao, wmo)
    y_prompt = _ffn_call(x1, nffn, wfg, wfu, wfd, nfin)

    xs2 = x_sample.reshape(ns, D_MODEL)
    (gaya, gb, gc, z, xs, dtxt, dect, bm, cm, q, s_conv, s_ssmc) = _sfront_call(
        xs2, state_conv[0].reshape(ns, -1), jnp.swapaxes(state_ssm_conv[0], 0, 1),
        nmix, wa, wz, wxbc, wdt, wq, wg, scw, cw, cb, dtb, alog, wsc, expand)
    s_ssm, y_s = _sssm_call(state_ssm[0].reshape(ns, SSM_D_INNER, SSM_STATE), dtxt, dect, bm, cm)
    att = _sattn_call(q, cache_mem_k, cache_mem_v)
    y_sample = _sback_call(xs2, y_s, xs, z, gaya, gb, gc, att, dful, gnw, wout, wao, wmo,
                           nffn, wfg, wfu, wfd, nfin)

    def from_rows(r):
        r = r.reshape(bp, MEM_LEN, ATTN_HEAD_DIM // V7X_LANES, ATTN_HEADS, V7X_LANES)
        return r.transpose(0, 1, 3, 2, 4).reshape(depth, bp, MEM_LEN, ATTN_HEADS, ATTN_HEAD_DIM)

    state_shape = (SSM_HEADS, SSM_HEAD_DIM, SSM_STATE)
    return (
        y_prompt,
        y_sample.reshape(ns, 1, D_MODEL),
        from_rows(mk),
        from_rows(mv),
        p_conv.reshape(depth, bp, SC_WIDTH - 1, SC_DIM),
        p_ssmc.reshape(depth, bp, SSM_CONV - 1, SSM_CONV_DIM),
        p_ssm.reshape((depth, bp) + state_shape),
        s_conv.reshape(depth, ns, SC_WIDTH - 1, SC_DIM),
        jnp.swapaxes(s_ssmc, 0, 1).reshape(depth, ns, SSM_CONV - 1, SSM_CONV_DIM),
        s_ssm.reshape((depth, ns) + state_shape),
    )
```

```python
import jax
import jax.numpy as jnp
from jax import lax
from jax.experimental import pallas as pl
from jax.experimental.pallas import tpu as pltpu

F32 = jnp.float32
BF16 = jnp.bfloat16

D_MODEL = 1024
RMS_EPS = 1e-6
LOG2_E = 1.4426950408889634
SC_DIM = D_MODEL
SC_WIDTH = 3
SSM_D_INNER = 2 * D_MODEL
SSM_HEAD_DIM = 64
SSM_HEADS = SSM_D_INNER // SSM_HEAD_DIM
SSM_STATE = 128
SSM_GROUPS = 4
SSM_CONV = 4
SSM_CHUNK = 128
SSM_BC = SSM_GROUPS * SSM_STATE
SSM_CONV_DIM = SSM_D_INNER + 2 * SSM_BC
HEADS_PER_GROUP = SSM_HEADS // SSM_GROUPS
GROUP_WIDTH = SSM_D_INNER // SSM_GROUPS
MEM_LEN = 256
ATTN_HEADS = 4
ATTN_HEAD_DIM = D_MODEL // ATTN_HEADS
FFN_HIDDEN = ((8 * D_MODEL // 3 + 255) // 256) * 256

V7X_LANES = 128
V7X_SUBLANES = 8
V7X_VMEM_BYTES = 64 * 1024 * 1024
VMEM_LIMIT_BYTES = V7X_VMEM_BYTES - 8 * 1024 * 1024
CACHE_ROWS = ATTN_HEADS * ATTN_HEAD_DIM // V7X_LANES

SSD_TILE = 512
SSD_SUB = 256
MIX_TILE = 512
FFN_TILE = 512
SAMPLE_BLOCK = 8
CONV_COLS = 512


def _params(n_grid, flags=None):
    return pltpu.CompilerParams(
        dimension_semantics=("arbitrary",) * n_grid,
        vmem_limit_bytes=VMEM_LIMIT_BYTES,
        flags=flags,
    )


def _whole(shape):
    nd = len(shape)
    return pl.BlockSpec(shape, lambda *_: (0,) * nd)


def _mm(a, b):
    return jnp.dot(a, b, preferred_element_type=F32)


def _mm_nt(a, b):
    return lax.dot_general(a, b, (((1,), (1,)), ((), ())), preferred_element_type=F32)


def _split3(x):
    hi = x.astype(BF16)
    r = x - hi.astype(F32)
    mid = r.astype(BF16)
    lo = (r - mid.astype(F32)).astype(BF16)
    return hi, mid, lo


def _mm_sel(sel, x):
    hi, mid, lo = _split3(x)
    return _mm(sel, hi) + _mm(sel, mid) + _mm(sel, lo)


def _mm_expand(x, sel):
    hi, mid, lo = _split3(x)
    return _mm(hi, sel) + _mm(mid, sel) + _mm(lo, sel)


def _rms(x, w):
    return x * lax.rsqrt(jnp.mean(x * x, axis=-1, keepdims=True) + RMS_EPS) * w


def _sigmoid(x):
    return 1.0 / (1.0 + jnp.exp2(x * (-LOG2_E)))


def _silu(x):
    return x * _sigmoid(x)


def _softplus(x):
    return jnp.maximum(x, 0.0) + jnp.log1p(jnp.exp(-jnp.abs(x)))


def _shift_rows(u, prev8, k):
    r = pltpu.roll(u, k, axis=0)
    p = pltpu.roll(prev8, k, axis=0)
    row = lax.broadcasted_iota(jnp.int32, prev8.shape, 0)
    head = jnp.where(row < k, p, r[:V7X_SUBLANES])
    return jnp.concatenate([head, r[V7X_SUBLANES:]], axis=0)


def _softmax_rows(s):
    m = jnp.max(s, axis=-1, keepdims=True)
    p = jnp.exp(s - m)
    return p / jnp.sum(p, axis=-1, keepdims=True)


def _group_norm(y, w):
    outs = []
    for g in range(SSM_GROUPS):
        cols = slice(g * GROUP_WIDTH, (g + 1) * GROUP_WIDTH)
        yg = y[:, cols]
        ms = jnp.mean(yg * yg, axis=-1, keepdims=True)
        outs.append(yg * lax.rsqrt(ms + RMS_EPS) * w[:, cols])
    return jnp.concatenate(outs, axis=1)


def _mem_kernel(m_ref, nw_ref, wk_ref, wv_ref, k_ref, v_ref, kb_ref, vb_ref):
    mn = _rms(m_ref[...], nw_ref[...]).astype(BF16)
    k = _mm(mn, wk_ref[...])
    v = _mm(mn, wv_ref[...])
    kb_ref[...] = k.astype(BF16)
    vb_ref[...] = v.astype(BF16)
    for j in range(ATTN_HEAD_DIM // V7X_LANES):
        for h in range(ATTN_HEADS):
            rows = pl.ds(j * ATTN_HEADS + h, MEM_LEN, stride=CACHE_ROWS)
            lo = h * ATTN_HEAD_DIM + j * V7X_LANES
            k_ref[rows, :] = k[:, lo:lo + V7X_LANES]
            v_ref[rows, :] = v[:, lo:lo + V7X_LANES]


def _mem_call(mem, nw, wk, wv):
    b = mem.shape[0]
    blk = pl.BlockSpec((None, MEM_LEN, D_MODEL), lambda i: (i, 0, 0))
    rows = pl.BlockSpec((None, MEM_LEN * CACHE_ROWS, V7X_LANES), lambda i: (i, 0, 0))
    rows_shape = jax.ShapeDtypeStruct((b, MEM_LEN * CACHE_ROWS, V7X_LANES), F32)
    return pl.pallas_call(
        _mem_kernel,
        grid=(b,),
        in_specs=[blk, _whole(nw.shape), _whole(wk.shape), _whole(wv.shape)],
        out_specs=[rows, rows, blk, blk],
        out_shape=[
            rows_shape,
            rows_shape,
            jax.ShapeDtypeStruct(mem.shape, BF16),
            jax.ShapeDtypeStruct(mem.shape, BF16),
        ],
        compiler_params=_params(1),
        name="mem_kv",
    )(mem, nw, wk, wv)


def _ssd_kernel(x_ref, nw_ref, wz_ref, wxbc_ref, wdt_ref, wgb_ref, cw_ref, cb_ref,
                dtb_ref, alog_ref, dful_ref, gnw_ref, wout_ref, e_ref,
                ybg_ref, cst_ref, sst_ref,
                hist_ref, st_ref, *scratch):
    t = pl.program_id(1)
    tile = x_ref.shape[0]
    q = SSM_CHUNK
    n_sub = tile // SSD_SUB
    sets = [scratch[k * 8:(k + 1) * 8] for k in range(n_sub)]

    @pl.when(t == 0)
    def _():
        hist_ref[...] = jnp.zeros_like(hist_ref)
        st_ref[...] = jnp.zeros_like(st_ref)

    def project(sub):
        xn_s, z_s, xs_s, b_s, c_s, dt_s, da_s, _ = sets[sub]
        xn_s[...] = _rms(x_ref[sub * SSD_SUB:(sub + 1) * SSD_SUB, :], nw_ref[...]).astype(BF16)
        dt = _softplus(_mm(xn_s[...], wdt_ref[...]) + dtb_ref[...])
        dt_s[...] = dt
        da_s[...] = dt * (-jnp.exp(alog_ref[...]))
        n_blocks = SSM_CONV_DIM // CONV_COLS
        first_bc = SSM_D_INNER // CONV_COLS
        for j in list(range(first_bc, n_blocks)) + list(range(first_bc)):
            cols = slice(j * CONV_COLS, (j + 1) * CONV_COLS)
            u = _mm(xn_s[...], wxbc_ref[:, cols])
            prev8 = hist_ref[:, cols]
            hist_ref[:, cols] = u[SSD_SUB - V7X_SUBLANES:]
            cw = cw_ref[:, cols]
            conv = _shift_rows(u, prev8, 3) * cw[0:1]
            conv = conv + _shift_rows(u, prev8, 2) * cw[1:2]
            conv = conv + _shift_rows(u, prev8, 1) * cw[2:3]
            conv = conv + u * cw[3:4]
            act = _silu(conv + cb_ref[:, cols])
            lo = j * CONV_COLS
            if lo < SSM_D_INNER:
                xs_s[:, cols] = act
            elif lo < SSM_D_INNER + SSM_BC:
                b_s[:, lo - SSM_D_INNER:lo - SSM_D_INNER + CONV_COLS] = act
            else:
                off = lo - SSM_D_INNER - SSM_BC
                c_s[:, off:off + CONV_COLS] = act
        z_s[...] = _silu(_mm(xn_s[...], wz_ref[...]))

    ri = lax.broadcasted_iota(jnp.int32, (q, q), 0)
    ci = lax.broadcasted_iota(jnp.int32, (q, q), 1)
    causal = ri >= ci
    tri = jnp.where(causal, 1.0, 0.0).astype(BF16)
    lane_lo = ci < SSM_HEAD_DIM
    keep_lo = jnp.where(lane_lo, 1.0, 0.0).astype(BF16)
    keep_hi = jnp.where(lane_lo, 0.0, 1.0).astype(BF16)
    sub8 =lax.broadcasted_iota(jnp.int32, (V7X_SUBLANES, q), 0)

    groups = range(SSM_GROUPS)
    gsl = [slice(g * GROUP_WIDTH, (g + 1) * GROUP_WIDTH) for g in groups]
    nsl = [slice(g * SSM_STATE, (g + 1) * SSM_STATE) for g in groups]

    def scan(sub, chunks):
        _, z_s, xs_s, b_s, c_s, dt_s, da_s, yn_s = sets[sub]
        rows = {c: pl.ds(c * q, q) for c in chunks}

        acum, row_t, w, cd = {}, {}, {}, {}
        for c in chunks:
            a = _mm_sel(tri, da_s[rows[c], :])
            last = a[q - 1:q, :]
            dtc = dt_s[rows[c], :]
            acum[c] = a
            row_t[c] = a.T - jnp.log(dtc.T)
            w[c] = (dtc * jnp.exp(last - a)).astype(BF16)
            cd[c] = jnp.where(sub8 == 0, jnp.exp(last), 0.0)

        bg = {(c, g): b_s[rows[c], nsl[g]] for c in chunks for g in groups}
        cg = {(c, g): c_s[rows[c], nsl[g]].astype(BF16) for c in chunks for g in groups}
        cb = {k: _mm_nt(cg[k], bg[k].astype(BF16)).astype(BF16) for k in bg}
        wx = {(c, g): _mm(w[c], e_ref[:, gsl[g]]) for c in chunks for g in groups}
        dec = {(c, g): _mm_expand(cd[c], e_ref[:, gsl[g]])[0:1, :] for c in chunks for g in groups}

        yo = {}
        for c in chunks:
            st = [st_ref[:, gsl[g]] for g in groups]
            for g in groups:
                yo[c, g] = _mm(cg[c, g], st[g].astype(BF16))
            for g in groups:
                xw = (xs_s[rows[c], gsl[g]] * wx[c, g]).astype(BF16)
                st_ref[:, gsl[g]] = st[g] * dec[c, g] + _mm(bg[c, g].T.astype(BF16), xw)

        yd = {}
        for c in chunks:
            for g in groups:
                for pq in range(HEADS_PER_GROUP // 2):
                    h0 = g * HEADS_PER_GROUP + 2 * pq
                    h1 = h0 + 1
                    l0 = jnp.where(causal, jnp.exp(acum[c][:, h0:h0 + 1] - row_t[c][h0:h0 + 1, :]), 0.0)
                    l1 = jnp.where(causal, jnp.exp(acum[c][:, h1:h1 + 1] - row_t[c][h1:h1 + 1, :]), 0.0)
                    lhs = jnp.concatenate([cb[c, g] * l0.astype(BF16), cb[c, g] * l1.astype(BF16)], axis=1)
                    xpb = xs_s[rows[c], h0 * SSM_HEAD_DIM:(h1 + 1) * SSM_HEAD_DIM].astype(BF16)
                    rhs = jnp.concatenate([xpb * keep_lo, xpb * keep_hi], axis=0)
                    yd[c, g, pq] = _mm(lhs, rhs)

        for c in chunks:
            for g in groups:
                pairs = []
                for pq in range(HEADS_PER_GROUP // 2):
                    h0 = g * HEADS_PER_GROUP + 2 * pq
                    h1 = h0 + 1
                    pcols = slice(h0 * SSM_HEAD_DIM, (h1 + 1) * SSM_HEAD_DIM)
                    sc = jnp.where(lane_lo, jnp.exp(acum[c][:, h0:h0 + 1]), jnp.exp(acum[c][:, h1:h1 + 1]))
                    lc = slice(2 * pq * SSM_HEAD_DIM, (2 * pq + 2) * SSM_HEAD_DIM)
                    pairs.append(yd[c, g, pq] + sc * yo[c, g][:, lc] + dful_ref[:, pcols] * xs_s[rows[c], pcols])
                yg = jnp.concatenate(pairs, axis=1) * z_s[rows[c], gsl[g]]
                ms = jnp.mean(yg * yg, axis=-1, keepdims=True)
                yn_s[rows[c], gsl[g]] = (yg * lax.rsqrt(ms + RMS_EPS) * gnw_ref[:, gsl[g]]).astype(BF16)

    for sub in range(n_sub):
        project(sub)
    for sub in range(n_sub):
        for c in range(SSD_SUB // q):
            scan(sub, [c])
        xn_s, yn_s = sets[sub][0], sets[sub][7]
        gb = _sigmoid(_mm(xn_s[...], wgb_ref[...]))
        ybg_ref[sub * SSD_SUB:(sub + 1) * SSD_SUB, :] = gb * _mm(yn_s[...], wout_ref[...])

    @pl.when(t == pl.num_programs(1) - 1)
    def _():
        cst_ref[...] = pltpu.roll(hist_ref[...], SSM_CONV - 1, axis=0)[0:SSM_CONV - 1]
        for k in range(SSM_D_INNER // V7X_LANES):
            blk = slice(k * V7X_LANES, (k + 1) * V7X_LANES)
            sst_ref[blk, :] = st_ref[:, blk].T


def _ssd_call(x, nw, wz, wxbc, wdt, wgb, cw, cb, dtb, alog, dful, gnw, wout, e):
    b, s, d = x.shape
    tile = SSD_TILE
    consts = (nw, wz, wxbc, wdt, wgb, cw, cb, dtb, alog, dful, gnw, wout, e)
    return pl.pallas_call(
        _ssd_kernel,
        grid=(b, s // tile),
        in_specs=[pl.BlockSpec((None, tile, d), lambda i, j: (i, j, 0))]
        + [_whole(c.shape) for c in consts],
        out_specs=[
            pl.BlockSpec((None, tile, d), lambda i, j: (i, j, 0)),
            pl.BlockSpec((None, SSM_CONV - 1, SSM_CONV_DIM), lambda i, j: (i, 0, 0)),
            pl.BlockSpec((None, SSM_D_INNER, SSM_STATE), lambda i, j: (i, 0, 0)),
        ],
        out_shape=[
            jax.ShapeDtypeStruct((b, s, d), F32),
            jax.ShapeDtypeStruct((b, SSM_CONV - 1, SSM_CONV_DIM), F32),
            jax.ShapeDtypeStruct((b, SSM_D_INNER, SSM_STATE), F32),
        ],
        scratch_shapes=[
            pltpu.VMEM((V7X_SUBLANES, SSM_CONV_DIM), F32),
            pltpu.VMEM((SSM_STATE, SSM_D_INNER), F32),
        ] + [
            pltpu.VMEM((SSD_SUB, d), BF16),
            pltpu.VMEM((SSD_SUB, SSM_D_INNER), F32),
            pltpu.VMEM((SSD_SUB, SSM_D_INNER), F32),
            pltpu.VMEM((SSD_SUB, SSM_BC), F32),
            pltpu.VMEM((SSD_SUB, SSM_BC), F32),
            pltpu.VMEM((SSD_SUB, V7X_LANES), F32),
            pltpu.VMEM((SSD_SUB, V7X_LANES), F32),
            pltpu.VMEM((SSD_SUB, SSM_D_INNER), BF16),
        ] * (tile // SSD_SUB),
        compiler_params=_params(2),
        name="ssd_prompt",
    )(x, *consts)


def _attention(q, k_ref, v_ref):
    outs = []
    for h in range(ATTN_HEADS):
        cols = slice(h * ATTN_HEAD_DIM, (h + 1) * ATTN_HEAD_DIM)
        s = _mm_nt(q[:, cols].astype(BF16), k_ref[:, cols]) * (ATTN_HEAD_DIM ** -0.5)
        outs.append(_mm(_softmax_rows(s).astype(BF16), v_ref[:, cols]))
    return jnp.concatenate(outs, axis=1)


def _mix_kernel(x_ref, ybg_ref, kb_ref, vb_ref, nw_ref, wa_ref, wq_ref, wg_ref, cw_ref,
                wsc_ref, wao_ref, wmo_ref, x1_ref, cst_ref, hist_ref):
    t = pl.program_id(1)
    tile = x_ref.shape[0]

    @pl.when(t == 0)
    def _():
        hist_ref[...] = jnp.zeros_like(hist_ref)

    x = x_ref[...]
    xn = _rms(x, nw_ref[...]).astype(BF16)

    sc_b = _mm(xn, wa_ref[:, 0:SC_DIM])
    u = _mm(xn, wa_ref[:, SC_DIM:2 * SC_DIM]) * _mm(xn, wa_ref[:, 2 * SC_DIM:3 * SC_DIM])
    prev8 = hist_ref[...]
    hist_ref[...] = u[tile - V7X_SUBLANES:]
    cw = cw_ref[...]
    conv = _shift_rows(u, prev8, 2) * cw[0:1]
    conv = conv + _shift_rows(u, prev8, 1) * cw[1:2]
    conv = conv + u * cw[2:3]
    y_a = _mm((sc_b * conv).astype(BF16), wsc_ref[...])

    att = _attention(_mm(xn, wq_ref[...]), kb_ref, vb_ref)
    y_c = _mm(att.astype(BF16), wao_ref[...])

    g_a = _sigmoid(_mm(xn, wg_ref[:, 0:D_MODEL]))
    g_c = _sigmoid(_mm(xn, wg_ref[:, D_MODEL:2 * D_MODEL]))
    merged = g_a * y_a + ybg_ref[...] + g_c * y_c
    x1_ref[...] = x + _mm(merged.astype(BF16), wmo_ref[...])

    @pl.when(t == pl.num_programs(1) - 1)
    def _():
        cst_ref[...] = pltpu.roll(hist_ref[...], SC_WIDTH - 1, axis=0)[0:SC_WIDTH - 1]


def _mix_call(x, ybg, kb, vb, nw, wa, wq, wg, cw, wsc, wao, wmo):
    b, s, d = x.shape
    tile = MIX_TILE
    consts = (nw, wa, wq, wg, cw, wsc, wao, wmo)
    tok = pl.BlockSpec((None, tile, d), lambda i, j: (i, j, 0))
    mem = pl.BlockSpec((None, MEM_LEN, d), lambda i, j: (i, 0, 0))
    return pl.pallas_call(
        _mix_kernel,
        grid=(b, s // tile),
        in_specs=[tok, tok, mem, mem] + [_whole(c.shape) for c in consts],
        out_specs=[tok, pl.BlockSpec((None, SC_WIDTH - 1, SC_DIM), lambda i, j: (i, 0, 0))],
        out_shape=[
            jax.ShapeDtypeStruct((b, s, d), F32),
            jax.ShapeDtypeStruct((b, SC_WIDTH - 1, SC_DIM), F32),
        ],
        scratch_shapes=[pltpu.VMEM((V7X_SUBLANES, SC_DIM), F32)],
        compiler_params=_params(2),
        name="mix_prompt",
    )(x, ybg, kb, vb, *consts)


def _ffn_body(x, nw, wg_ref, wu_ref, wd_ref, fw):
    xn = _rms(x, nw).astype(BF16)
    h = _silu(_mm(xn, wg_ref[...])) * _mm(xn, wu_ref[...])
    x2 = x + _mm(h.astype(BF16), wd_ref[...])
    return _rms(x2, fw)


def _ffn_kernel(x_ref, nw_ref, wg_ref, wu_ref, wd_ref, fw_ref, o_ref):
    o_ref[...] = _ffn_body(x_ref[...], nw_ref[...], wg_ref, wu_ref, wd_ref, fw_ref[...])


def _ffn_call(x, nw, wg, wu, wd, fw):
    b, s, d = x.shape
    tile = FFN_TILE
    consts = (nw, wg, wu, wd, fw)
    tok = pl.BlockSpec((None, tile, d), lambda i, j: (i, j, 0))
    return pl.pallas_call(
        _ffn_kernel,
        grid=(b, s // tile),
        in_specs=[tok] + [_whole(c.shape) for c in consts],
        out_specs=tok,
        out_shape=jax.ShapeDtypeStruct((b, s, d), F32),
        compiler_params=_params(2),
        name="ffn_prompt",
    )(x, *consts)


def _sfront_kernel(x_ref, sconv_ref, ssmc_ref, nw_ref, wa_ref, wz_ref, wxbc_ref, wdt_ref,
                   wq_ref, wg_ref, scw_ref, cw_ref, cb_ref, dtb_ref, alog_ref, wsc_ref, e_ref,
                   gaya_ref, gb_ref, gc_ref, z_ref, xs_ref, dtxt_ref, dect_ref, b_ref, c_ref,
                   q_ref, sconv_o, ssmc_o):
    xn = _rms(x_ref[...], nw_ref[...]).astype(BF16)

    sc_b = _mm(xn, wa_ref[:, 0:SC_DIM])
    u = _mm(xn, wa_ref[:, SC_DIM:2 * SC_DIM]) * _mm(xn, wa_ref[:, 2 * SC_DIM:3 * SC_DIM])
    h0 = sconv_ref[:, 0:SC_DIM]
    h1 = sconv_ref[:, SC_DIM:2 * SC_DIM]
    scw = scw_ref[...]
    conv = h0 * scw[0:1] + h1 * scw[1:2] + u * scw[2:3]
    sconv_o[:, 0:SC_DIM] = h1
    sconv_o[:, SC_DIM:2 * SC_DIM] = u
    y_a = _mm((sc_b * conv).astype(BF16), wsc_ref[...])
    gaya_ref[...] = _sigmoid(_mm(xn, wg_ref[:, 0:D_MODEL])) * y_a
    gb_ref[...] = _sigmoid(_mm(xn, wg_ref[:, D_MODEL:2 * D_MODEL]))
    gc_ref[...] = _sigmoid(_mm(xn, wg_ref[:, 2 * D_MODEL:3 * D_MODEL]))
    z_ref[...] = _mm(xn, wz_ref[...])
    q_ref[...] = _mm(xn, wq_ref[...])

    dt = _softplus(_mm(xn, wdt_ref[...]) + dtb_ref[...])
    dect_ref[...] = jnp.exp(dt * (-jnp.exp(alog_ref[...]))).T
    dtexp = _mm_expand(dt, e_ref[...])

    for j in range(SSM_CONV_DIM // CONV_COLS):
        lo = j * CONV_COLS
        cols = slice(lo, lo + CONV_COLS)
        xbc = _mm(xn, wxbc_ref[:, cols])
        p0 = ssmc_ref[0, :, cols]
        p1 = ssmc_ref[1, :, cols]
        p2 = ssmc_ref[2, :, cols]
        cw = cw_ref[:, cols]
        conv = p0 * cw[0:1] + p1 * cw[1:2] + p2 * cw[2:3] + xbc * cw[3:4]
        ssmc_o[0, :, cols] = p1
        ssmc_o[1, :, cols] = p2
        ssmc_o[2, :, cols] = xbc
        act = _silu(conv + cb_ref[:, cols])
        if lo < SSM_D_INNER:
            xs_ref[:, cols] = act
            dtx = act * dtexp[:, cols]
            for k in range(CONV_COLS // V7X_LANES):
                r0 = lo + k * V7X_LANES
                dtxt_ref[r0:r0 + V7X_LANES, :] = dtx[:, k * V7X_LANES:(k + 1) * V7X_LANES].T
        elif lo < SSM_D_INNER + SSM_BC:
            b_ref[:, lo - SSM_D_INNER:lo - SSM_D_INNER + CONV_COLS] = act
        else:
            off = lo - SSM_D_INNER - SSM_BC
            c_ref[:, off:off + CONV_COLS] = act


def _sfront_call(x, sconv, ssmc, nw, wa, wz, wxbc, wdt, wq, wg, scw, cw, cb, dtb, alog, wsc, e):
    n = x.shape[0]
    args = (x, sconv, ssmc, nw, wa, wz, wxbc, wdt, wq, wg, scw, cw, cb, dtb, alog, wsc, e)
    shapes = [
        (n, D_MODEL), (n, D_MODEL), (n, D_MODEL),
        (n, SSM_D_INNER), (n, SSM_D_INNER),
        (SSM_D_INNER, n), (V7X_LANES, n),
        (n, SSM_BC), (n, SSM_BC),
        (n, D_MODEL),
        sconv.shape, ssmc.shape,
    ]
    return pl.pallas_call(
        _sfront_kernel,
        grid=(1,),
        in_specs=[_whole(a.shape) for a in args],
        out_specs=[_whole(s) for s in shapes],
        out_shape=[jax.ShapeDtypeStruct(s, F32) for s in shapes],
        compiler_params=_params(1),
        name="sample_front",
    )(*args)


def _sssm_kernel(s_ref, dtxt_ref, dect_ref, b_ref, c_ref, so_ref, y_ref):
    i = pl.program_id(0)
    n = dtxt_ref.shape[1]
    lane = lax.broadcasted_iota(jnp.int32, (1, n), 1)
    sub = lax.broadcasted_iota(jnp.int32, (V7X_SUBLANES, SSM_STATE), 0)
    for k in range(SAMPLE_BLOCK):
        onehot = (lane == i * SAMPLE_BLOCK + k).astype(F32)
        xcol = jnp.sum(dtxt_ref[...] * onehot, axis=-1, keepdims=True)
        dcol = jnp.sum(dect_ref[0:SSM_HEADS, :] * onehot, axis=-1, keepdims=True)
        for h in range(SSM_HEADS):
            g = h // HEADS_PER_GROUP
            rows = slice(h * SSM_HEAD_DIM, (h + 1) * SSM_HEAD_DIM)
            brow = b_ref[k:k + 1, g * SSM_STATE:(g + 1) * SSM_STATE]
            so_ref[k, rows, :] = s_ref[k, rows, :] * dcol[h:h + 1, :] + xcol[rows, :] * brow
        c8 = jnp.zeros((V7X_SUBLANES, SSM_STATE), F32)
        for g in range(SSM_GROUPS):
            c8 = jnp.where(sub == g, c_ref[k:k + 1, g * SSM_STATE:(g + 1) * SSM_STATE], c8)
        y8 = _mm_nt(c8.astype(BF16), so_ref[k].astype(BF16))
        y_ref[k:k + 1, :] = jnp.concatenate(
            [y8[g:g + 1, g * GROUP_WIDTH:(g + 1) * GROUP_WIDTH] for g in range(SSM_GROUPS)], axis=1)


def _sssm_call(state, dtxt, dect, bm, cm):
    n = state.shape[0]
    blk = SAMPLE_BLOCK
    st = pl.BlockSpec((blk, SSM_D_INNER, SSM_STATE), lambda i: (i, 0, 0))
    return pl.pallas_call(
        _sssm_kernel,
        grid=(n // blk,),
        in_specs=[st, _whole(dtxt.shape), _whole(dect.shape),
                  pl.BlockSpec((blk, SSM_BC), lambda i: (i, 0)),
                  pl.BlockSpec((blk, SSM_BC), lambda i: (i, 0))],
        out_specs=[st, pl.BlockSpec((blk, SSM_D_INNER), lambda i: (i, 0))],
        out_shape=[jax.ShapeDtypeStruct(state.shape, F32),
                   jax.ShapeDtypeStruct((n, SSM_D_INNER), F32)],
        compiler_params=_params(1),
        name="sample_ssm",
    )(state, dtxt, dect, bm, cm)


def _cache_rows(c):
    n = c.shape[1]
    c = c.reshape(n, MEM_LEN, ATTN_HEADS, ATTN_HEAD_DIM // V7X_LANES, V7X_LANES)
    return c.transpose(0, 1, 3, 2, 4).reshape(n * MEM_LEN * CACHE_ROWS, V7X_LANES)


def _cache_head(ref, k, h):
    base = k * MEM_LEN * CACHE_ROWS
    halves = [ref[pl.ds(base + j * ATTN_HEADS + h, MEM_LEN, stride=CACHE_ROWS), :]
              for j in range(ATTN_HEAD_DIM // V7X_LANES)]
    return jnp.concatenate(halves, axis=1).astype(BF16)


def _sattn_kernel(q_ref, k_ref, v_ref, o_ref):
    sub = lax.broadcasted_iota(jnp.int32, (SAMPLE_BLOCK, MEM_LEN), 0)
    sub_o = lax.broadcasted_iota(jnp.int32, (SAMPLE_BLOCK, ATTN_HEAD_DIM), 0)
    outs = []
    for h in range(ATTN_HEADS):
        qh = q_ref[:, h * ATTN_HEAD_DIM:(h + 1) * ATTN_HEAD_DIM].astype(BF16)
        s = jnp.zeros((SAMPLE_BLOCK, MEM_LEN), F32)
        for k in range(SAMPLE_BLOCK):
            s = jnp.where(sub == k, _mm_nt(qh, _cache_head(k_ref, k, h)), s)
        p = _softmax_rows(s * (ATTN_HEAD_DIM ** -0.5)).astype(BF16)
        o = jnp.zeros((SAMPLE_BLOCK, ATTN_HEAD_DIM), F32)
        for k in range(SAMPLE_BLOCK):
            o = jnp.where(sub_o == k, _mm(p, _cache_head(v_ref, k, h)), o)
        outs.append(o)
    o_ref[...] = jnp.concatenate(outs, axis=1)


def _sattn_call(q, ck, cv):
    n = q.shape[0]
    blk = SAMPLE_BLOCK
    kv = pl.BlockSpec((blk * MEM_LEN * CACHE_ROWS, V7X_LANES), lambda i: (i, 0))
    row = pl.BlockSpec((blk, D_MODEL), lambda i: (i, 0))
    return pl.pallas_call(
        _sattn_kernel,
        grid=(n // blk,),
        in_specs=[row, kv, kv],
        out_specs=row,
        out_shape=jax.ShapeDtypeStruct((n, D_MODEL), F32),
        compiler_params=_params(1),
        name="sample_attn",
    )(q, _cache_rows(ck), _cache_rows(cv))


def _sback_kernel(x_ref, y_ref, xs_ref, z_ref, gaya_ref, gb_ref, gc_ref, att_ref, dful_ref,
                  gnw_ref, wout_ref, wao_ref, wmo_ref, fnw_ref, wg_ref, wu_ref, wd_ref, fw_ref,
                  o_ref):
    y = (y_ref[...] + dful_ref[...] * xs_ref[...]) * _silu(z_ref[...])
    y_b = _mm(_group_norm(y, gnw_ref[...]).astype(BF16), wout_ref[...])
    y_c = _mm(att_ref[...].astype(BF16), wao_ref[...])
    merged = gaya_ref[...] + gb_ref[...] * y_b + gc_ref[...] * y_c
    x1 = x_ref[...] + _mm(merged.astype(BF16), wmo_ref[...])
    o_ref[...] = _ffn_body(x1, fnw_ref[...], wg_ref, wu_ref, wd_ref, fw_ref[...])


def _sback_call(*args):
    n = args[0].shape[0]
    return pl.pallas_call(
        _sback_kernel,
        grid=(1,),
        in_specs=[_whole(a.shape) for a in args],
        out_specs=_whole((n, D_MODEL)),
        out_shape=jax.ShapeDtypeStruct((n, D_MODEL), F32),
        compiler_params=_params(1),
        name="sample_back",
    )(*args)


def kernel(x_prompt, x_sample, mem_prompt, cache_mem_k, cache_mem_v, state_conv, state_ssm_conv, state_ssm, norm_mix_w, w_in, sc_conv_w, w_sc_out, ssm_conv_w, ssm_conv_b, ssm_dt_bias, ssm_a_log, ssm_d, ssm_norm_w, w_ssm_out, norm_mem_w, w_mem_k, w_mem_v, w_attn_o, w_merge_o, norm_ffn_w, w_ffn_gate, w_ffn_up, w_ffn_down, norm_final_w):
    depth = w_in.shape[0]
    assert depth == 1
    bp = x_prompt.shape[0]
    ns = x_sample.shape[0]

    wi = w_in[0]
    o = 0
    wa = wi[:, o:o + 3 * SC_DIM].astype(BF16); o += 3 * SC_DIM
    wz = wi[:, o:o + SSM_D_INNER].astype(BF16); o += SSM_D_INNER
    wxbc = wi[:, o:o + SSM_CONV_DIM].astype(BF16); o += SSM_CONV_DIM
    wdt = jnp.pad(wi[:, o:o + SSM_HEADS], ((0, 0), (0, V7X_LANES - SSM_HEADS))).astype(BF16); o += SSM_HEADS
    wq = wi[:, o:o + D_MODEL].astype(BF16); o += D_MODEL
    wg = wi[:, o:o + 3 * D_MODEL].astype(BF16)
    wg_ac = jnp.concatenate([wg[:, 0:D_MODEL], wg[:, 2 * D_MODEL:]], axis=1)
    wg_b = wg[:, D_MODEL:2 * D_MODEL]
    row = lambda v: v.reshape(1, -1).astype(F32)
    pad_heads = lambda v: jnp.pad(row(v), ((0, 0), (0, V7X_LANES - SSM_HEADS)))
    nmix = row(norm_mix_w[0])
    dtb = pad_heads(ssm_dt_bias[0])
    alog = pad_heads(ssm_a_log[0])
    dful = row(jnp.repeat(ssm_d[0], SSM_HEAD_DIM))
    gnw = row(ssm_norm_w[0])
    cb = row(ssm_conv_b[0])
    cw = ssm_conv_w[0]
    scw = sc_conv_w[0]
    wsc = w_sc_out[0].astype(BF16)
    wout = w_ssm_out[0].astype(BF16)
    wao = w_attn_o[0].astype(BF16)
    wmo = w_merge_o[0].astype(BF16)
    wfg = w_ffn_gate[0].astype(BF16)
    wfu = w_ffn_up[0].astype(BF16)
    wfd = w_ffn_down[0].astype(BF16)
    nffn = row(norm_ffn_w[0])
    nfin = row(norm_final_w)
    expand = (jnp.arange(V7X_LANES)[:, None] == (jnp.arange(SSM_D_INNER)[None, :] // SSM_HEAD_DIM)).astype(BF16)

    mk, mv, mkb, mvb = _mem_call(mem_prompt, row(norm_mem_w[0]), w_mem_k[0].astype(BF16), w_mem_v[0].astype(BF16))
    ybg, p_ssmc, p_ssm = _ssd_call(x_prompt, nmix, wz, wxbc, wdt, wg_b, cw, cb, dtb, alog, dful, gnw, wout, expand)
    x1, p_conv = _mix_call(x_prompt, ybg, mkb, mvb, nmix, wa, wq, wg_ac, scw, wsc, wao, wmo)
    y_prompt = _ffn_call(x1, nffn, wfg, wfu, wfd, nfin)

    xs2 = x_sample.reshape(ns, D_MODEL)
    (gaya, gb, gc, z, xs, dtxt, dect, bm, cm, q, s_conv, s_ssmc) = _sfront_call(
        xs2, state_conv[0].reshape(ns, -1), jnp.swapaxes(state_ssm_conv[0], 0, 1),
        nmix, wa, wz, wxbc, wdt, wq, wg, scw, cw, cb, dtb, alog, wsc, expand)
    s_ssm, y_s = _sssm_call(state_ssm[0].reshape(ns, SSM_D_INNER, SSM_STATE), dtxt, dect, bm, cm)
    att = _sattn_call(q, cache_mem_k, cache_mem_v)
    y_sample = _sback_call(xs2, y_s, xs, z, gaya, gb, gc, att, dful, gnw, wout, wao, wmo,
                           nffn, wfg, wfu, wfd, nfin)

    def from_rows(r):
        r = r.reshape(bp, MEM_LEN, ATTN_HEAD_DIM // V7X_LANES, ATTN_HEADS, V7X_LANES)
        return r.transpose(0, 1, 3, 2, 4).reshape(depth, bp, MEM_LEN, ATTN_HEADS, ATTN_HEAD_DIM)

    state_shape = (SSM_HEADS, SSM_HEAD_DIM, SSM_STATE)
    return (
        y_prompt,
        y_sample.reshape(ns, 1, D_MODEL),
        from_rows(mk),
        from_rows(mv),
        p_conv.reshape(depth, bp, SC_WIDTH - 1, SC_DIM),
        p_ssmc.reshape(depth, bp, SSM_CONV - 1, SSM_CONV_DIM),
        p_ssm.reshape((depth, bp) + state_shape),
        s_conv.reshape(depth, ns, SC_WIDTH - 1, SC_DIM),
        jnp.swapaxes(s_ssmc, 0, 1).reshape(depth, ns, SSM_CONV - 1, SSM_CONV_DIM),
        s_ssm.reshape((depth, ns) + state_shape),
    )
```

```python
import itertools

import jax
import jax.numpy as jnp
from jax import lax
from jax.experimental import pallas as pl
from jax.experimental.pallas import tpu as pltpu

F32 = jnp.float32
BF16 = jnp.bfloat16

D_MODEL = 1024
RMS_EPS = 1e-6
LOG2_E = 1.4426950408889634
SC_DIM = D_MODEL
SC_WIDTH = 3
SSM_D_INNER = 2 * D_MODEL
SSM_HEAD_DIM = 64
SSM_HEADS = SSM_D_INNER // SSM_HEAD_DIM
SSM_STATE = 128
SSM_GROUPS = 4
SSM_CONV = 4
SSM_CHUNK = 128
SSM_BC = SSM_GROUPS * SSM_STATE
SSM_CONV_DIM = SSM_D_INNER + 2 * SSM_BC
HEADS_PER_GROUP = SSM_HEADS // SSM_GROUPS
GROUP_WIDTH = SSM_D_INNER // SSM_GROUPS
MEM_LEN = 256
ATTN_HEADS = 4
ATTN_HEAD_DIM = D_MODEL // ATTN_HEADS
FFN_HIDDEN = ((8 * D_MODEL // 3 + 255) // 256) * 256

V7X_LANES = 128
V7X_SUBLANES = 8
V7X_VMEM_BYTES = 64 * 1024 * 1024
VMEM_LIMIT_BYTES = V7X_VMEM_BYTES - 8 * 1024 * 1024
CACHE_ROWS = ATTN_HEADS * ATTN_HEAD_DIM // V7X_LANES

SSD_TILE = 512
SSD_SUB = 256
MIX_TILE = 512
FFN_TILE = 512
SAMPLE_BLOCK = 8
CONV_COLS = 512


def _params(n_grid, flags=None):
    return pltpu.CompilerParams(
        dimension_semantics=("arbitrary",) * n_grid,
        vmem_limit_bytes=VMEM_LIMIT_BYTES,
        flags=flags,
    )


def _whole(shape):
    nd = len(shape)
    return pl.BlockSpec(shape, lambda *_: (0,) * nd)


def _mm(a, b):
    return jnp.dot(a, b, preferred_element_type=F32)


def _mm_nt(a, b):
    return lax.dot_general(a, b, (((1,), (1,)), ((), ())), preferred_element_type=F32)


def _split3(x):
    hi = x.astype(BF16)
    r = x - hi.astype(F32)
    mid = r.astype(BF16)
    lo = (r - mid.astype(F32)).astype(BF16)
    return hi, mid, lo


def _mm_sel(sel, x):
    hi, mid, lo = _split3(x)
    return _mm(sel, hi) + _mm(sel, mid) + _mm(sel, lo)


def _mm_expand(x, sel):
    hi, mid, lo = _split3(x)
    return _mm(hi, sel) + _mm(mid, sel) + _mm(lo, sel)


def _rms(x, w):
    return x * lax.rsqrt(jnp.mean(x * x, axis=-1, keepdims=True) + RMS_EPS) * w


def _sigmoid(x):
    return 1.0 / (1.0 + jnp.exp2(x * (-LOG2_E)))


def _silu(x):
    return x * _sigmoid(x)


def _softplus(x):
    return jnp.maximum(x, 0.0) + jnp.log1p(jnp.exp(-jnp.abs(x)))


def _shift_rows(u, prev8, k):
    r = pltpu.roll(u, k, axis=0)
    p = pltpu.roll(prev8, k, axis=0)
    row = lax.broadcasted_iota(jnp.int32, prev8.shape, 0)
    head = jnp.where(row < k, p, r[:V7X_SUBLANES])
    return jnp.concatenate([head, r[V7X_SUBLANES:]], axis=0)


def _softmax_rows(s):
    m = jnp.max(s, axis=-1, keepdims=True)
    p = jnp.exp(s - m)
    return p / jnp.sum(p, axis=-1, keepdims=True)


def _group_norm(y, w):
    outs = []
    for g in range(SSM_GROUPS):
        cols = slice(g * GROUP_WIDTH, (g + 1) * GROUP_WIDTH)
        yg = y[:, cols]
        ms = jnp.mean(yg * yg, axis=-1, keepdims=True)
        outs.append(yg * lax.rsqrt(ms + RMS_EPS) * w[:, cols])
    return jnp.concatenate(outs, axis=1)


def _mem_kernel(m_ref, nw_ref, wk_ref, wv_ref, k_ref, v_ref, kb_ref, vb_ref):
    mn = _rms(m_ref[...], nw_ref[...]).astype(BF16)
    k = _mm(mn, wk_ref[...])
    v = _mm(mn, wv_ref[...])
    kb_ref[...] = k.astype(BF16)
    vb_ref[...] = v.astype(BF16)
    for j in range(ATTN_HEAD_DIM // V7X_LANES):
        for h in range(ATTN_HEADS):
            rows = pl.ds(j * ATTN_HEADS + h, MEM_LEN, stride=CACHE_ROWS)
            lo = h * ATTN_HEAD_DIM + j * V7X_LANES
            k_ref[rows, :] = k[:, lo:lo + V7X_LANES]
            v_ref[rows, :] = v[:, lo:lo + V7X_LANES]


def _mem_call(mem, nw, wk, wv):
    b = mem.shape[0]
    blk = pl.BlockSpec((None, MEM_LEN, D_MODEL), lambda i: (i, 0, 0))
    rows = pl.BlockSpec((None, MEM_LEN * CACHE_ROWS, V7X_LANES), lambda i: (i, 0, 0))
    rows_shape = jax.ShapeDtypeStruct((b, MEM_LEN * CACHE_ROWS, V7X_LANES), F32)
    return pl.pallas_call(
        _mem_kernel,
        grid=(b,),
        in_specs=[blk, _whole(nw.shape), _whole(wk.shape), _whole(wv.shape)],
        out_specs=[rows, rows, blk, blk],
        out_shape=[
            rows_shape,
            rows_shape,
            jax.ShapeDtypeStruct(mem.shape, BF16),
            jax.ShapeDtypeStruct(mem.shape, BF16),
        ],
        compiler_params=_params(1),
        name="mem_kv",
    )(mem, nw, wk, wv)


def _ssd_kernel(x_ref, nw_ref, wz_ref, wxbc_ref, wdt_ref, wgb_ref, cw_ref, cb_ref,
                dtb_ref, alog_ref, dful_ref, gnw_ref, wout_ref, e_ref,
                ybg_ref, cst_ref, sst_ref,
                *scratch):
    t = pl.program_id(1)
    tile = x_ref.shape[0]
    q = SSM_CHUNK
    n_sub = tile // SSD_SUB
    n_blocks = SSM_CONV_DIM // CONV_COLS
    hist_refs = scratch[0:n_blocks]
    st_refs = scratch[n_blocks:n_blocks + SSM_GROUPS]
    sets = [scratch[n_blocks + SSM_GROUPS + k * 8:n_blocks + SSM_GROUPS + (k + 1) * 8]
            for k in range(n_sub)]

    @pl.when(t == 0)
    def _():
        for ref in hist_refs + st_refs:
            ref[...] = jnp.zeros_like(ref)

    def project(sub):
        xn_s, z_s, xs_s, b_s, c_s, dt_s, da_s, _ = sets[sub]
        xn_s[...] = _rms(x_ref[sub * SSD_SUB:(sub + 1) * SSD_SUB, :], nw_ref[...]).astype(BF16)
        dt = _softplus(_mm(xn_s[...], wdt_ref[...]) + dtb_ref[...])
        dt_s[...] = dt
        da_s[...] = dt * (-jnp.exp(alog_ref[...]))
        first_bc = SSM_D_INNER // CONV_COLS
        for j in list(range(first_bc, n_blocks)) + list(range(first_bc)):
            cols = slice(j * CONV_COLS, (j + 1) * CONV_COLS)
            u = _mm(xn_s[...], wxbc_ref[:, cols])
            prev8 = hist_refs[j][...]
            hist_refs[j][...] = u[SSD_SUB - V7X_SUBLANES:]
            cw = cw_ref[:, cols]
            conv = _shift_rows(u, prev8, 3) * cw[0:1]
            conv = conv + _shift_rows(u, prev8, 2) * cw[1:2]
            conv = conv + _shift_rows(u, prev8, 1) * cw[2:3]
            conv = conv + u * cw[3:4]
            act = _silu(conv + cb_ref[:, cols])
            lo = j * CONV_COLS
            if lo < SSM_D_INNER:
                xs_s[:, cols] = act
            elif lo < SSM_D_INNER + SSM_BC:
                b_s[:, lo - SSM_D_INNER:lo - SSM_D_INNER + CONV_COLS] = act
            else:
                off = lo - SSM_D_INNER - SSM_BC
                c_s[:, off:off + CONV_COLS] = act
            yield
        z_s[...] = _silu(_mm(xn_s[...], wz_ref[...]))
        yield

    ri = lax.broadcasted_iota(jnp.int32, (q, q), 0)
    ci = lax.broadcasted_iota(jnp.int32, (q, q), 1)
    causal = ri >= ci
    tri = jnp.where(causal, 1.0, 0.0).astype(BF16)
    lane_lo = ci < SSM_HEAD_DIM
    keep_lo = jnp.where(lane_lo, 1.0, 0.0).astype(BF16)
    keep_hi = jnp.where(lane_lo, 0.0, 1.0).astype(BF16)
    sub8 =lax.broadcasted_iota(jnp.int32, (V7X_SUBLANES, q), 0)

    groups = range(SSM_GROUPS)
    gsl = [slice(g * GROUP_WIDTH, (g + 1) * GROUP_WIDTH) for g in groups]
    nsl = [slice(g * SSM_STATE, (g + 1) * SSM_STATE) for g in groups]

    def scan(sub, chunks):
        _, z_s, xs_s, b_s, c_s, dt_s, da_s, yn_s = sets[sub]
        rows = {c: pl.ds(c * q, q) for c in chunks}

        acum, row_t, w, cd = {}, {}, {}, {}
        for c in chunks:
            a = _mm_sel(tri, da_s[rows[c], :])
            last = a[q - 1:q, :]
            dtc = dt_s[rows[c], :]
            acum[c] = a
            row_t[c] = a.T - jnp.log(dtc.T)
            w[c] = (dtc * jnp.exp(last - a)).astype(BF16)
            cd[c] = jnp.where(sub8 == 0, jnp.exp(last), 0.0)
        yield

        bg = {(c, g): b_s[rows[c], nsl[g]] for c in chunks for g in groups}
        cg = {(c, g): c_s[rows[c], nsl[g]].astype(BF16) for c in chunks for g in groups}
        cb = {k: _mm_nt(cg[k], bg[k].astype(BF16)).astype(BF16) for k in bg}
        wx = {(c, g): _mm(w[c], e_ref[:, gsl[g]]) for c in chunks for g in groups}
        dec = {(c, g): _mm_expand(cd[c], e_ref[:, gsl[g]])[0:1, :] for c in chunks for g in groups}
        yield

        yo = {}
        for c in chunks:
            st = [st_refs[g][...] for g in groups]
            for g in groups:
                yo[c, g] = _mm(cg[c, g], st[g].astype(BF16))
            for g in groups:
                xw = (xs_s[rows[c], gsl[g]] * wx[c, g]).astype(BF16)
                st_refs[g][...] = st[g] * dec[c, g] + _mm(bg[c, g].T.astype(BF16), xw)
        yield

        yd = {}
        for c in chunks:
            for g in groups:
                for pq in range(HEADS_PER_GROUP // 2):
                    h0 = g * HEADS_PER_GROUP + 2 * pq
                    h1 = h0 + 1
                    l0 = jnp.where(causal, jnp.exp(acum[c][:, h0:h0 + 1] - row_t[c][h0:h0 + 1, :]), 0.0)
                    l1 = jnp.where(causal, jnp.exp(acum[c][:, h1:h1 + 1] - row_t[c][h1:h1 + 1, :]), 0.0)
                    lhs = jnp.concatenate([cb[c, g] * l0.astype(BF16), cb[c, g] * l1.astype(BF16)], axis=1)
                    xpb = xs_s[rows[c], h0 * SSM_HEAD_DIM:(h1 + 1) * SSM_HEAD_DIM].astype(BF16)
                    rhs = jnp.concatenate([xpb * keep_lo, xpb * keep_hi], axis=0)
                    yd[c, g, pq] = _mm(lhs, rhs)
        yield

        for c in chunks:
            for g in groups:
                pairs = []
                for pq in range(HEADS_PER_GROUP // 2):
                    h0 = g * HEADS_PER_GROUP + 2 * pq
                    h1 = h0 + 1
                    pcols = slice(h0 * SSM_HEAD_DIM, (h1 + 1) * SSM_HEAD_DIM)
                    sc = jnp.where(lane_lo, jnp.exp(acum[c][:, h0:h0 + 1]), jnp.exp(acum[c][:, h1:h1 + 1]))
                    lc = slice(2 * pq * SSM_HEAD_DIM, (2 * pq + 2) * SSM_HEAD_DIM)
                    pairs.append(yd[c, g, pq] + sc * yo[c, g][:, lc] + dful_ref[:, pcols] * xs_s[rows[c], pcols])
                yg = jnp.concatenate(pairs, axis=1) * z_s[rows[c], gsl[g]]
                ms = jnp.mean(yg * yg, axis=-1, keepdims=True)
                yn_s[rows[c], gsl[g]] = (yg * lax.rsqrt(ms + RMS_EPS) * gnw_ref[:, gsl[g]]).astype(BF16)
        yield

    def out(sub):
        xn_s, yn_s = sets[sub][0], sets[sub][7]
        gb = _sigmoid(_mm(xn_s[...], wgb_ref[...]))
        ybg_ref[sub * SSD_SUB:(sub + 1) * SSD_SUB, :] = gb * _mm(yn_s[...], wout_ref[...])
        yield

    def run(gen):
        for _ in gen:
            pass

    def interleave(a, b):
        live = [a, b]
        while live:
            for gen in list(live):
                try:
                    next(gen)
                except StopIteration:
                    live.remove(gen)

    def work(sub):
        return itertools.chain(*[scan(sub, [c]) for c in range(SSD_SUB // q)], out(sub))

    run(project(0))
    for sub in range(n_sub):
        if sub + 1 < n_sub:
            interleave(work(sub), project(sub + 1))
        else:
            run(work(sub))

    @pl.when(t == pl.num_programs(1) - 1)
    def _():
        for j in range(n_blocks):
            cols = slice(j * CONV_COLS, (j + 1) * CONV_COLS)
            cst_ref[:, cols] = pltpu.roll(hist_refs[j][...], SSM_CONV - 1, axis=0)[0:SSM_CONV - 1]
        for g in range(SSM_GROUPS):
            for k in range(GROUP_WIDTH // V7X_LANES):
                blk = slice(k * V7X_LANES, (k + 1) * V7X_LANES)
                out_rows = slice(g * GROUP_WIDTH + k * V7X_LANES, g * GROUP_WIDTH + (k + 1) * V7X_LANES)
                sst_ref[out_rows, :] = st_refs[g][:, blk].T


def _ssd_call(x, nw, wz, wxbc, wdt, wgb, cw, cb, dtb, alog, dful, gnw, wout, e):
    b, s, d = x.shape
    tile = SSD_TILE
    consts = (nw, wz, wxbc, wdt, wgb, cw, cb, dtb, alog, dful, gnw, wout, e)
    return pl.pallas_call(
        _ssd_kernel,
        grid=(b, s // tile),
        in_specs=[pl.BlockSpec((None, tile, d), lambda i, j: (i, j, 0))]
        + [_whole(c.shape) for c in consts],
        out_specs=[
            pl.BlockSpec((None, tile, d), lambda i, j: (i, j, 0)),
            pl.BlockSpec((None, SSM_CONV - 1, SSM_CONV_DIM), lambda i, j: (i, 0, 0)),
            pl.BlockSpec((None, SSM_D_INNER, SSM_STATE), lambda i, j: (i, 0, 0)),
        ],
        out_shape=[
            jax.ShapeDtypeStruct((b, s, d), F32),
            jax.ShapeDtypeStruct((b, SSM_CONV - 1, SSM_CONV_DIM), F32),
            jax.ShapeDtypeStruct((b, SSM_D_INNER, SSM_STATE), F32),
        ],
        scratch_shapes=[
            pltpu.VMEM((V7X_SUBLANES, CONV_COLS), F32)
        ] * (SSM_CONV_DIM // CONV_COLS) + [
            pltpu.VMEM((SSM_STATE, GROUP_WIDTH), F32)
        ] * SSM_GROUPS + [
            pltpu.VMEM((SSD_SUB, d), BF16),
            pltpu.VMEM((SSD_SUB, SSM_D_INNER), F32),
            pltpu.VMEM((SSD_SUB, SSM_D_INNER), F32),
            pltpu.VMEM((SSD_SUB, SSM_BC), F32),
            pltpu.VMEM((SSD_SUB, SSM_BC), F32),
            pltpu.VMEM((SSD_SUB, V7X_LANES), F32),
            pltpu.VMEM((SSD_SUB, V7X_LANES), F32),
            pltpu.VMEM((SSD_SUB, SSM_D_INNER), BF16),
        ] * (tile // SSD_SUB),
        compiler_params=_params(2),
        name="ssd_prompt",
    )(x, *consts)


def _attention(q, k_ref, v_ref):
    outs = []
    for h in range(ATTN_HEADS):
        cols = slice(h * ATTN_HEAD_DIM, (h + 1) * ATTN_HEAD_DIM)
        s = _mm_nt(q[:, cols].astype(BF16), k_ref[:, cols]) * (ATTN_HEAD_DIM ** -0.5)
        outs.append(_mm(_softmax_rows(s).astype(BF16), v_ref[:, cols]))
    return jnp.concatenate(outs, axis=1)


def _mix_kernel(x_ref, ybg_ref, kb_ref, vb_ref, nw_ref, wa_ref, wq_ref, wg_ref, cw_ref,
                wsc_ref, wao_ref, wmo_ref, x1_ref, cst_ref, hist_ref):
    t = pl.program_id(1)
    tile = x_ref.shape[0]

    @pl.when(t == 0)
    def _():
        hist_ref[...] = jnp.zeros_like(hist_ref)

    x = x_ref[...]
    xn = _rms(x, nw_ref[...]).astype(BF16)

    sc_b = _mm(xn, wa_ref[:, 0:SC_DIM])
    u = _mm(xn, wa_ref[:, SC_DIM:2 * SC_DIM]) * _mm(xn, wa_ref[:, 2 * SC_DIM:3 * SC_DIM])
    prev8 = hist_ref[...]
    hist_ref[...] = u[tile - V7X_SUBLANES:]
    cw = cw_ref[...]
    conv = _shift_rows(u, prev8, 2) * cw[0:1]
    conv = conv + _shift_rows(u, prev8, 1) * cw[1:2]
    conv = conv + u * cw[2:3]
    y_a = _mm((sc_b * conv).astype(BF16), wsc_ref[...])

    att = _attention(_mm(xn, wq_ref[...]), kb_ref, vb_ref)
    y_c = _mm(att.astype(BF16), wao_ref[...])

    g_a = _sigmoid(_mm(xn, wg_ref[:, 0:D_MODEL]))
    g_c = _sigmoid(_mm(xn, wg_ref[:, D_MODEL:2 * D_MODEL]))
    merged = g_a * y_a + ybg_ref[...] + g_c * y_c
    x1_ref[...] = x + _mm(merged.astype(BF16), wmo_ref[...])

    @pl.when(t == pl.num_programs(1) - 1)
    def _():
        cst_ref[...] = pltpu.roll(hist_ref[...], SC_WIDTH - 1, axis=0)[0:SC_WIDTH - 1]


def _mix_call(x, ybg, kb, vb, nw, wa, wq, wg, cw, wsc, wao, wmo):
    b, s, d = x.shape
    tile = MIX_TILE
    consts = (nw, wa, wq, wg, cw, wsc, wao, wmo)
    tok = pl.BlockSpec((None, tile, d), lambda i, j: (i, j, 0))
    mem = pl.BlockSpec((None, MEM_LEN, d), lambda i, j: (i, 0, 0))
    return pl.pallas_call(
        _mix_kernel,
        grid=(b, s // tile),
        in_specs=[tok, tok, mem, mem] + [_whole(c.shape) for c in consts],
        out_specs=[tok, pl.BlockSpec((None, SC_WIDTH - 1, SC_DIM), lambda i, j: (i, 0, 0))],
        out_shape=[
            jax.ShapeDtypeStruct((b, s, d), F32),
            jax.ShapeDtypeStruct((b, SC_WIDTH - 1, SC_DIM), F32),
        ],
        scratch_shapes=[pltpu.VMEM((V7X_SUBLANES, SC_DIM), F32)],
        compiler_params=_params(2),
        name="mix_prompt",
    )(x, ybg, kb, vb, *consts)


def _ffn_body(x, nw, wg_ref, wu_ref, wd_ref, fw):
    xn = _rms(x, nw).astype(BF16)
    h = _silu(_mm(xn, wg_ref[...])) * _mm(xn, wu_ref[...])
    x2 = x + _mm(h.astype(BF16), wd_ref[...])
    return _rms(x2, fw)


def _ffn_kernel(x_ref, nw_ref, wg_ref, wu_ref, wd_ref, fw_ref, o_ref):
    o_ref[...] = _ffn_body(x_ref[...], nw_ref[...], wg_ref, wu_ref, wd_ref, fw_ref[...])


def _ffn_call(x, nw, wg, wu, wd, fw):
    b, s, d = x.shape
    tile = FFN_TILE
    consts = (nw, wg, wu, wd, fw)
    tok = pl.BlockSpec((None, tile, d), lambda i, j: (i, j, 0))
    return pl.pallas_call(
        _ffn_kernel,
        grid=(b, s // tile),
        in_specs=[tok] + [_whole(c.shape) for c in consts],
        out_specs=tok,
        out_shape=jax.ShapeDtypeStruct((b, s, d), F32),
        compiler_params=_params(2),
        name="ffn_prompt",
    )(x, *consts)


def _sfront_kernel(x_ref, sconv_ref, ssmc_ref, nw_ref, wa_ref, wz_ref, wxbc_ref, wdt_ref,
                   wq_ref, wg_ref, scw_ref, cw_ref, cb_ref, dtb_ref, alog_ref, wsc_ref, e_ref,
                   gaya_ref, gb_ref, gc_ref, z_ref, xs_ref, dtxt_ref, dect_ref, b_ref, c_ref,
                   q_ref, sconv_o, ssmc_o):
    xn = _rms(x_ref[...], nw_ref[...]).astype(BF16)

    sc_b = _mm(xn, wa_ref[:, 0:SC_DIM])
    u = _mm(xn, wa_ref[:, SC_DIM:2 * SC_DIM]) * _mm(xn, wa_ref[:, 2 * SC_DIM:3 * SC_DIM])
    h0 = sconv_ref[:, 0:SC_DIM]
    h1 = sconv_ref[:, SC_DIM:2 * SC_DIM]
    scw = scw_ref[...]
    conv = h0 * scw[0:1] + h1 * scw[1:2] + u * scw[2:3]
    sconv_o[:, 0:SC_DIM] = h1
    sconv_o[:, SC_DIM:2 * SC_DIM] = u
    y_a = _mm((sc_b * conv).astype(BF16), wsc_ref[...])
    gaya_ref[...] = _sigmoid(_mm(xn, wg_ref[:, 0:D_MODEL])) * y_a
    gb_ref[...] = _sigmoid(_mm(xn, wg_ref[:, D_MODEL:2 * D_MODEL]))
    gc_ref[...] = _sigmoid(_mm(xn, wg_ref[:, 2 * D_MODEL:3 * D_MODEL]))
    z_ref[...] = _mm(xn, wz_ref[...])
    q_ref[...] = _mm(xn, wq_ref[...])

    dt = _softplus(_mm(xn, wdt_ref[...]) + dtb_ref[...])
    dect_ref[...] = jnp.exp(dt * (-jnp.exp(alog_ref[...]))).T
    dtexp = _mm_expand(dt, e_ref[...])

    for j in range(SSM_CONV_DIM // CONV_COLS):
        lo = j * CONV_COLS
        cols = slice(lo, lo + CONV_COLS)
        xbc = _mm(xn, wxbc_ref[:, cols])
        p0 = ssmc_ref[0, :, cols]
        p1 = ssmc_ref[1, :, cols]
        p2 = ssmc_ref[2, :, cols]
        cw = cw_ref[:, cols]
        conv = p0 * cw[0:1] + p1 * cw[1:2] + p2 * cw[2:3] + xbc * cw[3:4]
        ssmc_o[0, :, cols] = p1
        ssmc_o[1, :, cols] = p2
        ssmc_o[2, :, cols] = xbc
        act = _silu(conv + cb_ref[:, cols])
        if lo < SSM_D_INNER:
            xs_ref[:, cols] = act
            dtx = act * dtexp[:, cols]
            for k in range(CONV_COLS // V7X_LANES):
                r0 = lo + k * V7X_LANES
                dtxt_ref[r0:r0 + V7X_LANES, :] = dtx[:, k * V7X_LANES:(k + 1) * V7X_LANES].T
        elif lo < SSM_D_INNER + SSM_BC:
            b_ref[:, lo - SSM_D_INNER:lo - SSM_D_INNER + CONV_COLS] = act
        else:
            off = lo - SSM_D_INNER - SSM_BC
            c_ref[:, off:off + CONV_COLS] = act


def _sfront_call(x, sconv, ssmc, nw, wa, wz, wxbc, wdt, wq, wg, scw, cw, cb, dtb, alog, wsc, e):
    n = x.shape[0]
    args = (x, sconv, ssmc, nw, wa, wz, wxbc, wdt, wq, wg, scw, cw, cb, dtb, alog, wsc, e)
    shapes = [
        (n, D_MODEL), (n, D_MODEL), (n, D_MODEL),
        (n, SSM_D_INNER), (n, SSM_D_INNER),
        (SSM_D_INNER, n), (V7X_LANES, n),
        (n, SSM_BC), (n, SSM_BC),
        (n, D_MODEL),
        sconv.shape, ssmc.shape,
    ]
    return pl.pallas_call(
        _sfront_kernel,
        grid=(1,),
        in_specs=[_whole(a.shape) for a in args],
        out_specs=[_whole(s) for s in shapes],
        out_shape=[jax.ShapeDtypeStruct(s, F32) for s in shapes],
        compiler_params=_params(1),
        name="sample_front",
    )(*args)


def _sssm_kernel(s_ref, dtxt_ref, dect_ref, b_ref, c_ref, so_ref, y_ref):
    i = pl.program_id(0)
    n = dtxt_ref.shape[1]
    lane = lax.broadcasted_iota(jnp.int32, (1, n), 1)
    sub = lax.broadcasted_iota(jnp.int32, (V7X_SUBLANES, SSM_STATE), 0)
    for k in range(SAMPLE_BLOCK):
        onehot = (lane == i * SAMPLE_BLOCK + k).astype(F32)
        xcol = jnp.sum(dtxt_ref[...] * onehot, axis=-1, keepdims=True)
        dcol = jnp.sum(dect_ref[0:SSM_HEADS, :] * onehot, axis=-1, keepdims=True)
        for h in range(SSM_HEADS):
            g = h // HEADS_PER_GROUP
            rows = slice(h * SSM_HEAD_DIM, (h + 1) * SSM_HEAD_DIM)
            brow = b_ref[k:k + 1, g * SSM_STATE:(g + 1) * SSM_STATE]
            so_ref[k, rows, :] = s_ref[k, rows, :] * dcol[h:h + 1, :] + xcol[rows, :] * brow
        c8 = jnp.zeros((V7X_SUBLANES, SSM_STATE), F32)
        for g in range(SSM_GROUPS):
            c8 = jnp.where(sub == g, c_ref[k:k + 1, g * SSM_STATE:(g + 1) * SSM_STATE], c8)
        y8 = _mm_nt(c8.astype(BF16), so_ref[k].astype(BF16))
        y_ref[k:k + 1, :] = jnp.concatenate(
            [y8[g:g + 1, g * GROUP_WIDTH:(g + 1) * GROUP_WIDTH] for g in range(SSM_GROUPS)], axis=1)


def _sssm_call(state, dtxt, dect, bm, cm):
    n = state.shape[0]
    blk = SAMPLE_BLOCK
    st = pl.BlockSpec((blk, SSM_D_INNER, SSM_STATE), lambda i: (i, 0, 0))
    return pl.pallas_call(
        _sssm_kernel,
        grid=(n // blk,),
        in_specs=[st, _whole(dtxt.shape), _whole(dect.shape),
                  pl.BlockSpec((blk, SSM_BC), lambda i: (i, 0)),
                  pl.BlockSpec((blk, SSM_BC), lambda i: (i, 0))],
        out_specs=[st, pl.BlockSpec((blk, SSM_D_INNER), lambda i: (i, 0))],
        out_shape=[jax.ShapeDtypeStruct(state.shape, F32),
                   jax.ShapeDtypeStruct((n, SSM_D_INNER), F32)],
        compiler_params=_params(1),
        name="sample_ssm",
    )(state, dtxt, dect, bm, cm)


def _cache_rows(c):
    n = c.shape[1]
    c = c.reshape(n, MEM_LEN, ATTN_HEADS, ATTN_HEAD_DIM // V7X_LANES, V7X_LANES)
    return c.transpose(0, 1, 3, 2, 4).reshape(n * MEM_LEN * CACHE_ROWS, V7X_LANES)


def _cache_head(ref, k, h):
    base = k * MEM_LEN * CACHE_ROWS
    halves = [ref[pl.ds(base + j * ATTN_HEADS + h, MEM_LEN, stride=CACHE_ROWS), :]
              for j in range(ATTN_HEAD_DIM // V7X_LANES)]
    return jnp.concatenate(halves, axis=1).astype(BF16)


def _sattn_kernel(q_ref, k_ref, v_ref, o_ref):
    sub = lax.broadcasted_iota(jnp.int32, (SAMPLE_BLOCK, MEM_LEN), 0)
    sub_o = lax.broadcasted_iota(jnp.int32, (SAMPLE_BLOCK, ATTN_HEAD_DIM), 0)
    outs = []
    for h in range(ATTN_HEADS):
        qh = q_ref[:, h * ATTN_HEAD_DIM:(h + 1) * ATTN_HEAD_DIM].astype(BF16)
        s = jnp.zeros((SAMPLE_BLOCK, MEM_LEN), F32)
        for k in range(SAMPLE_BLOCK):
            s = jnp.where(sub == k, _mm_nt(qh, _cache_head(k_ref, k, h)), s)
        p = _softmax_rows(s * (ATTN_HEAD_DIM ** -0.5)).astype(BF16)
        o = jnp.zeros((SAMPLE_BLOCK, ATTN_HEAD_DIM), F32)
        for k in range(SAMPLE_BLOCK):
            o = jnp.where(sub_o == k, _mm(p, _cache_head(v_ref, k, h)), o)
        outs.append(o)
    o_ref[...] = jnp.concatenate(outs, axis=1)


def _sattn_call(q, ck, cv):
    n = q.shape[0]
    blk = SAMPLE_BLOCK
    kv = pl.BlockSpec((blk * MEM_LEN * CACHE_ROWS, V7X_LANES), lambda i: (i, 0))
    row = pl.BlockSpec((blk, D_MODEL), lambda i: (i, 0))
    return pl.pallas_call(
        _sattn_kernel,
        grid=(n // blk,),
        in_specs=[row, kv, kv],
        out_specs=row,
        out_shape=jax.ShapeDtypeStruct((n, D_MODEL), F32),
        compiler_params=_params(1),
        name="sample_attn",
    )(q, _cache_rows(ck), _cache_rows(cv))


def _sback_kernel(x_ref, y_ref, xs_ref, z_ref, gaya_ref, gb_ref, gc_ref, att_ref, dful_ref,
                  gnw_ref, wout_ref, wao_ref, wmo_ref, fnw_ref, wg_ref, wu_ref, wd_ref, fw_ref,
                  o_ref):
    y = (y_ref[...] + dful_ref[...] * xs_ref[...]) * _silu(z_ref[...])
    y_b = _mm(_group_norm(y, gnw_ref[...]).astype(BF16), wout_ref[...])
    y_c = _mm(att_ref[...].astype(BF16), wao_ref[...])
    merged = gaya_ref[...] + gb_ref[...] * y_b + gc_ref[...] * y_c
    x1 = x_ref[...] + _mm(merged.astype(BF16), wmo_ref[...])
    o_ref[...] = _ffn_body(x1, fnw_ref[...], wg_ref, wu_ref, wd_ref, fw_ref[...])


def _sback_call(*args):
    n = args[0].shape[0]
    return pl.pallas_call(
        _sback_kernel,
        grid=(1,),
        in_specs=[_whole(a.shape) for a in args],
        out_specs=_whole((n, D_MODEL)),
        out_shape=jax.ShapeDtypeStruct((n, D_MODEL), F32),
        compiler_params=_params(1),
        name="sample_back",
    )(*args)


def kernel(x_prompt, x_sample, mem_prompt, cache_mem_k, cache_mem_v, state_conv, state_ssm_conv, state_ssm, norm_mix_w, w_in, sc_conv_w, w_sc_out, ssm_conv_w, ssm_conv_b, ssm_dt_bias, ssm_a_log, ssm_d, ssm_norm_w, w_ssm_out, norm_mem_w, w_mem_k, w_mem_v, w_attn_o, w_merge_o, norm_ffn_w, w_ffn_gate, w_ffn_up, w_ffn_down, norm_final_w):
    depth = w_in.shape[0]
    assert depth == 1
    bp = x_prompt.shape[0]
    ns = x_sample.shape[0]

    wi = w_in[0]
    o = 0
    wa = wi[:, o:o + 3 * SC_DIM].astype(BF16); o += 3 * SC_DIM
    wz = wi[:, o:o + SSM_D_INNER].astype(BF16); o += SSM_D_INNER
    wxbc = wi[:, o:o + SSM_CONV_DIM].astype(BF16); o += SSM_CONV_DIM
    wdt = jnp.pad(wi[:, o:o + SSM_HEADS], ((0, 0), (0, V7X_LANES - SSM_HEADS))).astype(BF16); o += SSM_HEADS
    wq = wi[:, o:o + D_MODEL].astype(BF16); o += D_MODEL
    wg = wi[:, o:o + 3 * D_MODEL].astype(BF16)
    wg_ac = jnp.concatenate([wg[:, 0:D_MODEL], wg[:, 2 * D_MODEL:]], axis=1)
    wg_b = wg[:, D_MODEL:2 * D_MODEL]
    row = lambda v: v.reshape(1, -1).astype(F32)
    pad_heads = lambda v: jnp.pad(row(v), ((0, 0), (0, V7X_LANES - SSM_HEADS)))
    nmix = row(norm_mix_w[0])
    dtb = pad_heads(ssm_dt_bias[0])
    alog = pad_heads(ssm_a_log[0])
    dful = row(jnp.repeat(ssm_d[0], SSM_HEAD_DIM))
    gnw = row(ssm_norm_w[0])
    cb = row(ssm_conv_b[0])
    cw = ssm_conv_w[0]
    scw = sc_conv_w[0]
    wsc = w_sc_out[0].astype(BF16)
    wout = w_ssm_out[0].astype(BF16)
    wao = w_attn_o[0].astype(BF16)
    wmo = w_merge_o[0].astype(BF16)
    wfg = w_ffn_gate[0].astype(BF16)
    wfu = w_ffn_up[0].astype(BF16)
    wfd = w_ffn_down[0].astype(BF16)
    nffn = row(norm_ffn_w[0])
    nfin = row(norm_final_w)
    expand = (jnp.arange(V7X_LANES)[:, None] == (jnp.arange(SSM_D_INNER)[None, :] // SSM_HEAD_DIM)).astype(BF16)

    mk, mv, mkb, mvb = _mem_call(mem_prompt, row(norm_mem_w[0]), w_mem_k[0].astype(BF16), w_mem_v[0].astype(BF16))
    ybg, p_ssmc, p_ssm = _ssd_call(x_prompt, nmix, wz, wxbc, wdt, wg_b, cw, cb, dtb, alog, dful, gnw, wout, expand)
    x1, p_conv = _mix_call(x_prompt, ybg, mkb, mvb, nmix, wa, wq, wg_ac, scw, wsc, wao, wmo)
    y_prompt = _ffn_call(x1, nffn, wfg, wfu, wfd, nfin)

    xs2 = x_sample.reshape(ns, D_MODEL)
    (gaya, gb, gc, z, xs, dtxt, dect, bm, cm, q, s_conv, s_ssmc) = _sfront_call(
        xs2, state_conv[0].reshape(ns, -1), jnp.swapaxes(state_ssm_conv[0], 0, 1),
        nmix, wa, wz, wxbc, wdt, wq, wg, scw, cw, cb, dtb, alog, wsc, expand)
    s_ssm, y_s = _sssm_call(state_ssm[0].reshape(ns, SSM_D_INNER, SSM_STATE), dtxt, dect, bm, cm)
    att = _sattn_call(q, cache_mem_k, cache_mem_v)
    y_sample = _sback_call(xs2, y_s, xs, z, gaya, gb, gc, att, dful, gnw, wout, wao, wmo,
                           nffn, wfg, wfu, wfd, nfin)

    def from_rows(r):
        r = r.reshape(bp, MEM_LEN, ATTN_HEAD_DIM // V7X_LANES, ATTN_HEADS, V7X_LANES)
        return r.transpose(0, 1, 3, 2, 4).reshape(depth, bp, MEM_LEN, ATTN_HEADS, ATTN_HEAD_DIM)

    state_shape = (SSM_HEADS, SSM_HEAD_DIM, SSM_STATE)
    return (
        y_prompt,
        y_sample.reshape(ns, 1, D_MODEL),
        from_rows(mk),
        from_rows(mv),
        p_conv.reshape(depth, bp, SC_WIDTH - 1, SC_DIM),
        p_ssmc.reshape(depth, bp, SSM_CONV - 1, SSM_CONV_DIM),
        p_ssm.reshape((depth, bp) + state_shape),
        s_conv.reshape(depth, ns, SC_WIDTH - 1, SC_DIM),
        jnp.swapaxes(s_ssmc, 0, 1).reshape(depth, ns, SSM_CONV - 1, SSM_CONV_DIM),
        s_ssm.reshape((depth, ns) + state_shape),
    )
```

```python
import jax
import jax.numpy as jnp
from jax import lax
from jax.experimental import pallas as pl
from jax.experimental.pallas import tpu as pltpu

F32 = jnp.float32
BF16 = jnp.bfloat16

D_MODEL = 1024
RMS_EPS = 1e-6
LOG2_E = 1.4426950408889634
SC_DIM = D_MODEL
SC_WIDTH = 3
SSM_D_INNER = 2 * D_MODEL
SSM_HEAD_DIM = 64
SSM_HEADS = SSM_D_INNER // SSM_HEAD_DIM
SSM_STATE = 128
SSM_GROUPS = 4
SSM_CONV = 4
SSM_CHUNK = 128
SSM_BC = SSM_GROUPS * SSM_STATE
SSM_CONV_DIM = SSM_D_INNER + 2 * SSM_BC
HEADS_PER_GROUP = SSM_HEADS // SSM_GROUPS
GROUP_WIDTH = SSM_D_INNER // SSM_GROUPS
MEM_LEN = 256
ATTN_HEADS = 4
ATTN_HEAD_DIM = D_MODEL // ATTN_HEADS
FFN_HIDDEN = ((8 * D_MODEL // 3 + 255) // 256) * 256

V7X_LANES = 128
V7X_SUBLANES = 8
V7X_VMEM_BYTES = 64 * 1024 * 1024
VMEM_LIMIT_BYTES = V7X_VMEM_BYTES - 8 * 1024 * 1024
CACHE_ROWS = ATTN_HEADS * ATTN_HEAD_DIM // V7X_LANES

SSD_TILE = 512
SSD_SUB = 256
MIX_TILE = 512
FFN_TILE = 512
SAMPLE_BLOCK = 8
CONV_COLS = 512


def _params(n_grid, flags=None):
    return pltpu.CompilerParams(
        dimension_semantics=("arbitrary",) * n_grid,
        vmem_limit_bytes=VMEM_LIMIT_BYTES,
        flags=flags,
    )


def _whole(shape):
    nd = len(shape)
    return pl.BlockSpec(shape, lambda *_: (0,) * nd)


def _mm(a, b):
    return jnp.dot(a, b, preferred_element_type=F32)


def _mm_nt(a, b):
    return lax.dot_general(a, b, (((1,), (1,)), ((), ())), preferred_element_type=F32)


def _split3(x):
    hi = x.astype(BF16)
    r = x - hi.astype(F32)
    mid = r.astype(BF16)
    lo = (r - mid.astype(F32)).astype(BF16)
    return hi, mid, lo


def _mm_sel(sel, x):
    hi, mid, lo = _split3(x)
    return _mm(sel, hi) + _mm(sel, mid) + _mm(sel, lo)


def _mm_expand(x, sel):
    hi, mid, lo = _split3(x)
    return _mm(hi, sel) + _mm(mid, sel) + _mm(lo, sel)


def _rms(x, w):
    return x * lax.rsqrt(jnp.mean(x * x, axis=-1, keepdims=True) + RMS_EPS) * w


def _sigmoid(x):
    return 1.0 / (1.0 + jnp.exp2(x * (-LOG2_E)))


def _silu(x):
    return x * _sigmoid(x)


def _softplus(x):
    return jnp.maximum(x, 0.0) + jnp.log1p(jnp.exp(-jnp.abs(x)))


def _shift_rows(u, prev8, k):
    r = pltpu.roll(u, k, axis=0)
    p = pltpu.roll(prev8, k, axis=0)
    row = lax.broadcasted_iota(jnp.int32, prev8.shape, 0)
    head = jnp.where(row < k, p, r[:V7X_SUBLANES])
    return jnp.concatenate([head, r[V7X_SUBLANES:]], axis=0)


def _softmax_rows(s):
    m = jnp.max(s, axis=-1, keepdims=True)
    p = jnp.exp(s - m)
    return p / jnp.sum(p, axis=-1, keepdims=True)


def _group_norm(y, w):
    outs = []
    for g in range(SSM_GROUPS):
        cols = slice(g * GROUP_WIDTH, (g + 1) * GROUP_WIDTH)
        yg = y[:, cols]
        ms = jnp.mean(yg * yg, axis=-1, keepdims=True)
        outs.append(yg * lax.rsqrt(ms + RMS_EPS) * w[:, cols])
    return jnp.concatenate(outs, axis=1)


def _mem_kernel(m_ref, nw_ref, wk_ref, wv_ref, k_ref, v_ref, kb_ref, vb_ref):
    mn = _rms(m_ref[...], nw_ref[...]).astype(BF16)
    k = _mm(mn, wk_ref[...])
    v = _mm(mn, wv_ref[...])
    kb_ref[...] = k.astype(BF16)
    vb_ref[...] = v.astype(BF16)
    for j in range(ATTN_HEAD_DIM // V7X_LANES):
        for h in range(ATTN_HEADS):
            rows = pl.ds(j * ATTN_HEADS + h, MEM_LEN, stride=CACHE_ROWS)
            lo = h * ATTN_HEAD_DIM + j * V7X_LANES
            k_ref[rows, :] = k[:, lo:lo + V7X_LANES]
            v_ref[rows, :] = v[:, lo:lo + V7X_LANES]


def _mem_call(mem, nw, wk, wv):
    b = mem.shape[0]
    blk = pl.BlockSpec((None, MEM_LEN, D_MODEL), lambda i: (i, 0, 0))
    rows = pl.BlockSpec((None, MEM_LEN * CACHE_ROWS, V7X_LANES), lambda i: (i, 0, 0))
    rows_shape = jax.ShapeDtypeStruct((b, MEM_LEN * CACHE_ROWS, V7X_LANES), F32)
    return pl.pallas_call(
        _mem_kernel,
        grid=(b,),
        in_specs=[blk, _whole(nw.shape), _whole(wk.shape), _whole(wv.shape)],
        out_specs=[rows, rows, blk, blk],
        out_shape=[
            rows_shape,
            rows_shape,
            jax.ShapeDtypeStruct(mem.shape, BF16),
            jax.ShapeDtypeStruct(mem.shape, BF16),
        ],
        compiler_params=_params(1),
        name="mem_kv",
    )(mem, nw, wk, wv)


def _ssd_kernel(x_ref, nw_ref, wz_ref, wxbc_ref, wdt_ref, wgb_ref, cw_ref, cb_ref,
                dtb_ref, alog_ref, dful_ref, gnw_ref, wout_ref, e_ref,
                ybg_ref, cst_ref, sst_ref,
                hist_ref, st_ref, *scratch):
    t = pl.program_id(1)
    tile = x_ref.shape[0]
    q = SSM_CHUNK
    n_sub = tile // SSD_SUB
    sets = [scratch[k * 8:(k + 1) * 8] for k in range(n_sub)]

    @pl.when(t == 0)
    def _():
        hist_ref[...] = jnp.zeros_like(hist_ref)
        st_ref[...] = jnp.zeros_like(st_ref)

    def project(sub):
        xn_s, z_s, xs_s, b_s, c_s, dt_s, da_s, _ = sets[sub]
        xn_s[...] = _rms(x_ref[sub * SSD_SUB:(sub + 1) * SSD_SUB, :], nw_ref[...]).astype(BF16)
        dt = _softplus(_mm(xn_s[...], wdt_ref[...]) + dtb_ref[...])
        dt_s[...] = dt
        da_s[...] = dt * (-jnp.exp(alog_ref[...]))
        n_blocks = SSM_CONV_DIM // CONV_COLS
        first_bc = SSM_D_INNER // CONV_COLS
        for j in list(range(first_bc, n_blocks)) + list(range(first_bc)):
            cols = slice(j * CONV_COLS, (j + 1) * CONV_COLS)
            u = _mm(xn_s[...], wxbc_ref[:, cols])
            prev8 = hist_ref[:, cols]
            hist_ref[:, cols] = u[SSD_SUB - V7X_SUBLANES:]
            cw = cw_ref[:, cols]
            conv = _shift_rows(u, prev8, 3) * cw[0:1]
            conv = conv + _shift_rows(u, prev8, 2) * cw[1:2]
            conv = conv + _shift_rows(u, prev8, 1) * cw[2:3]
            conv = conv + u * cw[3:4]
            act = _silu(conv + cb_ref[:, cols])
            lo = j * CONV_COLS
            if lo < SSM_D_INNER:
                xs_s[:, cols] = act
            elif lo < SSM_D_INNER + SSM_BC:
                b_s[:, lo - SSM_D_INNER:lo - SSM_D_INNER + CONV_COLS] = act
            else:
                off = lo - SSM_D_INNER - SSM_BC
                c_s[:, off:off + CONV_COLS] = act
        z_s[...] = _silu(_mm(xn_s[...], wz_ref[...]))

    ri = lax.broadcasted_iota(jnp.int32, (q, q), 0)
    ci = lax.broadcasted_iota(jnp.int32, (q, q), 1)
    causal = ri >= ci
    tri = jnp.where(causal, 1.0, 0.0).astype(BF16)
    lane_lo = ci < SSM_HEAD_DIM
    keep_lo = jnp.where(lane_lo, 1.0, 0.0).astype(BF16)
    keep_hi = jnp.where(lane_lo, 0.0, 1.0).astype(BF16)
    sub8 =lax.broadcasted_iota(jnp.int32, (V7X_SUBLANES, q), 0)

    groups = range(SSM_GROUPS)
    gsl = [slice(g * GROUP_WIDTH, (g + 1) * GROUP_WIDTH) for g in groups]
    nsl = [slice(g * SSM_STATE, (g + 1) * SSM_STATE) for g in groups]

    def scan(sub, chunks):
        _, z_s, xs_s, b_s, c_s, dt_s, da_s, yn_s = sets[sub]
        rows = {c: pl.ds(c * q, q) for c in chunks}

        acum, row_t, w, cd = {}, {}, {}, {}
        for c in chunks:
            a = _mm_sel(tri, da_s[rows[c], :])
            last = a[q - 1:q, :]
            dtc = dt_s[rows[c], :]
            acum[c] = a
            row_t[c] = a.T - jnp.log(dtc.T)
            w[c] = (dtc * jnp.exp(last - a)).astype(BF16)
            cd[c] = jnp.where(sub8 == 0, jnp.exp(last), 0.0)

        bg = {(c, g): b_s[rows[c], nsl[g]] for c in chunks for g in groups}
        cg = {(c, g): c_s[rows[c], nsl[g]].astype(BF16) for c in chunks for g in groups}
        cb = {k: _mm_nt(cg[k], bg[k].astype(BF16)).astype(BF16) for k in bg}
        wx = {(c, g): _mm(w[c], e_ref[:, gsl[g]]) for c in chunks for g in groups}
        dec = {(c, g): _mm_expand(cd[c], e_ref[:, gsl[g]])[0:1, :] for c in chunks for g in groups}

        yo = {}
        for c in chunks:
            st = [st_ref[:, gsl[g]] for g in groups]
            for g in groups:
                yo[c, g] = _mm(cg[c, g], st[g].astype(BF16))
            for g in groups:
                xw = (xs_s[rows[c], gsl[g]] * wx[c, g]).astype(BF16)
                st_ref[:, gsl[g]] = st[g] * dec[c, g] + _mm(bg[c, g].T.astype(BF16), xw)

        yd = {}
        for c in chunks:
            for g in groups:
                for pq in range(HEADS_PER_GROUP // 2):
                    h0 = g * HEADS_PER_GROUP + 2 * pq
                    h1 = h0 + 1
                    l0 = jnp.where(causal, jnp.exp(acum[c][:, h0:h0 + 1] - row_t[c][h0:h0 + 1, :]), 0.0)
                    l1 = jnp.where(causal, jnp.exp(acum[c][:, h1:h1 + 1] - row_t[c][h1:h1 + 1, :]), 0.0)
                    lhs = jnp.concatenate([cb[c, g] * l0.astype(BF16), cb[c, g] * l1.astype(BF16)], axis=1)
                    xpb = xs_s[rows[c], h0 * SSM_HEAD_DIM:(h1 + 1) * SSM_HEAD_DIM].astype(BF16)
                    rhs = jnp.concatenate([xpb * keep_lo, xpb * keep_hi], axis=0)
                    yd[c, g, pq] = _mm(lhs, rhs)

        for c in chunks:
            for g in groups:
                pairs = []
                for pq in range(HEADS_PER_GROUP // 2):
                    h0 = g * HEADS_PER_GROUP + 2 * pq
                    h1 = h0 + 1
                    pcols = slice(h0 * SSM_HEAD_DIM, (h1 + 1) * SSM_HEAD_DIM)
                    sc = jnp.where(lane_lo, jnp.exp(acum[c][:, h0:h0 + 1]), jnp.exp(acum[c][:, h1:h1 + 1]))
                    lc = slice(2 * pq * SSM_HEAD_DIM, (2 * pq + 2) * SSM_HEAD_DIM)
                    pairs.append(yd[c, g, pq] + sc * yo[c, g][:, lc] + dful_ref[:, pcols] * xs_s[rows[c], pcols])
                yg = jnp.concatenate(pairs, axis=1) * z_s[rows[c], gsl[g]]
                ms = jnp.mean(yg * yg, axis=-1, keepdims=True)
                yn_s[rows[c], gsl[g]] = (yg * lax.rsqrt(ms + RMS_EPS) * gnw_ref[:, gsl[g]]).astype(BF16)

    for sub in range(n_sub):
        project(sub)
    for sub in range(n_sub):
        scan(sub, list(range(SSD_SUB // q)))
        xn_s, yn_s = sets[sub][0], sets[sub][7]
        gb = _sigmoid(_mm(xn_s[...], wgb_ref[...]))
        ybg_ref[sub * SSD_SUB:(sub + 1) * SSD_SUB, :] = gb * _mm(yn_s[...], wout_ref[...])

    @pl.when(t == pl.num_programs(1) - 1)
    def _():
        cst_ref[...] = pltpu.roll(hist_ref[...], SSM_CONV - 1, axis=0)[0:SSM_CONV - 1]
        for k in range(SSM_D_INNER // V7X_LANES):
            blk = slice(k * V7X_LANES, (k + 1) * V7X_LANES)
            sst_ref[blk, :] = st_ref[:, blk].T


def _ssd_call(x, nw, wz, wxbc, wdt, wgb, cw, cb, dtb, alog, dful, gnw, wout, e):
    b, s, d = x.shape
    tile = SSD_TILE
    consts = (nw, wz, wxbc, wdt, wgb, cw, cb, dtb, alog, dful, gnw, wout, e)
    return pl.pallas_call(
        _ssd_kernel,
        grid=(b, s // tile),
        in_specs=[pl.BlockSpec((None, tile, d), lambda i, j: (i, j, 0))]
        + [_whole(c.shape) for c in consts],
        out_specs=[
            pl.BlockSpec((None, tile, d), lambda i, j: (i, j, 0)),
            pl.BlockSpec((None, SSM_CONV - 1, SSM_CONV_DIM), lambda i, j: (i, 0, 0)),
            pl.BlockSpec((None, SSM_D_INNER, SSM_STATE), lambda i, j: (i, 0, 0)),
        ],
        out_shape=[
            jax.ShapeDtypeStruct((b, s, d), F32),
            jax.ShapeDtypeStruct((b, SSM_CONV - 1, SSM_CONV_DIM), F32),
            jax.ShapeDtypeStruct((b, SSM_D_INNER, SSM_STATE), F32),
        ],
        scratch_shapes=[
            pltpu.VMEM((V7X_SUBLANES, SSM_CONV_DIM), F32),
            pltpu.VMEM((SSM_STATE, SSM_D_INNER), F32),
        ] + [
            pltpu.VMEM((SSD_SUB, d), BF16),
            pltpu.VMEM((SSD_SUB, SSM_D_INNER), F32),
            pltpu.VMEM((SSD_SUB, SSM_D_INNER), F32),
            pltpu.VMEM((SSD_SUB, SSM_BC), F32),
            pltpu.VMEM((SSD_SUB, SSM_BC), F32),
            pltpu.VMEM((SSD_SUB, V7X_LANES), F32),
            pltpu.VMEM((SSD_SUB, V7X_LANES), F32),
            pltpu.VMEM((SSD_SUB, SSM_D_INNER), BF16),
        ] * (tile // SSD_SUB),
        compiler_params=_params(2),
        name="ssd_prompt",
    )(x, *consts)


def _attention(q, k_ref, v_ref):
    outs = []
    for h in range(ATTN_HEADS):
        cols = slice(h * ATTN_HEAD_DIM, (h + 1) * ATTN_HEAD_DIM)
        s = _mm_nt(q[:, cols].astype(BF16), k_ref[:, cols]) * (ATTN_HEAD_DIM ** -0.5)
        outs.append(_mm(_softmax_rows(s).astype(BF16), v_ref[:, cols]))
    return jnp.concatenate(outs, axis=1)


def _mix_kernel(x_ref, ybg_ref, kb_ref, vb_ref, nw_ref, wa_ref, wq_ref, wg_ref, cw_ref,
                wsc_ref, wao_ref, wmo_ref, x1_ref, cst_ref, hist_ref):
    t = pl.program_id(1)
    tile = x_ref.shape[0]

    @pl.when(t == 0)
    def _():
        hist_ref[...] = jnp.zeros_like(hist_ref)

    x = x_ref[...]
    xn = _rms(x, nw_ref[...]).astype(BF16)

    sc_b = _mm(xn, wa_ref[:, 0:SC_DIM])
    u = _mm(xn, wa_ref[:, SC_DIM:2 * SC_DIM]) * _mm(xn, wa_ref[:, 2 * SC_DIM:3 * SC_DIM])
    prev8 = hist_ref[...]
    hist_ref[...] = u[tile - V7X_SUBLANES:]
    cw = cw_ref[...]
    conv = _shift_rows(u, prev8, 2) * cw[0:1]
    conv = conv + _shift_rows(u, prev8, 1) * cw[1:2]
    conv = conv + u * cw[2:3]
    y_a = _mm((sc_b * conv).astype(BF16), wsc_ref[...])

    att = _attention(_mm(xn, wq_ref[...]), kb_ref, vb_ref)
    y_c = _mm(att.astype(BF16), wao_ref[...])

    g_a = _sigmoid(_mm(xn, wg_ref[:, 0:D_MODEL]))
    g_c = _sigmoid(_mm(xn, wg_ref[:, D_MODEL:2 * D_MODEL]))
    merged = g_a * y_a + ybg_ref[...] + g_c * y_c
    x1_ref[...] = x + _mm(merged.astype(BF16), wmo_ref[...])

    @pl.when(t == pl.num_programs(1) - 1)
    def _():
        cst_ref[...] = pltpu.roll(hist_ref[...], SC_WIDTH - 1, axis=0)[0:SC_WIDTH - 1]


def _mix_call(x, ybg, kb, vb, nw, wa, wq, wg, cw, wsc, wao, wmo):
    b, s, d = x.shape
    tile = MIX_TILE
    consts = (nw, wa, wq, wg, cw, wsc, wao, wmo)
    tok = pl.BlockSpec((None, tile, d), lambda i, j: (i, j, 0))
    mem = pl.BlockSpec((None, MEM_LEN, d), lambda i, j: (i, 0, 0))
    return pl.pallas_call(
        _mix_kernel,
        grid=(b, s // tile),
        in_specs=[tok, tok, mem, mem] + [_whole(c.shape) for c in consts],
        out_specs=[tok, pl.BlockSpec((None, SC_WIDTH - 1, SC_DIM), lambda i, j: (i, 0, 0))],
        out_shape=[
            jax.ShapeDtypeStruct((b, s, d), F32),
            jax.ShapeDtypeStruct((b, SC_WIDTH - 1, SC_DIM), F32),
        ],
        scratch_shapes=[pltpu.VMEM((V7X_SUBLANES, SC_DIM), F32)],
        compiler_params=_params(2),
        name="mix_prompt",
    )(x, ybg, kb, vb, *consts)


def _ffn_body(x, nw, wg_ref, wu_ref, wd_ref, fw):
    xn = _rms(x, nw).astype(BF16)
    h = _silu(_mm(xn, wg_ref[...])) * _mm(xn, wu_ref[...])
    x2 = x + _mm(h.astype(BF16), wd_ref[...])
    return _rms(x2, fw)


def _ffn_kernel(x_ref, nw_ref, wg_ref, wu_ref, wd_ref, fw_ref, o_ref):
    o_ref[...] = _ffn_body(x_ref[...], nw_ref[...], wg_ref, wu_ref, wd_ref, fw_ref[...])


def _ffn_call(x, nw, wg, wu, wd, fw):
    b, s, d = x.shape
    tile = FFN_TILE
    consts = (nw, wg, wu, wd, fw)
    tok = pl.BlockSpec((None, tile, d), lambda i, j: (i, j, 0))
    return pl.pallas_call(
        _ffn_kernel,
        grid=(b, s // tile),
        in_specs=[tok] + [_whole(c.shape) for c in consts],
        out_specs=tok,
        out_shape=jax.ShapeDtypeStruct((b, s, d), F32),
        compiler_params=_params(2),
        name="ffn_prompt",
    )(x, *consts)


def _sfront_kernel(x_ref, sconv_ref, ssmc_ref, nw_ref, wa_ref, wz_ref, wxbc_ref, wdt_ref,
                   wq_ref, wg_ref, scw_ref, cw_ref, cb_ref, dtb_ref, alog_ref, wsc_ref, e_ref,
                   gaya_ref, gb_ref, gc_ref, z_ref, xs_ref, dtxt_ref, dect_ref, b_ref, c_ref,
                   q_ref, sconv_o, ssmc_o):
    xn = _rms(x_ref[...], nw_ref[...]).astype(BF16)

    sc_b = _mm(xn, wa_ref[:, 0:SC_DIM])
    u = _mm(xn, wa_ref[:, SC_DIM:2 * SC_DIM]) * _mm(xn, wa_ref[:, 2 * SC_DIM:3 * SC_DIM])
    h0 = sconv_ref[:, 0:SC_DIM]
    h1 = sconv_ref[:, SC_DIM:2 * SC_DIM]
    scw = scw_ref[...]
    conv = h0 * scw[0:1] + h1 * scw[1:2] + u * scw[2:3]
    sconv_o[:, 0:SC_DIM] = h1
    sconv_o[:, SC_DIM:2 * SC_DIM] = u
    y_a = _mm((sc_b * conv).astype(BF16), wsc_ref[...])
    gaya_ref[...] = _sigmoid(_mm(xn, wg_ref[:, 0:D_MODEL])) * y_a
    gb_ref[...] = _sigmoid(_mm(xn, wg_ref[:, D_MODEL:2 * D_MODEL]))
    gc_ref[...] = _sigmoid(_mm(xn, wg_ref[:, 2 * D_MODEL:3 * D_MODEL]))
    z_ref[...] = _mm(xn, wz_ref[...])
    q_ref[...] = _mm(xn, wq_ref[...])

    dt = _softplus(_mm(xn, wdt_ref[...]) + dtb_ref[...])
    dect_ref[...] = jnp.exp(dt * (-jnp.exp(alog_ref[...]))).T
    dtexp = _mm_expand(dt, e_ref[...])

    for j in range(SSM_CONV_DIM // CONV_COLS):
        lo = j * CONV_COLS
        cols = slice(lo, lo + CONV_COLS)
        xbc = _mm(xn, wxbc_ref[:, cols])
        p0 = ssmc_ref[0, :, cols]
        p1 = ssmc_ref[1, :, cols]
        p2 = ssmc_ref[2, :, cols]
        cw = cw_ref[:, cols]
        conv = p0 * cw[0:1] + p1 * cw[1:2] + p2 * cw[2:3] + xbc * cw[3:4]
        ssmc_o[0, :, cols] = p1
        ssmc_o[1, :, cols] = p2
        ssmc_o[2, :, cols] = xbc
        act = _silu(conv + cb_ref[:, cols])
        if lo < SSM_D_INNER:
            xs_ref[:, cols] = act
            dtx = act * dtexp[:, cols]
            for k in range(CONV_COLS // V7X_LANES):
                r0 = lo + k * V7X_LANES
                dtxt_ref[r0:r0 + V7X_LANES, :] = dtx[:, k * V7X_LANES:(k + 1) * V7X_LANES].T
        elif lo < SSM_D_INNER + SSM_BC:
            b_ref[:, lo - SSM_D_INNER:lo - SSM_D_INNER + CONV_COLS] = act
        else:
            off = lo - SSM_D_INNER - SSM_BC
            c_ref[:, off:off + CONV_COLS] = act


def _sfront_call(x, sconv, ssmc, nw, wa, wz, wxbc, wdt, wq, wg, scw, cw, cb, dtb, alog, wsc, e):
    n = x.shape[0]
    args = (x, sconv, ssmc, nw, wa, wz, wxbc, wdt, wq, wg, scw, cw, cb, dtb, alog, wsc, e)
    shapes = [
        (n, D_MODEL), (n, D_MODEL), (n, D_MODEL),
        (n, SSM_D_INNER), (n, SSM_D_INNER),
        (SSM_D_INNER, n), (V7X_LANES, n),
        (n, SSM_BC), (n, SSM_BC),
        (n, D_MODEL),
        sconv.shape, ssmc.shape,
    ]
    return pl.pallas_call(
        _sfront_kernel,
        grid=(1,),
        in_specs=[_whole(a.shape) for a in args],
        out_specs=[_whole(s) for s in shapes],
        out_shape=[jax.ShapeDtypeStruct(s, F32) for s in shapes],
        compiler_params=_params(1),
        name="sample_front",
    )(*args)


def _sssm_kernel(s_ref, dtxt_ref, dect_ref, b_ref, c_ref, so_ref, y_ref):
    i = pl.program_id(0)
    n = dtxt_ref.shape[1]
    lane = lax.broadcasted_iota(jnp.int32, (1, n), 1)
    sub = lax.broadcasted_iota(jnp.int32, (V7X_SUBLANES, SSM_STATE), 0)
    for k in range(SAMPLE_BLOCK):
        onehot = (lane == i * SAMPLE_BLOCK + k).astype(F32)
        xcol = jnp.sum(dtxt_ref[...] * onehot, axis=-1, keepdims=True)
        dcol = jnp.sum(dect_ref[0:SSM_HEADS, :] * onehot, axis=-1, keepdims=True)
        for h in range(SSM_HEADS):
            g = h // HEADS_PER_GROUP
            rows = slice(h * SSM_HEAD_DIM, (h + 1) * SSM_HEAD_DIM)
            brow = b_ref[k:k + 1, g * SSM_STATE:(g + 1) * SSM_STATE]
            so_ref[k, rows, :] = s_ref[k, rows, :] * dcol[h:h + 1, :] + xcol[rows, :] * brow
        c8 = jnp.zeros((V7X_SUBLANES, SSM_STATE), F32)
        for g in range(SSM_GROUPS):
            c8 = jnp.where(sub == g, c_ref[k:k + 1, g * SSM_STATE:(g + 1) * SSM_STATE], c8)
        y8 = _mm_nt(c8.astype(BF16), so_ref[k].astype(BF16))
        y_ref[k:k + 1, :] = jnp.concatenate(
            [y8[g:g + 1, g * GROUP_WIDTH:(g + 1) * GROUP_WIDTH] for g in range(SSM_GROUPS)], axis=1)


def _sssm_call(state, dtxt, dect, bm, cm):
    n = state.shape[0]
    blk = SAMPLE_BLOCK
    st = pl.BlockSpec((blk, SSM_D_INNER, SSM_STATE), lambda i: (i, 0, 0))
    return pl.pallas_call(
        _sssm_kernel,
        grid=(n // blk,),
        in_specs=[st, _whole(dtxt.shape), _whole(dect.shape),
                  pl.BlockSpec((blk, SSM_BC), lambda i: (i, 0)),
                  pl.BlockSpec((blk, SSM_BC), lambda i: (i, 0))],
        out_specs=[st, pl.BlockSpec((blk, SSM_D_INNER), lambda i: (i, 0))],
        out_shape=[jax.ShapeDtypeStruct(state.shape, F32),
                   jax.ShapeDtypeStruct((n, SSM_D_INNER), F32)],
        compiler_params=_params(1),
        name="sample_ssm",
    )(state, dtxt, dect, bm, cm)


def _cache_rows(c):
    n = c.shape[1]
    c = c.reshape(n, MEM_LEN, ATTN_HEADS, ATTN_HEAD_DIM // V7X_LANES, V7X_LANES)
    return c.transpose(0, 1, 3, 2, 4).reshape(n * MEM_LEN * CACHE_ROWS, V7X_LANES)


def _cache_head(ref, k, h):
    base = k * MEM_LEN * CACHE_ROWS
    halves = [ref[pl.ds(base + j * ATTN_HEADS + h, MEM_LEN, stride=CACHE_ROWS), :]
              for j in range(ATTN_HEAD_DIM // V7X_LANES)]
    return jnp.concatenate(halves, axis=1).astype(BF16)


def _sattn_kernel(q_ref, k_ref, v_ref, o_ref):
    sub = lax.broadcasted_iota(jnp.int32, (SAMPLE_BLOCK, MEM_LEN), 0)
    sub_o = lax.broadcasted_iota(jnp.int32, (SAMPLE_BLOCK, ATTN_HEAD_DIM), 0)
    outs = []
    for h in range(ATTN_HEADS):
        qh = q_ref[:, h * ATTN_HEAD_DIM:(h + 1) * ATTN_HEAD_DIM].astype(BF16)
        s = jnp.zeros((SAMPLE_BLOCK, MEM_LEN), F32)
        for k in range(SAMPLE_BLOCK):
            s = jnp.where(sub == k, _mm_nt(qh, _cache_head(k_ref, k, h)), s)
        p = _softmax_rows(s * (ATTN_HEAD_DIM ** -0.5)).astype(BF16)
        o = jnp.zeros((SAMPLE_BLOCK, ATTN_HEAD_DIM), F32)
        for k in range(SAMPLE_BLOCK):
            o = jnp.where(sub_o == k, _mm(p, _cache_head(v_ref, k, h)), o)
        outs.append(o)
    o_ref[...] = jnp.concatenate(outs, axis=1)


def _sattn_call(q, ck, cv):
    n = q.shape[0]
    blk = SAMPLE_BLOCK
    kv = pl.BlockSpec((blk * MEM_LEN * CACHE_ROWS, V7X_LANES), lambda i: (i, 0))
    row = pl.BlockSpec((blk, D_MODEL), lambda i: (i, 0))
    return pl.pallas_call(
        _sattn_kernel,
        grid=(n // blk,),
        in_specs=[row, kv, kv],
        out_specs=row,
        out_shape=jax.ShapeDtypeStruct((n, D_MODEL), F32),
        compiler_params=_params(1),
        name="sample_attn",
    )(q, _cache_rows(ck), _cache_rows(cv))


def _sback_kernel(x_ref, y_ref, xs_ref, z_ref, gaya_ref, gb_ref, gc_ref, att_ref, dful_ref,
                  gnw_ref, wout_ref, wao_ref, wmo_ref, fnw_ref, wg_ref, wu_ref, wd_ref, fw_ref,
                  o_ref):
    y = (y_ref[...] + dful_ref[...] * xs_ref[...]) * _silu(z_ref[...])
    y_b = _mm(_group_norm(y, gnw_ref[...]).astype(BF16), wout_ref[...])
    y_c = _mm(att_ref[...].astype(BF16), wao_ref[...])
    merged = gaya_ref[...] + gb_ref[...] * y_b + gc_ref[...] * y_c
    x1 = x_ref[...] + _mm(merged.astype(BF16), wmo_ref[...])
    o_ref[...] = _ffn_body(x1, fnw_ref[...], wg_ref, wu_ref, wd_ref, fw_ref[...])


def _sback_call(*args):
    n = args[0].shape[0]
    return pl.pallas_call(
        _sback_kernel,
        grid=(1,),
        in_specs=[_whole(a.shape) for a in args],
        out_specs=_whole((n, D_MODEL)),
        out_shape=jax.ShapeDtypeStruct((n, D_MODEL), F32),
        compiler_params=_params(1),
        name="sample_back",
    )(*args)


def kernel(x_prompt, x_sample, mem_prompt, cache_mem_k, cache_mem_v, state_conv, state_ssm_conv, state_ssm, norm_mix_w, w_in, sc_conv_w, w_sc_out, ssm_conv_w, ssm_conv_b, ssm_dt_bias, ssm_a_log, ssm_d, ssm_norm_w, w_ssm_out, norm_mem_w, w_mem_k, w_mem_v, w_attn_o, w_merge_o, norm_ffn_w, w_ffn_gate, w_ffn_up, w_ffn_down, norm_final_w):
    depth = w_in.shape[0]
    assert depth == 1
    bp = x_prompt.shape[0]
    ns = x_sample.shape[0]

    wi = w_in[0]
    o = 0
    wa = wi[:, o:o + 3 * SC_DIM].astype(BF16); o += 3 * SC_DIM
    wz = wi[:, o:o + SSM_D_INNER].astype(BF16); o += SSM_D_INNER
    wxbc = wi[:, o:o + SSM_CONV_DIM].astype(BF16); o += SSM_CONV_DIM
    wdt = jnp.pad(wi[:, o:o + SSM_HEADS], ((0, 0), (0, V7X_LANES - SSM_HEADS))).astype(BF16); o += SSM_HEADS
    wq = wi[:, o:o + D_MODEL].astype(BF16); o += D_MODEL
    wg = wi[:, o:o + 3 * D_MODEL].astype(BF16)
    wg_ac = jnp.concatenate([wg[:, 0:D_MODEL], wg[:, 2 * D_MODEL:]], axis=1)
    wg_b = wg[:, D_MODEL:2 * D_MODEL]
    row = lambda v: v.reshape(1, -1).astype(F32)
    pad_heads = lambda v: jnp.pad(row(v), ((0, 0), (0, V7X_LANES - SSM_HEADS)))
    nmix = row(norm_mix_w[0])
    dtb = pad_heads(ssm_dt_bias[0])
    alog = pad_heads(ssm_a_log[0])
    dful = row(jnp.repeat(ssm_d[0], SSM_HEAD_DIM))
    gnw = row(ssm_norm_w[0])
    cb = row(ssm_conv_b[0])
    cw = ssm_conv_w[0]
    scw = sc_conv_w[0]
    wsc = w_sc_out[0].astype(BF16)
    wout = w_ssm_out[0].astype(BF16)
    wao = w_attn_o[0].astype(BF16)
    wmo = w_merge_o[0].astype(BF16)
    wfg = w_ffn_gate[0].astype(BF16)
    wfu = w_ffn_up[0].astype(BF16)
    wfd = w_ffn_down[0].astype(BF16)
    nffn = row(norm_ffn_w[0])
    nfin = row(norm_final_w)
    expand = (jnp.arange(V7X_LANES)[:, None] == (jnp.arange(SSM_D_INNER)[None, :] // SSM_HEAD_DIM)).astype(BF16)

    mk, mv, mkb, mvb = _mem_call(mem_prompt, row(norm_mem_w[0]), w_mem_k[0].astype(BF16), w_mem_v[0].astype(BF16))
    ybg, p_ssmc, p_ssm = _ssd_call(x_prompt, nmix, wz, wxbc, wdt, wg_b, cw, cb, dtb, alog, dful, gnw, wout, expand)
    x1, p_conv = _mix_call(x_prompt, ybg, mkb, mvb, nmix, wa, wq, wg_ac, scw, wsc, wao, wmo)
    y_prompt = _ffn_call(x1, nffn, wfg, wfu, wfd, nfin)

    xs2 = x_sample.reshape(ns, D_MODEL)
    (gaya, gb, gc, z, xs, dtxt, dect, bm, cm, q, s_conv, s_ssmc) = _sfront_call(
        xs2, state_conv[0].reshape(ns, -1), jnp.swapaxes(state_ssm_conv[0], 0, 1),
        nmix, wa, wz, wxbc, wdt, wq, wg, scw, cw, cb, dtb, alog, wsc, expand)
    s_ssm, y_s = _sssm_call(state_ssm[0].reshape(ns, SSM_D_INNER, SSM_STATE), dtxt, dect, bm, cm)
    att = _sattn_call(q, cache_mem_k, cache_mem_v)
    y_sample = _sback_call(xs2, y_s, xs, z, gaya, gb, gc, att, dful, gnw, wout, wao, wmo,
                           nffn, wfg, wfu, wfd, nfin)

    def from_rows(r):
        r = r.reshape(bp, MEM_LEN, ATTN_HEAD_DIM // V7X_LANES, ATTN_HEADS, V7X_LANES)
        return r.transpose(0, 1, 3, 2, 4).reshape(depth, bp, MEM_LEN, ATTN_HEADS, ATTN_HEAD_DIM)

    state_shape = (SSM_HEADS, SSM_HEAD_DIM, SSM_STATE)
    return (
        y_prompt,
        y_sample.reshape(ns, 1, D_MODEL),
        from_rows(mk),
        from_rows(mv),
        p_conv.reshape(depth, bp, SC_WIDTH - 1, SC_DIM),
        p_ssmc.reshape(depth, bp, SSM_CONV - 1, SSM_CONV_DIM),
        p_ssm.reshape((depth, bp) + state_shape),
        s_conv.reshape(depth, ns, SC_WIDTH - 1, SC_DIM),
        jnp.swapaxes(s_ssmc, 0, 1).reshape(depth, ns, SSM_CONV - 1, SSM_CONV_DIM),
        s_ssm.reshape((depth, ns) + state_shape),
    )
```

```python
import jax
import jax.numpy as jnp
from jax import lax
from jax.experimental import pallas as pl
from jax.experimental.pallas import tpu as pltpu

F32 = jnp.float32
BF16 = jnp.bfloat16

D_MODEL = 1024
RMS_EPS = 1e-6
LOG2_E = 1.4426950408889634
SC_DIM = D_MODEL
SC_WIDTH = 3
SSM_D_INNER = 2 * D_MODEL
SSM_HEAD_DIM = 64
SSM_HEADS = SSM_D_INNER // SSM_HEAD_DIM
SSM_STATE = 128
SSM_GROUPS = 4
SSM_CONV = 4
SSM_CHUNK = 128
SSM_BC = SSM_GROUPS * SSM_STATE
SSM_CONV_DIM = SSM_D_INNER + 2 * SSM_BC
HEADS_PER_GROUP = SSM_HEADS // SSM_GROUPS
GROUP_WIDTH = SSM_D_INNER // SSM_GROUPS
MEM_LEN = 256
ATTN_HEADS = 4
ATTN_HEAD_DIM = D_MODEL // ATTN_HEADS
FFN_HIDDEN = ((8 * D_MODEL // 3 + 255) // 256) * 256

V7X_LANES = 128
V7X_SUBLANES = 8
V7X_VMEM_BYTES = 64 * 1024 * 1024
VMEM_LIMIT_BYTES = V7X_VMEM_BYTES - 8 * 1024 * 1024
CACHE_ROWS = ATTN_HEADS * ATTN_HEAD_DIM // V7X_LANES

SSD_TILE = 512
SSD_SUB = 256
MIX_TILE = 512
FFN_TILE = 512
SAMPLE_BLOCK = 8
CONV_COLS = 512


def _params(n_grid, flags=None):
    return pltpu.CompilerParams(
        dimension_semantics=("arbitrary",) * n_grid,
        vmem_limit_bytes=VMEM_LIMIT_BYTES,
        flags=flags,
    )


def _whole(shape):
    nd = len(shape)
    return pl.BlockSpec(shape, lambda *_: (0,) * nd)


def _mm(a, b):
    return jnp.dot(a, b, preferred_element_type=F32)


def _mm_nt(a, b):
    return lax.dot_general(a, b, (((1,), (1,)), ((), ())), preferred_element_type=F32)


def _split3(x):
    hi = x.astype(BF16)
    r = x - hi.astype(F32)
    mid = r.astype(BF16)
    lo = (r - mid.astype(F32)).astype(BF16)
    return hi, mid, lo


def _mm_sel(sel, x):
    hi, mid, lo = _split3(x)
    return _mm(sel, hi) + _mm(sel, mid) + _mm(sel, lo)


def _mm_expand(x, sel):
    hi, mid, lo = _split3(x)
    return _mm(hi, sel) + _mm(mid, sel) + _mm(lo, sel)


def _rms(x, w):
    return x * lax.rsqrt(jnp.mean(x * x, axis=-1, keepdims=True) + RMS_EPS) * w


def _sigmoid(x):
    return 1.0 / (1.0 + jnp.exp2(x * (-LOG2_E)))


def _silu(x):
    return x * _sigmoid(x)


def _softplus(x):
    return jnp.maximum(x, 0.0) + jnp.log1p(jnp.exp(-jnp.abs(x)))


def _shift_rows(u, prev8, k):
    r = pltpu.roll(u, k, axis=0)
    p = pltpu.roll(prev8, k, axis=0)
    row = lax.broadcasted_iota(jnp.int32, prev8.shape, 0)
    head = jnp.where(row < k, p, r[:V7X_SUBLANES])
    return jnp.concatenate([head, r[V7X_SUBLANES:]], axis=0)


def _softmax_rows(s):
    m = jnp.max(s, axis=-1, keepdims=True)
    p = jnp.exp(s - m)
    return p / jnp.sum(p, axis=-1, keepdims=True)


def _group_norm(y, w):
    outs = []
    for g in range(SSM_GROUPS):
        cols = slice(g * GROUP_WIDTH, (g + 1) * GROUP_WIDTH)
        yg = y[:, cols]
        ms = jnp.mean(yg * yg, axis=-1, keepdims=True)
        outs.append(yg * lax.rsqrt(ms + RMS_EPS) * w[:, cols])
    return jnp.concatenate(outs, axis=1)


def _mem_kernel(m_ref, nw_ref, wk_ref, wv_ref, k_ref, v_ref, kb_ref, vb_ref):
    mn = _rms(m_ref[...], nw_ref[...]).astype(BF16)
    k = _mm(mn, wk_ref[...])
    v = _mm(mn, wv_ref[...])
    kb_ref[...] = k.astype(BF16)
    vb_ref[...] = v.astype(BF16)
    for j in range(ATTN_HEAD_DIM // V7X_LANES):
        for h in range(ATTN_HEADS):
            rows = pl.ds(j * ATTN_HEADS + h, MEM_LEN, stride=CACHE_ROWS)
            lo = h * ATTN_HEAD_DIM + j * V7X_LANES
            k_ref[rows, :] = k[:, lo:lo + V7X_LANES]
            v_ref[rows, :] = v[:, lo:lo + V7X_LANES]


def _mem_call(mem, nw, wk, wv):
    b = mem.shape[0]
    blk = pl.BlockSpec((None, MEM_LEN, D_MODEL), lambda i: (i, 0, 0))
    rows = pl.BlockSpec((None, MEM_LEN * CACHE_ROWS, V7X_LANES), lambda i: (i, 0, 0))
    rows_shape = jax.ShapeDtypeStruct((b, MEM_LEN * CACHE_ROWS, V7X_LANES), F32)
    return pl.pallas_call(
        _mem_kernel,
        grid=(b,),
        in_specs=[blk, _whole(nw.shape), _whole(wk.shape), _whole(wv.shape)],
        out_specs=[rows, rows, blk, blk],
        out_shape=[
            rows_shape,
            rows_shape,
            jax.ShapeDtypeStruct(mem.shape, BF16),
            jax.ShapeDtypeStruct(mem.shape, BF16),
        ],
        compiler_params=_params(1),
        name="mem_kv",
    )(mem, nw, wk, wv)


def _ssd_kernel(x_ref, nw_ref, wz_in, wxbc_in, wdt_ref, wgb_ref, cw_ref, cb_ref,
                dtb_ref, alog_ref, dful_ref, gnw_ref, wout_ref, e_ref,
                ybg_ref, cst_ref, sst_ref,
                hist_ref, st_ref, *scratch):
    t = pl.program_id(1)
    tile = x_ref.shape[0]
    q = SSM_CHUNK
    n_sub = tile // SSD_SUB
    sets = [scratch[k * 8:(k + 1) * 8] for k in range(n_sub)]
    wxbc_ref, wz_ref = scratch[n_sub * 8:]

    @pl.when((pl.program_id(0) == 0) & (t == 0))
    def _():
        wxbc_ref[...] = wxbc_in[...]
        wz_ref[...] = wz_in[...]

    @pl.when(t == 0)
    def _():
        hist_ref[...] = jnp.zeros_like(hist_ref)
        st_ref[...] = jnp.zeros_like(st_ref)

    def project(sub):
        xn_s, z_s, xs_s, b_s, c_s, dt_s, da_s, _ = sets[sub]
        xn_s[...] = _rms(x_ref[sub * SSD_SUB:(sub + 1) * SSD_SUB, :], nw_ref[...]).astype(BF16)
        dt = _softplus(_mm(xn_s[...], wdt_ref[...]) + dtb_ref[...])
        dt_s[...] = dt
        da_s[...] = dt * (-jnp.exp(alog_ref[...]))
        n_blocks = SSM_CONV_DIM // CONV_COLS
        first_bc = SSM_D_INNER // CONV_COLS
        for j in list(range(first_bc, n_blocks)) + list(range(first_bc)):
            cols = slice(j * CONV_COLS, (j + 1) * CONV_COLS)
            u = _mm(xn_s[...], wxbc_ref[:, cols])
            prev8 = hist_ref[:, cols]
            hist_ref[:, cols] = u[SSD_SUB - V7X_SUBLANES:]
            cw = cw_ref[:, cols]
            conv = _shift_rows(u, prev8, 3) * cw[0:1]
            conv = conv + _shift_rows(u, prev8, 2) * cw[1:2]
            conv = conv + _shift_rows(u, prev8, 1) * cw[2:3]
            conv = conv + u * cw[3:4]
            act = _silu(conv + cb_ref[:, cols])
            lo = j * CONV_COLS
            if lo < SSM_D_INNER:
                xs_s[:, cols] = act
            elif lo < SSM_D_INNER + SSM_BC:
                b_s[:, lo - SSM_D_INNER:lo - SSM_D_INNER + CONV_COLS] = act
            else:
                off = lo - SSM_D_INNER - SSM_BC
                c_s[:, off:off + CONV_COLS] = act
        z_s[...] = _silu(_mm(xn_s[...], wz_ref[...]))

    ri = lax.broadcasted_iota(jnp.int32, (q, q), 0)
    ci = lax.broadcasted_iota(jnp.int32, (q, q), 1)
    causal = ri >= ci
    tri = jnp.where(causal, 1.0, 0.0).astype(BF16)
    lane_lo = ci < SSM_HEAD_DIM
    keep_lo = jnp.where(lane_lo, 1.0, 0.0).astype(BF16)
    keep_hi = jnp.where(lane_lo, 0.0, 1.0).astype(BF16)
    sub8 =lax.broadcasted_iota(jnp.int32, (V7X_SUBLANES, q), 0)

    groups = range(SSM_GROUPS)
    gsl = [slice(g * GROUP_WIDTH, (g + 1) * GROUP_WIDTH) for g in groups]
    nsl = [slice(g * SSM_STATE, (g + 1) * SSM_STATE) for g in groups]

    def scan(sub, chunks):
        _, z_s, xs_s, b_s, c_s, dt_s, da_s, yn_s = sets[sub]
        rows = {c: pl.ds(c * q, q) for c in chunks}

        acum, row_t, w, cd = {}, {}, {}, {}
        for c in chunks:
            a = _mm_sel(tri, da_s[rows[c], :])
            last = a[q - 1:q, :]
            dtc = dt_s[rows[c], :]
            acum[c] = a
            row_t[c] = a.T - jnp.log(dtc.T)
            w[c] = (dtc * jnp.exp(last - a)).astype(BF16)
            cd[c] = jnp.where(sub8 == 0, jnp.exp(last), 0.0)

        bg = {(c, g): b_s[rows[c], nsl[g]] for c in chunks for g in groups}
        cg = {(c, g): c_s[rows[c], nsl[g]].astype(BF16) for c in chunks for g in groups}
        cb = {k: _mm_nt(cg[k], bg[k].astype(BF16)).astype(BF16) for k in bg}
        wx = {(c, g): _mm(w[c], e_ref[:, gsl[g]]) for c in chunks for g in groups}
        dec = {(c, g): _mm_expand(cd[c], e_ref[:, gsl[g]])[0:1, :] for c in chunks for g in groups}

        yo = {}
        for c in chunks:
            st = [st_ref[:, gsl[g]] for g in groups]
            for g in groups:
                yo[c, g] = _mm(cg[c, g], st[g].astype(BF16))
            for g in groups:
                xw = (xs_s[rows[c], gsl[g]] * wx[c, g]).astype(BF16)
                st_ref[:, gsl[g]] = st[g] * dec[c, g] + _mm(bg[c, g].T.astype(BF16), xw)

        yd = {}
        for c in chunks:
            for g in groups:
                for pq in range(HEADS_PER_GROUP // 2):
                    h0 = g * HEADS_PER_GROUP + 2 * pq
                    h1 = h0 + 1
                    l0 = jnp.where(causal, jnp.exp(acum[c][:, h0:h0 + 1] - row_t[c][h0:h0 + 1, :]), 0.0)
                    l1 = jnp.where(causal, jnp.exp(acum[c][:, h1:h1 + 1] - row_t[c][h1:h1 + 1, :]), 0.0)
                    lhs = jnp.concatenate([cb[c, g] * l0.astype(BF16), cb[c, g] * l1.astype(BF16)], axis=1)
                    xpb = xs_s[rows[c], h0 * SSM_HEAD_DIM:(h1 + 1) * SSM_HEAD_DIM].astype(BF16)
                    rhs = jnp.concatenate([xpb * keep_lo, xpb * keep_hi], axis=0)
                    yd[c, g, pq] = _mm(lhs, rhs)

        for c in chunks:
            for g in groups:
                pairs = []
                for pq in range(HEADS_PER_GROUP // 2):
                    h0 = g * HEADS_PER_GROUP + 2 * pq
                    h1 = h0 + 1
                    pcols = slice(h0 * SSM_HEAD_DIM, (h1 + 1) * SSM_HEAD_DIM)
                    sc = jnp.where(lane_lo, jnp.exp(acum[c][:, h0:h0 + 1]), jnp.exp(acum[c][:, h1:h1 + 1]))
                    lc = slice(2 * pq * SSM_HEAD_DIM, (2 * pq + 2) * SSM_HEAD_DIM)
                    pairs.append(yd[c, g, pq] + sc * yo[c, g][:, lc] + dful_ref[:, pcols] * xs_s[rows[c], pcols])
                yg = jnp.concatenate(pairs, axis=1) * z_s[rows[c], gsl[g]]
                ms = jnp.mean(yg * yg, axis=-1, keepdims=True)
                yn_s[rows[c], gsl[g]] = (yg * lax.rsqrt(ms + RMS_EPS) * gnw_ref[:, gsl[g]]).astype(BF16)

    for sub in range(n_sub):
        project(sub)
    for sub in range(n_sub):
        for c in range(SSD_SUB // q):
            scan(sub, [c])
        xn_s, yn_s = sets[sub][0], sets[sub][7]
        gb = _sigmoid(_mm(xn_s[...], wgb_ref[...]))
        ybg_ref[sub * SSD_SUB:(sub + 1) * SSD_SUB, :] = gb * _mm(yn_s[...], wout_ref[...])

    @pl.when(t == pl.num_programs(1) - 1)
    def _():
        cst_ref[...] = pltpu.roll(hist_ref[...], SSM_CONV - 1, axis=0)[0:SSM_CONV - 1]
        for k in range(SSM_D_INNER // V7X_LANES):
            blk = slice(k * V7X_LANES, (k + 1) * V7X_LANES)
            sst_ref[blk, :] = st_ref[:, blk].T


def _ssd_call(x, nw, wz, wxbc, wdt, wgb, cw, cb, dtb, alog, dful, gnw, wout, e):
    b, s, d = x.shape
    tile = SSD_TILE
    consts = (nw, wz, wxbc, wdt, wgb, cw, cb, dtb, alog, dful, gnw, wout, e)
    return pl.pallas_call(
        _ssd_kernel,
        grid=(b, s // tile),
        in_specs=[pl.BlockSpec((None, tile, d), lambda i, j: (i, j, 0))]
        + [_whole(c.shape) for c in consts],
        out_specs=[
            pl.BlockSpec((None, tile, d), lambda i, j: (i, j, 0)),
            pl.BlockSpec((None, SSM_CONV - 1, SSM_CONV_DIM), lambda i, j: (i, 0, 0)),
            pl.BlockSpec((None, SSM_D_INNER, SSM_STATE), lambda i, j: (i, 0, 0)),
        ],
        out_shape=[
            jax.ShapeDtypeStruct((b, s, d), F32),
            jax.ShapeDtypeStruct((b, SSM_CONV - 1, SSM_CONV_DIM), F32),
            jax.ShapeDtypeStruct((b, SSM_D_INNER, SSM_STATE), F32),
        ],
        scratch_shapes=[
            pltpu.VMEM((V7X_SUBLANES, SSM_CONV_DIM), F32),
            pltpu.VMEM((SSM_STATE, SSM_D_INNER), F32),
        ] + [
            pltpu.VMEM((SSD_SUB, d), BF16),
            pltpu.VMEM((SSD_SUB, SSM_D_INNER), F32),
            pltpu.VMEM((SSD_SUB, SSM_D_INNER), F32),
            pltpu.VMEM((SSD_SUB, SSM_BC), F32),
            pltpu.VMEM((SSD_SUB, SSM_BC), F32),
            pltpu.VMEM((SSD_SUB, V7X_LANES), F32),
            pltpu.VMEM((SSD_SUB, V7X_LANES), F32),
            pltpu.VMEM((SSD_SUB, SSM_D_INNER), BF16),
        ] * (tile // SSD_SUB) + [pltpu.VMEM(wxbc.shape, BF16), pltpu.VMEM(wz.shape, BF16)],
        compiler_params=_params(2),
        name="ssd_prompt",
    )(x, *consts)


def _attention(q, k_ref, v_ref):
    outs = []
    for h in range(ATTN_HEADS):
        cols = slice(h * ATTN_HEAD_DIM, (h + 1) * ATTN_HEAD_DIM)
        s = _mm_nt(q[:, cols].astype(BF16), k_ref[:, cols]) * (ATTN_HEAD_DIM ** -0.5)
        outs.append(_mm(_softmax_rows(s).astype(BF16), v_ref[:, cols]))
    return jnp.concatenate(outs, axis=1)


def _mix_kernel(x_ref, ybg_ref, kb_ref, vb_ref, nw_ref, wa_ref, wq_ref, wg_ref, cw_ref,
                wsc_ref, wao_ref, wmo_ref, x1_ref, cst_ref, hist_ref):
    t = pl.program_id(1)
    tile = x_ref.shape[0]

    @pl.when(t == 0)
    def _():
        hist_ref[...] = jnp.zeros_like(hist_ref)

    x = x_ref[...]
    xn = _rms(x, nw_ref[...]).astype(BF16)

    sc_b = _mm(xn, wa_ref[:, 0:SC_DIM])
    u = _mm(xn, wa_ref[:, SC_DIM:2 * SC_DIM]) * _mm(xn, wa_ref[:, 2 * SC_DIM:3 * SC_DIM])
    prev8 = hist_ref[...]
    hist_ref[...] = u[tile - V7X_SUBLANES:]
    cw = cw_ref[...]
    conv = _shift_rows(u, prev8, 2) * cw[0:1]
    conv = conv + _shift_rows(u, prev8, 1) * cw[1:2]
    conv = conv + u * cw[2:3]
    y_a = _mm((sc_b * conv).astype(BF16), wsc_ref[...])

    att = _attention(_mm(xn, wq_ref[...]), kb_ref, vb_ref)
    y_c = _mm(att.astype(BF16), wao_ref[...])

    g_a = _sigmoid(_mm(xn, wg_ref[:, 0:D_MODEL]))
    g_c = _sigmoid(_mm(xn, wg_ref[:, D_MODEL:2 * D_MODEL]))
    merged = g_a * y_a + ybg_ref[...] + g_c * y_c
    x1_ref[...] = x + _mm(merged.astype(BF16), wmo_ref[...])

    @pl.when(t == pl.num_programs(1) - 1)
    def _():
        cst_ref[...] = pltpu.roll(hist_ref[...], SC_WIDTH - 1, axis=0)[0:SC_WIDTH - 1]


def _mix_call(x, ybg, kb, vb, nw, wa, wq, wg, cw, wsc, wao, wmo):
    b, s, d = x.shape
    tile = MIX_TILE
    consts = (nw, wa, wq, wg, cw, wsc, wao, wmo)
    tok = pl.BlockSpec((None, tile, d), lambda i, j: (i, j, 0))
    mem = pl.BlockSpec((None, MEM_LEN, d), lambda i, j: (i, 0, 0))
    return pl.pallas_call(
        _mix_kernel,
        grid=(b, s // tile),
        in_specs=[tok, tok, mem, mem] + [_whole(c.shape) for c in consts],
        out_specs=[tok, pl.BlockSpec((None, SC_WIDTH - 1, SC_DIM), lambda i, j: (i, 0, 0))],
        out_shape=[
            jax.ShapeDtypeStruct((b, s, d), F32),
            jax.ShapeDtypeStruct((b, SC_WIDTH - 1, SC_DIM), F32),
        ],
        scratch_shapes=[pltpu.VMEM((V7X_SUBLANES, SC_DIM), F32)],
        compiler_params=_params(2),
        name="mix_prompt",
    )(x, ybg, kb, vb, *consts)


def _ffn_body(x, nw, wg_ref, wu_ref, wd_ref, fw):
    xn = _rms(x, nw).astype(BF16)
    h = _silu(_mm(xn, wg_ref[...])) * _mm(xn, wu_ref[...])
    x2 = x + _mm(h.astype(BF16), wd_ref[...])
    return _rms(x2, fw)


def _ffn_kernel(x_ref, nw_ref, wg_ref, wu_ref, wd_ref, fw_ref, o_ref):
    o_ref[...] = _ffn_body(x_ref[...], nw_ref[...], wg_ref, wu_ref, wd_ref, fw_ref[...])


def _ffn_call(x, nw, wg, wu, wd, fw):
    b, s, d = x.shape
    tile = FFN_TILE
    consts = (nw, wg, wu, wd, fw)
    tok = pl.BlockSpec((None, tile, d), lambda i, j: (i, j, 0))
    return pl.pallas_call(
        _ffn_kernel,
        grid=(b, s // tile),
        in_specs=[tok] + [_whole(c.shape) for c in consts],
        out_specs=tok,
        out_shape=jax.ShapeDtypeStruct((b, s, d), F32),
        compiler_params=_params(2),
        name="ffn_prompt",
    )(x, *consts)


def _sfront_kernel(x_ref, sconv_ref, ssmc_ref, nw_ref, wa_ref, wz_ref, wxbc_ref, wdt_ref,
                   wq_ref, wg_ref, scw_ref, cw_ref, cb_ref, dtb_ref, alog_ref, wsc_ref, e_ref,
                   gaya_ref, gb_ref, gc_ref, z_ref, xs_ref, dtxt_ref, dect_ref, b_ref, c_ref,
                   q_ref, sconv_o, ssmc_o):
    xn = _rms(x_ref[...], nw_ref[...]).astype(BF16)

    sc_b = _mm(xn, wa_ref[:, 0:SC_DIM])
    u = _mm(xn, wa_ref[:, SC_DIM:2 * SC_DIM]) * _mm(xn, wa_ref[:, 2 * SC_DIM:3 * SC_DIM])
    h0 = sconv_ref[:, 0:SC_DIM]
    h1 = sconv_ref[:, SC_DIM:2 * SC_DIM]
    scw = scw_ref[...]
    conv = h0 * scw[0:1] + h1 * scw[1:2] + u * scw[2:3]
    sconv_o[:, 0:SC_DIM] = h1
    sconv_o[:, SC_DIM:2 * SC_DIM] = u
    y_a = _mm((sc_b * conv).astype(BF16), wsc_ref[...])
    gaya_ref[...] = _sigmoid(_mm(xn, wg_ref[:, 0:D_MODEL])) * y_a
    gb_ref[...] = _sigmoid(_mm(xn, wg_ref[:, D_MODEL:2 * D_MODEL]))
    gc_ref[...] = _sigmoid(_mm(xn, wg_ref[:, 2 * D_MODEL:3 * D_MODEL]))
    z_ref[...] = _mm(xn, wz_ref[...])
    q_ref[...] = _mm(xn, wq_ref[...])

    dt = _softplus(_mm(xn, wdt_ref[...]) + dtb_ref[...])
    dect_ref[...] = jnp.exp(dt * (-jnp.exp(alog_ref[...]))).T
    dtexp = _mm_expand(dt, e_ref[...])

    for j in range(SSM_CONV_DIM // CONV_COLS):
        lo = j * CONV_COLS
        cols = slice(lo, lo + CONV_COLS)
        xbc = _mm(xn, wxbc_ref[:, cols])
        p0 = ssmc_ref[0, :, cols]
        p1 = ssmc_ref[1, :, cols]
        p2 = ssmc_ref[2, :, cols]
        cw = cw_ref[:, cols]
        conv = p0 * cw[0:1] + p1 * cw[1:2] + p2 * cw[2:3] + xbc * cw[3:4]
        ssmc_o[0, :, cols] = p1
        ssmc_o[1, :, cols] = p2
        ssmc_o[2, :, cols] = xbc
        act = _silu(conv + cb_ref[:, cols])
        if lo < SSM_D_INNER:
            xs_ref[:, cols] = act
            dtx = act * dtexp[:, cols]
            for k in range(CONV_COLS // V7X_LANES):
                r0 = lo + k * V7X_LANES
                dtxt_ref[r0:r0 + V7X_LANES, :] = dtx[:, k * V7X_LANES:(k + 1) * V7X_LANES].T
        elif lo < SSM_D_INNER + SSM_BC:
            b_ref[:, lo - SSM_D_INNER:lo - SSM_D_INNER + CONV_COLS] = act
        else:
            off = lo - SSM_D_INNER - SSM_BC
            c_ref[:, off:off + CONV_COLS] = act


def _sfront_call(x, sconv, ssmc, nw, wa, wz, wxbc, wdt, wq, wg, scw, cw, cb, dtb, alog, wsc, e):
    n = x.shape[0]
    args = (x, sconv, ssmc, nw, wa, wz, wxbc, wdt, wq, wg, scw, cw, cb, dtb, alog, wsc, e)
    shapes = [
        (n, D_MODEL), (n, D_MODEL), (n, D_MODEL),
        (n, SSM_D_INNER), (n, SSM_D_INNER),
        (SSM_D_INNER, n), (V7X_LANES, n),
        (n, SSM_BC), (n, SSM_BC),
        (n, D_MODEL),
        sconv.shape, ssmc.shape,
    ]
    return pl.pallas_call(
        _sfront_kernel,
        grid=(1,),
        in_specs=[_whole(a.shape) for a in args],
        out_specs=[_whole(s) for s in shapes],
        out_shape=[jax.ShapeDtypeStruct(s, F32) for s in shapes],
        compiler_params=_params(1),
        name="sample_front",
    )(*args)


def _sssm_kernel(s_ref, dtxt_ref, dect_ref, b_ref, c_ref, so_ref, y_ref):
    i = pl.program_id(0)
    n = dtxt_ref.shape[1]
    lane = lax.broadcasted_iota(jnp.int32, (1, n), 1)
    sub = lax.broadcasted_iota(jnp.int32, (V7X_SUBLANES, SSM_STATE), 0)
    for k in range(SAMPLE_BLOCK):
        onehot = (lane == i * SAMPLE_BLOCK + k).astype(F32)
        xcol = jnp.sum(dtxt_ref[...] * onehot, axis=-1, keepdims=True)
        dcol = jnp.sum(dect_ref[0:SSM_HEADS, :] * onehot, axis=-1, keepdims=True)
        for h in range(SSM_HEADS):
            g = h // HEADS_PER_GROUP
            rows = slice(h * SSM_HEAD_DIM, (h + 1) * SSM_HEAD_DIM)
            brow = b_ref[k:k + 1, g * SSM_STATE:(g + 1) * SSM_STATE]
            so_ref[k, rows, :] = s_ref[k, rows, :] * dcol[h:h + 1, :] + xcol[rows, :] * brow
        c8 = jnp.zeros((V7X_SUBLANES, SSM_STATE), F32)
        for g in range(SSM_GROUPS):
            c8 = jnp.where(sub == g, c_ref[k:k + 1, g * SSM_STATE:(g + 1) * SSM_STATE], c8)
        y8 = _mm_nt(c8.astype(BF16), so_ref[k].astype(BF16))
        y_ref[k:k + 1, :] = jnp.concatenate(
            [y8[g:g + 1, g * GROUP_WIDTH:(g + 1) * GROUP_WIDTH] for g in range(SSM_GROUPS)], axis=1)


def _sssm_call(state, dtxt, dect, bm, cm):
    n = state.shape[0]
    blk = SAMPLE_BLOCK
    st = pl.BlockSpec((blk, SSM_D_INNER, SSM_STATE), lambda i: (i, 0, 0))
    return pl.pallas_call(
        _sssm_kernel,
        grid=(n // blk,),
        in_specs=[st, _whole(dtxt.shape), _whole(dect.shape),
                  pl.BlockSpec((blk, SSM_BC), lambda i: (i, 0)),
                  pl.BlockSpec((blk, SSM_BC), lambda i: (i, 0))],
        out_specs=[st, pl.BlockSpec((blk, SSM_D_INNER), lambda i: (i, 0))],
        out_shape=[jax.ShapeDtypeStruct(state.shape, F32),
                   jax.ShapeDtypeStruct((n, SSM_D_INNER), F32)],
        compiler_params=_params(1),
        name="sample_ssm",
    )(state, dtxt, dect, bm, cm)


def _cache_rows(c):
    n = c.shape[1]
    c = c.reshape(n, MEM_LEN, ATTN_HEADS, ATTN_HEAD_DIM // V7X_LANES, V7X_LANES)
    return c.transpose(0, 1, 3, 2, 4).reshape(n * MEM_LEN * CACHE_ROWS, V7X_LANES)


def _cache_head(ref, k, h):
    base = k * MEM_LEN * CACHE_ROWS
    halves = [ref[pl.ds(base + j * ATTN_HEADS + h, MEM_LEN, stride=CACHE_ROWS), :]
              for j in range(ATTN_HEAD_DIM // V7X_LANES)]
    return jnp.concatenate(halves, axis=1).astype(BF16)


def _sattn_kernel(q_ref, k_ref, v_ref, o_ref):
    sub = lax.broadcasted_iota(jnp.int32, (SAMPLE_BLOCK, MEM_LEN), 0)
    sub_o = lax.broadcasted_iota(jnp.int32, (SAMPLE_BLOCK, ATTN_HEAD_DIM), 0)
    outs = []
    for h in range(ATTN_HEADS):
        qh = q_ref[:, h * ATTN_HEAD_DIM:(h + 1) * ATTN_HEAD_DIM].astype(BF16)
        s = jnp.zeros((SAMPLE_BLOCK, MEM_LEN), F32)
        for k in range(SAMPLE_BLOCK):
            s = jnp.where(sub == k, _mm_nt(qh, _cache_head(k_ref, k, h)), s)
        p = _softmax_rows(s * (ATTN_HEAD_DIM ** -0.5)).astype(BF16)
        o = jnp.zeros((SAMPLE_BLOCK, ATTN_HEAD_DIM), F32)
        for k in range(SAMPLE_BLOCK):
            o = jnp.where(sub_o == k, _mm(p, _cache_head(v_ref, k, h)), o)
        outs.append(o)
    o_ref[...] = jnp.concatenate(outs, axis=1)


def _sattn_call(q, ck, cv):
    n = q.shape[0]
    blk = SAMPLE_BLOCK
    kv = pl.BlockSpec((blk * MEM_LEN * CACHE_ROWS, V7X_LANES), lambda i: (i, 0))
    row = pl.BlockSpec((blk, D_MODEL), lambda i: (i, 0))
    return pl.pallas_call(
        _sattn_kernel,
        grid=(n // blk,),
        in_specs=[row, kv, kv],
        out_specs=row,
        out_shape=jax.ShapeDtypeStruct((n, D_MODEL), F32),
        compiler_params=_params(1),
        name="sample_attn",
    )(q, _cache_rows(ck), _cache_rows(cv))


def _sback_kernel(x_ref, y_ref, xs_ref, z_ref, gaya_ref, gb_ref, gc_ref, att_ref, dful_ref,
                  gnw_ref, wout_ref, wao_ref, wmo_ref, fnw_ref, wg_ref, wu_ref, wd_ref, fw_ref,
                  o_ref):
    y = (y_ref[...] + dful_ref[...] * xs_ref[...]) * _silu(z_ref[...])
    y_b = _mm(_group_norm(y, gnw_ref[...]).astype(BF16), wout_ref[...])
    y_c = _mm(att_ref[...].astype(BF16), wao_ref[...])
    merged = gaya_ref[...] + gb_ref[...] * y_b + gc_ref[...] * y_c
    x1 = x_ref[...] + _mm(merged.astype(BF16), wmo_ref[...])
    o_ref[...] = _ffn_body(x1, fnw_ref[...], wg_ref, wu_ref, wd_ref, fw_ref[...])


def _sback_call(*args):
    n = args[0].shape[0]
    return pl.pallas_call(
        _sback_kernel,
        grid=(1,),
        in_specs=[_whole(a.shape) for a in args],
        out_specs=_whole((n, D_MODEL)),
        out_shape=jax.ShapeDtypeStruct((n, D_MODEL), F32),
        compiler_params=_params(1),
        name="sample_back",
    )(*args)


def kernel(x_prompt, x_sample, mem_prompt, cache_mem_k, cache_mem_v, state_conv, state_ssm_conv, state_ssm, norm_mix_w, w_in, sc_conv_w, w_sc_out, ssm_conv_w, ssm_conv_b, ssm_dt_bias, ssm_a_log, ssm_d, ssm_norm_w, w_ssm_out, norm_mem_w, w_mem_k, w_mem_v, w_attn_o, w_merge_o, norm_ffn_w, w_ffn_gate, w_ffn_up, w_ffn_down, norm_final_w):
    depth = w_in.shape[0]
    assert depth == 1
    bp = x_prompt.shape[0]
    ns = x_sample.shape[0]

    wi = w_in[0]
    o = 0
    wa = wi[:, o:o + 3 * SC_DIM].astype(BF16); o += 3 * SC_DIM
    wz = wi[:, o:o + SSM_D_INNER].astype(BF16); o += SSM_D_INNER
    wxbc = wi[:, o:o + SSM_CONV_DIM].astype(BF16); o += SSM_CONV_DIM
    wdt = jnp.pad(wi[:, o:o + SSM_HEADS], ((0, 0), (0, V7X_LANES - SSM_HEADS))).astype(BF16); o += SSM_HEADS
    wq = wi[:, o:o + D_MODEL].astype(BF16); o += D_MODEL
    wg = wi[:, o:o + 3 * D_MODEL].astype(BF16)
    wg_ac = jnp.concatenate([wg[:, 0:D_MODEL], wg[:, 2 * D_MODEL:]], axis=1)
    wg_b = wg[:, D_MODEL:2 * D_MODEL]
    row = lambda v: v.reshape(1, -1).astype(F32)
    pad_heads = lambda v: jnp.pad(row(v), ((0, 0), (0, V7X_LANES - SSM_HEADS)))
    nmix = row(norm_mix_w[0])
    dtb = pad_heads(ssm_dt_bias[0])
    alog = pad_heads(ssm_a_log[0])
    dful = row(jnp.repeat(ssm_d[0], SSM_HEAD_DIM))
    gnw = row(ssm_norm_w[0])
    cb = row(ssm_conv_b[0])
    cw = ssm_conv_w[0]
    scw = sc_conv_w[0]
    wsc = w_sc_out[0].astype(BF16)
    wout = w_ssm_out[0].astype(BF16)
    wao = w_attn_o[0].astype(BF16)
    wmo = w_merge_o[0].astype(BF16)
    wfg = w_ffn_gate[0].astype(BF16)
    wfu = w_ffn_up[0].astype(BF16)
    wfd = w_ffn_down[0].astype(BF16)
    nffn = row(norm_ffn_w[0])
    nfin = row(norm_final_w)
    expand = (jnp.arange(V7X_LANES)[:, None] == (jnp.arange(SSM_D_INNER)[None, :] // SSM_HEAD_DIM)).astype(BF16)

    mk, mv, mkb, mvb = _mem_call(mem_prompt, row(norm_mem_w[0]), w_mem_k[0].astype(BF16), w_mem_v[0].astype(BF16))
    ybg, p_ssmc, p_ssm = _ssd_call(x_prompt, nmix, wz, wxbc, wdt, wg_b, cw, cb, dtb, alog, dful, gnw, wout, expand)
    x1, p_conv = _mix_call(x_prompt, ybg, mkb, mvb, nmix, wa, wq, wg_ac, scw, wsc, wao, wmo)
    y_prompt = _ffn_call(x1, nffn, wfg, wfu, wfd, nfin)

    xs2 = x_sample.reshape(ns, D_MODEL)
    (gaya, gb, gc, z, xs, dtxt, dect, bm, cm, q, s_conv, s_ssmc) = _sfront_call(
        xs2, state_conv[0].reshape(ns, -1), jnp.swapaxes(state_ssm_conv[0], 0, 1),
        nmix, wa, wz, wxbc, wdt, wq, wg, scw, cw, cb, dtb, alog, wsc, expand)
    s_ssm, y_s = _sssm_call(state_ssm[0].reshape(ns, SSM_D_INNER, SSM_STATE), dtxt, dect, bm, cm)
    att = _sattn_call(q, cache_mem_k, cache_mem_v)
    y_sample = _sback_call(xs2, y_s, xs, z, gaya, gb, gc, att, dful, gnw, wout, wao, wmo,
                           nffn, wfg, wfu, wfd, nfin)

    def from_rows(r):
        r = r.reshape(bp, MEM_LEN, ATTN_HEAD_DIM // V7X_LANES, ATTN_HEADS, V7X_LANES)
        return r.transpose(0, 1, 3, 2, 4).reshape(depth, bp, MEM_LEN, ATTN_HEADS, ATTN_HEAD_DIM)

    state_shape = (SSM_HEADS, SSM_HEAD_DIM, SSM_STATE)
    return (
        y_prompt,
        y_sample.reshape(ns, 1, D_MODEL),
        from_rows(mk),
        from_rows(mv),
        p_conv.reshape(depth, bp, SC_WIDTH - 1, SC_DIM),
        p_ssmc.reshape(depth, bp, SSM_CONV - 1, SSM_CONV_DIM),
        p_ssm.reshape((depth, bp) + state_shape),
        s_conv.reshape(depth, ns, SC_WIDTH - 1, SC_DIM),
        jnp.swapaxes(s_ssmc, 0, 1).reshape(depth, ns, SSM_CONV - 1, SSM_CONV_DIM),
        s_ssm.reshape((depth, ns) + state_shape),
    )
```

```python
import jax
import jax.numpy as jnp
from jax import lax
from jax.experimental import pallas as pl
from jax.experimental.pallas import tpu as pltpu

F32 = jnp.float32
BF16 = jnp.bfloat16

D_MODEL = 1024
RMS_EPS = 1e-6
LOG2_E = 1.4426950408889634
SC_DIM = D_MODEL
SC_WIDTH = 3
SSM_D_INNER = 2 * D_MODEL
SSM_HEAD_DIM = 64
SSM_HEADS = SSM_D_INNER // SSM_HEAD_DIM
SSM_STATE = 128
SSM_GROUPS = 4
SSM_CONV = 4
SSM_CHUNK = 128
SSM_BC = SSM_GROUPS * SSM_STATE
SSM_CONV_DIM = SSM_D_INNER + 2 * SSM_BC
HEADS_PER_GROUP = SSM_HEADS // SSM_GROUPS
GROUP_WIDTH = SSM_D_INNER // SSM_GROUPS
MEM_LEN = 256
ATTN_HEADS = 4
ATTN_HEAD_DIM = D_MODEL // ATTN_HEADS
FFN_HIDDEN = ((8 * D_MODEL // 3 + 255) // 256) * 256

V7X_LANES = 128
V7X_SUBLANES = 8
V7X_VMEM_BYTES = 64 * 1024 * 1024
VMEM_LIMIT_BYTES = V7X_VMEM_BYTES - 8 * 1024 * 1024
CACHE_ROWS = ATTN_HEADS * ATTN_HEAD_DIM // V7X_LANES

SSD_TILE = 512
SSD_SUB = 256
MIX_TILE = 512
FFN_TILE = 512
SAMPLE_BLOCK = 8
CONV_COLS = 512


def _params(n_grid, flags=None):
    return pltpu.CompilerParams(
        dimension_semantics=("arbitrary",) * n_grid,
        vmem_limit_bytes=VMEM_LIMIT_BYTES,
        flags=flags,
    )


def _whole(shape):
    nd = len(shape)
    return pl.BlockSpec(shape, lambda *_: (0,) * nd)


def _mm(a, b):
    return jnp.dot(a, b, preferred_element_type=F32)


def _mm_nt(a, b):
    return lax.dot_general(a, b, (((1,), (1,)), ((), ())), preferred_element_type=F32)


def _split3(x):
    hi = x.astype(BF16)
    r = x - hi.astype(F32)
    mid = r.astype(BF16)
    lo = (r - mid.astype(F32)).astype(BF16)
    return hi, mid, lo


def _mm_sel(sel, x):
    hi, mid, lo = _split3(x)
    return _mm(sel, hi) + _mm(sel, mid) + _mm(sel, lo)


def _mm_expand(x, sel):
    hi, mid, lo = _split3(x)
    return _mm(hi, sel) + _mm(mid, sel) + _mm(lo, sel)


def _rms(x, w):
    return x * lax.rsqrt(jnp.mean(x * x, axis=-1, keepdims=True) + RMS_EPS) * w


def _sigmoid(x):
    return 1.0 / (1.0 + jnp.exp2(x * (-LOG2_E)))


def _silu(x):
    return x * _sigmoid(x)


def _softplus(x):
    return jnp.maximum(x, 0.0) + jnp.log1p(jnp.exp(-jnp.abs(x)))


def _shift_rows(u, prev8, k):
    r = pltpu.roll(u, k, axis=0)
    p = pltpu.roll(prev8, k, axis=0)
    row = lax.broadcasted_iota(jnp.int32, prev8.shape, 0)
    head = jnp.where(row < k, p, r[:V7X_SUBLANES])
    return jnp.concatenate([head, r[V7X_SUBLANES:]], axis=0)


def _softmax_rows(s):
    m = jnp.max(s, axis=-1, keepdims=True)
    p = jnp.exp(s - m)
    return p / jnp.sum(p, axis=-1, keepdims=True)


def _group_norm(y, w):
    outs = []
    for g in range(SSM_GROUPS):
        cols = slice(g * GROUP_WIDTH, (g + 1) * GROUP_WIDTH)
        yg = y[:, cols]
        ms = jnp.mean(yg * yg, axis=-1, keepdims=True)
        outs.append(yg * lax.rsqrt(ms + RMS_EPS) * w[:, cols])
    return jnp.concatenate(outs, axis=1)


def _mem_kernel(m_ref, nw_ref, wk_ref, wv_ref, k_ref, v_ref, kb_ref, vb_ref):
    mn = _rms(m_ref[...], nw_ref[...]).astype(BF16)
    k = _mm(mn, wk_ref[...])
    v = _mm(mn, wv_ref[...])
    kb_ref[...] = k.astype(BF16)
    vb_ref[...] = v.astype(BF16)
    for j in range(ATTN_HEAD_DIM // V7X_LANES):
        for h in range(ATTN_HEADS):
            rows = pl.ds(j * ATTN_HEADS + h, MEM_LEN, stride=CACHE_ROWS)
            lo = h * ATTN_HEAD_DIM + j * V7X_LANES
            k_ref[rows, :] = k[:, lo:lo + V7X_LANES]
            v_ref[rows, :] = v[:, lo:lo + V7X_LANES]


def _mem_call(mem, nw, wk, wv):
    b = mem.shape[0]
    blk = pl.BlockSpec((None, MEM_LEN, D_MODEL), lambda i: (i, 0, 0))
    rows = pl.BlockSpec((None, MEM_LEN * CACHE_ROWS, V7X_LANES), lambda i: (i, 0, 0))
    rows_shape = jax.ShapeDtypeStruct((b, MEM_LEN * CACHE_ROWS, V7X_LANES), F32)
    return pl.pallas_call(
        _mem_kernel,
        grid=(b,),
        in_specs=[blk, _whole(nw.shape), _whole(wk.shape), _whole(wv.shape)],
        out_specs=[rows, rows, blk, blk],
        out_shape=[
            rows_shape,
            rows_shape,
            jax.ShapeDtypeStruct(mem.shape, BF16),
            jax.ShapeDtypeStruct(mem.shape, BF16),
        ],
        compiler_params=_params(1),
        name="mem_kv",
    )(mem, nw, wk, wv)


def _ssd_kernel(x_ref, nw_ref, wz_ref, wxbc_ref, wdt_ref, wgb_ref, cw_ref, cb_ref,
                dtb_ref, alog_ref, dful_ref, gnw_ref, wout_ref, e_ref,
                ybg_ref, cst_ref, sst_ref,
                hist_ref, st_ref, *scratch):
    t = pl.program_id(1)
    tile = x_ref.shape[0]
    q = SSM_CHUNK
    n_sub = tile // SSD_SUB
    sets = [scratch[k * 8:(k + 1) * 8] for k in range(n_sub)]

    @pl.when(t == 0)
    def _():
        hist_ref[...] = jnp.zeros_like(hist_ref)
        st_ref[...] = jnp.zeros_like(st_ref)

    def project(sub):
        xn_s, z_s, xs_s, b_s, c_s, dt_s, da_s, _ = sets[sub]
        xn_s[...] = _rms(x_ref[sub * SSD_SUB:(sub + 1) * SSD_SUB, :], nw_ref[...]).astype(BF16)
        dt = _softplus(_mm(xn_s[...], wdt_ref[...]) + dtb_ref[...])
        dt_s[...] = dt
        da_s[...] = dt * (-jnp.exp(alog_ref[...]))
        n_blocks = SSM_CONV_DIM // CONV_COLS
        first_bc = SSM_D_INNER // CONV_COLS
        for j in list(range(first_bc, n_blocks)) + list(range(first_bc)):
            cols = slice(j * CONV_COLS, (j + 1) * CONV_COLS)
            u = _mm(xn_s[...], wxbc_ref[:, cols])
            prev8 = hist_ref[:, cols]
            hist_ref[:, cols] = u[SSD_SUB - V7X_SUBLANES:]
            cw = cw_ref[:, cols]
            conv = _shift_rows(u, prev8, 3) * cw[0:1]
            conv = conv + _shift_rows(u, prev8, 2) * cw[1:2]
            conv = conv + _shift_rows(u, prev8, 1) * cw[2:3]
            conv = conv + u * cw[3:4]
            act = _silu(conv + cb_ref[:, cols])
            lo = j * CONV_COLS
            if lo < SSM_D_INNER:
                xs_s[:, cols] = act
            elif lo < SSM_D_INNER + SSM_BC:
                b_s[:, lo - SSM_D_INNER:lo - SSM_D_INNER + CONV_COLS] = act
            else:
                off = lo - SSM_D_INNER - SSM_BC
                c_s[:, off:off + CONV_COLS] = act
        z_s[...] = _silu(_mm(xn_s[...], wz_ref[...]))

    ri = lax.broadcasted_iota(jnp.int32, (q, q), 0)
    ci = lax.broadcasted_iota(jnp.int32, (q, q), 1)
    causal = ri >= ci
    tri = jnp.where(causal, 1.0, 0.0).astype(BF16)
    lane_lo = ci < SSM_HEAD_DIM
    keep_lo = jnp.where(lane_lo, 1.0, 0.0).astype(BF16)
    keep_hi = jnp.where(lane_lo, 0.0, 1.0).astype(BF16)
    sub8 =lax.broadcasted_iota(jnp.int32, (V7X_SUBLANES, q), 0)

    groups = range(SSM_GROUPS)
    gsl = [slice(g * GROUP_WIDTH, (g + 1) * GROUP_WIDTH) for g in groups]
    nsl = [slice(g * SSM_STATE, (g + 1) * SSM_STATE) for g in groups]

    def scan(sub, chunks):
        _, z_s, xs_s, b_s, c_s, dt_s, da_s, yn_s = sets[sub]
        rows = {c: pl.ds(c * q, q) for c in chunks}

        acum, row_t, w, cd = {}, {}, {}, {}
        for c in chunks:
            a = _mm_sel(tri, da_s[rows[c], :])
            last = a[q - 1:q, :]
            dtc = dt_s[rows[c], :]
            acum[c] = a
            row_t[c] = a.T - jnp.log(dtc.T)
            w[c] = (dtc * jnp.exp(last - a)).astype(BF16)
            cd[c] = jnp.where(sub8 == 0, jnp.exp(last), 0.0)

        bg = {(c, g): b_s[rows[c], nsl[g]] for c in chunks for g in groups}
        cg = {(c, g): c_s[rows[c], nsl[g]].astype(BF16) for c in chunks for g in groups}
        cb = {k: _mm_nt(cg[k], bg[k].astype(BF16)).astype(BF16) for k in bg}
        wx = {(c, g): _mm(w[c], e_ref[:, gsl[g]]) for c in chunks for g in groups}
        dec = {(c, g): _mm_expand(cd[c], e_ref[:, gsl[g]])[0:1, :] for c in chunks for g in groups}

        yo = {}
        for c in chunks:
            st = [st_ref[:, gsl[g]] for g in groups]
            for g in groups:
                yo[c, g] = _mm(cg[c, g], st[g].astype(BF16))
            for g in groups:
                xw = (xs_s[rows[c], gsl[g]] * wx[c, g]).astype(BF16)
                st_ref[:, gsl[g]] = st[g] * dec[c, g] + _mm(bg[c, g].T.astype(BF16), xw)

        yd = {}
        for c in chunks:
            for g in groups:
                for pq in range(HEADS_PER_GROUP // 2):
                    h0 = g * HEADS_PER_GROUP + 2 * pq
                    h1 = h0 + 1
                    l0 = jnp.where(causal, jnp.exp(acum[c][:, h0:h0 + 1] - row_t[c][h0:h0 + 1, :]), 0.0)
                    l1 = jnp.where(causal, jnp.exp(acum[c][:, h1:h1 + 1] - row_t[c][h1:h1 + 1, :]), 0.0)
                    lhs = jnp.concatenate([cb[c, g] * l0.astype(BF16), cb[c, g] * l1.astype(BF16)], axis=1)
                    xpb = xs_s[rows[c], h0 * SSM_HEAD_DIM:(h1 + 1) * SSM_HEAD_DIM].astype(BF16)
                    rhs = jnp.concatenate([xpb * keep_lo, xpb * keep_hi], axis=0)
                    yd[c, g, pq] = _mm(lhs, rhs)

        for c in chunks:
            for g in groups:
                pairs = []
                for pq in range(HEADS_PER_GROUP // 2):
                    h0 = g * HEADS_PER_GROUP + 2 * pq
                    h1 = h0 + 1
                    pcols = slice(h0 * SSM_HEAD_DIM, (h1 + 1) * SSM_HEAD_DIM)
                    sc = jnp.where(lane_lo, jnp.exp(acum[c][:, h0:h0 + 1]), jnp.exp(acum[c][:, h1:h1 + 1]))
                    lc = slice(2 * pq * SSM_HEAD_DIM, (2 * pq + 2) * SSM_HEAD_DIM)
                    pairs.append(yd[c, g, pq] + sc * yo[c, g][:, lc] + dful_ref[:, pcols] * xs_s[rows[c], pcols])
                yg = jnp.concatenate(pairs, axis=1) * z_s[rows[c], gsl[g]]
                ms = jnp.mean(yg * yg, axis=-1, keepdims=True)
                yn_s[rows[c], gsl[g]] = (yg * lax.rsqrt(ms + RMS_EPS) * gnw_ref[:, gsl[g]]).astype(BF16)

    for sub in range(n_sub):
        project(sub)
    for sub in range(n_sub):
        for c in range(SSD_SUB // q):
            scan(sub, [c])
        xn_s, yn_s = sets[sub][0], sets[sub][7]
        gb = _sigmoid(_mm(xn_s[...], wgb_ref[...]))
        ybg_ref[sub * SSD_SUB:(sub + 1) * SSD_SUB, :] = gb * _mm(yn_s[...], wout_ref[...])

    @pl.when(t == pl.num_programs(1) - 1)
    def _():
        cst_ref[...] = pltpu.roll(hist_ref[...], SSM_CONV - 1, axis=0)[0:SSM_CONV - 1]
        for k in range(SSM_D_INNER // V7X_LANES):
            blk = slice(k * V7X_LANES, (k + 1) * V7X_LANES)
            sst_ref[blk, :] = st_ref[:, blk].T


def _ssd_call(x, nw, wz, wxbc, wdt, wgb, cw, cb, dtb, alog, dful, gnw, wout, e):
    b, s, d = x.shape
    tile = SSD_TILE
    consts = (nw, wz, wxbc, wdt, wgb, cw, cb, dtb, alog, dful, gnw, wout, e)
    return pl.pallas_call(
        _ssd_kernel,
        grid=(b, s // tile),
        in_specs=[pl.BlockSpec((None, tile, d), lambda i, j: (i, j, 0))]
        + [_whole(c.shape) for c in consts],
        out_specs=[
            pl.BlockSpec((None, tile, d), lambda i, j: (i, j, 0)),
            pl.BlockSpec((None, SSM_CONV - 1, SSM_CONV_DIM), lambda i, j: (i, 0, 0)),
            pl.BlockSpec((None, SSM_D_INNER, SSM_STATE), lambda i, j: (i, 0, 0)),
        ],
        out_shape=[
            jax.ShapeDtypeStruct((b, s, d), F32),
            jax.ShapeDtypeStruct((b, SSM_CONV - 1, SSM_CONV_DIM), F32),
            jax.ShapeDtypeStruct((b, SSM_D_INNER, SSM_STATE), F32),
        ],
        scratch_shapes=[
            pltpu.VMEM((V7X_SUBLANES, SSM_CONV_DIM), F32),
            pltpu.VMEM((SSM_STATE, SSM_D_INNER), F32),
        ] + [
            pltpu.VMEM((SSD_SUB, d), BF16),
            pltpu.VMEM((SSD_SUB, SSM_D_INNER), F32),
            pltpu.VMEM((SSD_SUB, SSM_D_INNER), F32),
            pltpu.VMEM((SSD_SUB, SSM_BC), F32),
            pltpu.VMEM((SSD_SUB, SSM_BC), F32),
            pltpu.VMEM((SSD_SUB, V7X_LANES), F32),
            pltpu.VMEM((SSD_SUB, V7X_LANES), F32),
            pltpu.VMEM((SSD_SUB, SSM_D_INNER), BF16),
        ] * (tile // SSD_SUB),
        compiler_params=_params(2),
        name="ssd_prompt",
    )(x, *consts)


def _attention(q, k_ref, v_ref):
    outs = []
    for h in range(ATTN_HEADS):
        cols = slice(h * ATTN_HEAD_DIM, (h + 1) * ATTN_HEAD_DIM)
        s = _mm_nt(q[:, cols].astype(BF16), k_ref[:, cols]) * (ATTN_HEAD_DIM ** -0.5)
        outs.append(_mm(_softmax_rows(s).astype(BF16), v_ref[:, cols]))
    return jnp.concatenate(outs, axis=1)


def _mix_kernel(x_ref, ybg_ref, kb_ref, vb_ref, nw_ref, wa_ref, wq_ref, wg_ref, cw_ref,
                wsc_ref, wao_ref, wmo_ref, x1_ref, cst_ref, hist_ref):
    t = pl.program_id(1)
    tile = x_ref.shape[0]

    @pl.when(t == 0)
    def _():
        hist_ref[...] = jnp.zeros_like(hist_ref)

    x = x_ref[...]
    xn = _rms(x, nw_ref[...]).astype(BF16)

    sc_b = _mm(xn, wa_ref[:, 0:SC_DIM])
    u = _mm(xn, wa_ref[:, SC_DIM:2 * SC_DIM]) * _mm(xn, wa_ref[:, 2 * SC_DIM:3 * SC_DIM])
    prev8 = hist_ref[...]
    hist_ref[...] = u[tile - V7X_SUBLANES:]
    cw = cw_ref[...]
    conv = _shift_rows(u, prev8, 2) * cw[0:1]
    conv = conv + _shift_rows(u, prev8, 1) * cw[1:2]
    conv = conv + u * cw[2:3]
    y_a = _mm((sc_b * conv).astype(BF16), wsc_ref[...])

    att = _attention(_mm(xn, wq_ref[...]), kb_ref, vb_ref)
    y_c = _mm(att.astype(BF16), wao_ref[...])

    g_a = _sigmoid(_mm(xn, wg_ref[:, 0:D_MODEL]))
    g_c = _sigmoid(_mm(xn, wg_ref[:, D_MODEL:2 * D_MODEL]))
    merged = g_a * y_a + ybg_ref[...] + g_c * y_c
    x1_ref[...] = x + _mm(merged.astype(BF16), wmo_ref[...])

    @pl.when(t == pl.num_programs(1) - 1)
    def _():
        cst_ref[...] = pltpu.roll(hist_ref[...], SC_WIDTH - 1, axis=0)[0:SC_WIDTH - 1]


def _mix_call(x, ybg, kb, vb, nw, wa, wq, wg, cw, wsc, wao, wmo):
    b, s, d = x.shape
    tile = MIX_TILE
    consts = (nw, wa, wq, wg, cw, wsc, wao, wmo)
    tok = pl.BlockSpec((None, tile, d), lambda i, j: (i, j, 0))
    mem = pl.BlockSpec((None, MEM_LEN, d), lambda i, j: (i, 0, 0))
    return pl.pallas_call(
        _mix_kernel,
        grid=(b, s // tile),
        in_specs=[tok, tok, mem, mem] + [_whole(c.shape) for c in consts],
        out_specs=[tok, pl.BlockSpec((None, SC_WIDTH - 1, SC_DIM), lambda i, j: (i, 0, 0))],
        out_shape=[
            jax.ShapeDtypeStruct((b, s, d), F32),
            jax.ShapeDtypeStruct((b, SC_WIDTH - 1, SC_DIM), F32),
        ],
        scratch_shapes=[pltpu.VMEM((V7X_SUBLANES, SC_DIM), F32)],
        compiler_params=_params(2),
        name="mix_prompt",
    )(x, ybg, kb, vb, *consts)


def _ffn_body(x, nw, wg_ref, wu_ref, wd_ref, fw):
    xn = _rms(x, nw).astype(BF16)
    half = FFN_HIDDEN // 2
    x2 = x
    for lo in (0, half):
        h = _silu(_mm(xn, wg_ref[:, lo:lo + half])) * _mm(xn, wu_ref[:, lo:lo + half])
        x2 = x2 + _mm(h.astype(BF16), wd_ref[lo:lo + half, :])
    return _rms(x2, fw)


def _ffn_kernel(x_ref, nw_ref, wg_ref, wu_ref, wd_ref, fw_ref, o_ref):
    o_ref[...] = _ffn_body(x_ref[...], nw_ref[...], wg_ref, wu_ref, wd_ref, fw_ref[...])


def _ffn_call(x, nw, wg, wu, wd, fw):
    b, s, d = x.shape
    tile = FFN_TILE
    consts = (nw, wg, wu, wd, fw)
    tok = pl.BlockSpec((None, tile, d), lambda i, j: (i, j, 0))
    return pl.pallas_call(
        _ffn_kernel,
        grid=(b, s // tile),
        in_specs=[tok] + [_whole(c.shape) for c in consts],
        out_specs=tok,
        out_shape=jax.ShapeDtypeStruct((b, s, d), F32),
        compiler_params=_params(2),
        name="ffn_prompt",
    )(x, *consts)


def _sfront_kernel(x_ref, sconv_ref, ssmc_ref, nw_ref, wa_ref, wz_ref, wxbc_ref, wdt_ref,
                   wq_ref, wg_ref, scw_ref, cw_ref, cb_ref, dtb_ref, alog_ref, wsc_ref, e_ref,
                   gaya_ref, gb_ref, gc_ref, z_ref, xs_ref, dtxt_ref, dect_ref, b_ref, c_ref,
                   q_ref, sconv_o, ssmc_o):
    xn = _rms(x_ref[...], nw_ref[...]).astype(BF16)

    sc_b = _mm(xn, wa_ref[:, 0:SC_DIM])
    u = _mm(xn, wa_ref[:, SC_DIM:2 * SC_DIM]) * _mm(xn, wa_ref[:, 2 * SC_DIM:3 * SC_DIM])
    h0 = sconv_ref[:, 0:SC_DIM]
    h1 = sconv_ref[:, SC_DIM:2 * SC_DIM]
    scw = scw_ref[...]
    conv = h0 * scw[0:1] + h1 * scw[1:2] + u * scw[2:3]
    sconv_o[:, 0:SC_DIM] = h1
    sconv_o[:, SC_DIM:2 * SC_DIM] = u
    y_a = _mm((sc_b * conv).astype(BF16), wsc_ref[...])
    gaya_ref[...] = _sigmoid(_mm(xn, wg_ref[:, 0:D_MODEL])) * y_a
    gb_ref[...] = _sigmoid(_mm(xn, wg_ref[:, D_MODEL:2 * D_MODEL]))
    gc_ref[...] = _sigmoid(_mm(xn, wg_ref[:, 2 * D_MODEL:3 * D_MODEL]))
    z_ref[...] = _mm(xn, wz_ref[...])
    q_ref[...] = _mm(xn, wq_ref[...])

    dt = _softplus(_mm(xn, wdt_ref[...]) + dtb_ref[...])
    dect_ref[...] = jnp.exp(dt * (-jnp.exp(alog_ref[...]))).T
    dtexp = _mm_expand(dt, e_ref[...])

    for j in range(SSM_CONV_DIM // CONV_COLS):
        lo = j * CONV_COLS
        cols = slice(lo, lo + CONV_COLS)
        xbc = _mm(xn, wxbc_ref[:, cols])
        p0 = ssmc_ref[0, :, cols]
        p1 = ssmc_ref[1, :, cols]
        p2 = ssmc_ref[2, :, cols]
        cw = cw_ref[:, cols]
        conv = p0 * cw[0:1] + p1 * cw[1:2] + p2 * cw[2:3] + xbc * cw[3:4]
        ssmc_o[0, :, cols] = p1
        ssmc_o[1, :, cols] = p2
        ssmc_o[2, :, cols] = xbc
        act = _silu(conv + cb_ref[:, cols])
        if lo < SSM_D_INNER:
            xs_ref[:, cols] = act
            dtx = act * dtexp[:, cols]
            for k in range(CONV_COLS // V7X_LANES):
                r0 = lo + k * V7X_LANES
                dtxt_ref[r0:r0 + V7X_LANES, :] = dtx[:, k * V7X_LANES:(k + 1) * V7X_LANES].T
        elif lo < SSM_D_INNER + SSM_BC:
            b_ref[:, lo - SSM_D_INNER:lo - SSM_D_INNER + CONV_COLS] = act
        else:
            off = lo - SSM_D_INNER - SSM_BC
            c_ref[:, off:off + CONV_COLS] = act


def _sfront_call(x, sconv, ssmc, nw, wa, wz, wxbc, wdt, wq, wg, scw, cw, cb, dtb, alog, wsc, e):
    n = x.shape[0]
    args = (x, sconv, ssmc, nw, wa, wz, wxbc, wdt, wq, wg, scw, cw, cb, dtb, alog, wsc, e)
    shapes = [
        (n, D_MODEL), (n, D_MODEL), (n, D_MODEL),
        (n, SSM_D_INNER), (n, SSM_D_INNER),
        (SSM_D_INNER, n), (V7X_LANES, n),
        (n, SSM_BC), (n, SSM_BC),
        (n, D_MODEL),
        sconv.shape, ssmc.shape,
    ]
    return pl.pallas_call(
        _sfront_kernel,
        grid=(1,),
        in_specs=[_whole(a.shape) for a in args],
        out_specs=[_whole(s) for s in shapes],
        out_shape=[jax.ShapeDtypeStruct(s, F32) for s in shapes],
        compiler_params=_params(1),
        name="sample_front",
    )(*args)


def _sssm_kernel(s_ref, dtxt_ref, dect_ref, b_ref, c_ref, so_ref, y_ref):
    i = pl.program_id(0)
    n = dtxt_ref.shape[1]
    lane = lax.broadcasted_iota(jnp.int32, (1, n), 1)
    sub = lax.broadcasted_iota(jnp.int32, (V7X_SUBLANES, SSM_STATE), 0)
    for k in range(SAMPLE_BLOCK):
        onehot = (lane == i * SAMPLE_BLOCK + k).astype(F32)
        xcol = jnp.sum(dtxt_ref[...] * onehot, axis=-1, keepdims=True)
        dcol = jnp.sum(dect_ref[0:SSM_HEADS, :] * onehot, axis=-1, keepdims=True)
        for h in range(SSM_HEADS):
            g = h // HEADS_PER_GROUP
            rows = slice(h * SSM_HEAD_DIM, (h + 1) * SSM_HEAD_DIM)
            brow = b_ref[k:k + 1, g * SSM_STATE:(g + 1) * SSM_STATE]
            so_ref[k, rows, :] = s_ref[k, rows, :] * dcol[h:h + 1, :] + xcol[rows, :] * brow
        c8 = jnp.zeros((V7X_SUBLANES, SSM_STATE), F32)
        for g in range(SSM_GROUPS):
            c8 = jnp.where(sub == g, c_ref[k:k + 1, g * SSM_STATE:(g + 1) * SSM_STATE], c8)
        y8 = _mm_nt(c8.astype(BF16), so_ref[k].astype(BF16))
        y_ref[k:k + 1, :] = jnp.concatenate(
            [y8[g:g + 1, g * GROUP_WIDTH:(g + 1) * GROUP_WIDTH] for g in range(SSM_GROUPS)], axis=1)


def _sssm_call(state, dtxt, dect, bm, cm):
    n = state.shape[0]
    blk = SAMPLE_BLOCK
    st = pl.BlockSpec((blk, SSM_D_INNER, SSM_STATE), lambda i: (i, 0, 0))
    return pl.pallas_call(
        _sssm_kernel,
        grid=(n // blk,),
        in_specs=[st, _whole(dtxt.shape), _whole(dect.shape),
                  pl.BlockSpec((blk, SSM_BC), lambda i: (i, 0)),
                  pl.BlockSpec((blk, SSM_BC), lambda i: (i, 0))],
        out_specs=[st, pl.BlockSpec((blk, SSM_D_INNER), lambda i: (i, 0))],
        out_shape=[jax.ShapeDtypeStruct(state.shape, F32),
                   jax.ShapeDtypeStruct((n, SSM_D_INNER), F32)],
        compiler_params=_params(1),
        name="sample_ssm",
    )(state, dtxt, dect, bm, cm)


def _cache_rows(c):
    n = c.shape[1]
    c = c.reshape(n, MEM_LEN, ATTN_HEADS, ATTN_HEAD_DIM // V7X_LANES, V7X_LANES)
    return c.transpose(0, 1, 3, 2, 4).reshape(n * MEM_LEN * CACHE_ROWS, V7X_LANES)


def _cache_head(ref, k, h):
    base = k * MEM_LEN * CACHE_ROWS
    halves = [ref[pl.ds(base + j * ATTN_HEADS + h, MEM_LEN, stride=CACHE_ROWS), :]
              for j in range(ATTN_HEAD_DIM // V7X_LANES)]
    return jnp.concatenate(halves, axis=1).astype(BF16)


def _sattn_kernel(q_ref, k_ref, v_ref, o_ref):
    sub = lax.broadcasted_iota(jnp.int32, (SAMPLE_BLOCK, MEM_LEN), 0)
    sub_o = lax.broadcasted_iota(jnp.int32, (SAMPLE_BLOCK, ATTN_HEAD_DIM), 0)
    outs = []
    for h in range(ATTN_HEADS):
        qh = q_ref[:, h * ATTN_HEAD_DIM:(h + 1) * ATTN_HEAD_DIM].astype(BF16)
        s = jnp.zeros((SAMPLE_BLOCK, MEM_LEN), F32)
        for k in range(SAMPLE_BLOCK):
            s = jnp.where(sub == k, _mm_nt(qh, _cache_head(k_ref, k, h)), s)
        p = _softmax_rows(s * (ATTN_HEAD_DIM ** -0.5)).astype(BF16)
        o = jnp.zeros((SAMPLE_BLOCK, ATTN_HEAD_DIM), F32)
        for k in range(SAMPLE_BLOCK):
            o = jnp.where(sub_o == k, _mm(p, _cache_head(v_ref, k, h)), o)
        outs.append(o)
    o_ref[...] = jnp.concatenate(outs, axis=1)


def _sattn_call(q, ck, cv):
    n = q.shape[0]
    blk = SAMPLE_BLOCK
    kv = pl.BlockSpec((blk * MEM_LEN * CACHE_ROWS, V7X_LANES), lambda i: (i, 0))
    row = pl.BlockSpec((blk, D_MODEL), lambda i: (i, 0))
    return pl.pallas_call(
        _sattn_kernel,
        grid=(n // blk,),
        in_specs=[row, kv, kv],
        out_specs=row,
        out_shape=jax.ShapeDtypeStruct((n, D_MODEL), F32),
        compiler_params=_params(1),
        name="sample_attn",
    )(q, _cache_rows(ck), _cache_rows(cv))


def _sback_kernel(x_ref, y_ref, xs_ref, z_ref, gaya_ref, gb_ref, gc_ref, att_ref, dful_ref,
                  gnw_ref, wout_ref, wao_ref, wmo_ref, fnw_ref, wg_ref, wu_ref, wd_ref, fw_ref,
                  o_ref):
    y = (y_ref[...] + dful_ref[...] * xs_ref[...]) * _silu(z_ref[...])
    y_b = _mm(_group_norm(y, gnw_ref[...]).astype(BF16), wout_ref[...])
    y_c = _mm(att_ref[...].astype(BF16), wao_ref[...])
    merged = gaya_ref[...] + gb_ref[...] * y_b + gc_ref[...] * y_c
    x1 = x_ref[...] + _mm(merged.astype(BF16), wmo_ref[...])
    o_ref[...] = _ffn_body(x1, fnw_ref[...], wg_ref, wu_ref, wd_ref, fw_ref[...])


def _sback_call(*args):
    n = args[0].shape[0]
    return pl.pallas_call(
        _sback_kernel,
        grid=(1,),
        in_specs=[_whole(a.shape) for a in args],
        out_specs=_whole((n, D_MODEL)),
        out_shape=jax.ShapeDtypeStruct((n, D_MODEL), F32),
        compiler_params=_params(1),
        name="sample_back",
    )(*args)


def kernel(x_prompt, x_sample, mem_prompt, cache_mem_k, cache_mem_v, state_conv, state_ssm_conv, state_ssm, norm_mix_w, w_in, sc_conv_w, w_sc_out, ssm_conv_w, ssm_conv_b, ssm_dt_bias, ssm_a_log, ssm_d, ssm_norm_w, w_ssm_out, norm_mem_w, w_mem_k, w_mem_v, w_attn_o, w_merge_o, norm_ffn_w, w_ffn_gate, w_ffn_up, w_ffn_down, norm_final_w):
    depth = w_in.shape[0]
    assert depth == 1
    bp = x_prompt.shape[0]
    ns = x_sample.shape[0]

    wi = w_in[0]
    o = 0
    wa = wi[:, o:o + 3 * SC_DIM].astype(BF16); o += 3 * SC_DIM
    wz = wi[:, o:o + SSM_D_INNER].astype(BF16); o += SSM_D_INNER
    wxbc = wi[:, o:o + SSM_CONV_DIM].astype(BF16); o += SSM_CONV_DIM
    wdt = jnp.pad(wi[:, o:o + SSM_HEADS], ((0, 0), (0, V7X_LANES - SSM_HEADS))).astype(BF16); o += SSM_HEADS
    wq = wi[:, o:o + D_MODEL].astype(BF16); o += D_MODEL
    wg = wi[:, o:o + 3 * D_MODEL].astype(BF16)
    wg_ac = jnp.concatenate([wg[:, 0:D_MODEL], wg[:, 2 * D_MODEL:]], axis=1)
    wg_b = wg[:, D_MODEL:2 * D_MODEL]
    row = lambda v: v.reshape(1, -1).astype(F32)
    pad_heads = lambda v: jnp.pad(row(v), ((0, 0), (0, V7X_LANES - SSM_HEADS)))
    nmix = row(norm_mix_w[0])
    dtb = pad_heads(ssm_dt_bias[0])
    alog = pad_heads(ssm_a_log[0])
    dful = row(jnp.repeat(ssm_d[0], SSM_HEAD_DIM))
    gnw = row(ssm_norm_w[0])
    cb = row(ssm_conv_b[0])
    cw = ssm_conv_w[0]
    scw = sc_conv_w[0]
    wsc = w_sc_out[0].astype(BF16)
    wout = w_ssm_out[0].astype(BF16)
    wao = w_attn_o[0].astype(BF16)
    wmo = w_merge_o[0].astype(BF16)
    wfg = w_ffn_gate[0].astype(BF16)
    wfu = w_ffn_up[0].astype(BF16)
    wfd = w_ffn_down[0].astype(BF16)
    nffn = row(norm_ffn_w[0])
    nfin = row(norm_final_w)
    expand = (jnp.arange(V7X_LANES)[:, None] == (jnp.arange(SSM_D_INNER)[None, :] // SSM_HEAD_DIM)).astype(BF16)

    mk, mv, mkb, mvb = _mem_call(mem_prompt, row(norm_mem_w[0]), w_mem_k[0].astype(BF16), w_mem_v[0].astype(BF16))
    ybg, p_ssmc, p_ssm = _ssd_call(x_prompt, nmix, wz, wxbc, wdt, wg_b, cw, cb, dtb, alog, dful, gnw, wout, expand)
    x1, p_conv = _mix_call(x_prompt, ybg, mkb, mvb, nmix, wa, wq, wg_ac, scw, wsc, wao, wmo)
    y_prompt = _ffn_call(x1, nffn, wfg, wfu, wfd, nfin)

    xs2 = x_sample.reshape(ns, D_MODEL)
    (gaya, gb, gc, z, xs, dtxt, dect, bm, cm, q, s_conv, s_ssmc) = _sfront_call(
        xs2, state_conv[0].reshape(ns, -1), jnp.swapaxes(state_ssm_conv[0], 0, 1),
        nmix, wa, wz, wxbc, wdt, wq, wg, scw, cw, cb, dtb, alog, wsc, expand)
    s_ssm, y_s = _sssm_call(state_ssm[0].reshape(ns, SSM_D_INNER, SSM_STATE), dtxt, dect, bm, cm)
    att = _sattn_call(q, cache_mem_k, cache_mem_v)
    y_sample = _sback_call(xs2, y_s, xs, z, gaya, gb, gc, att, dful, gnw, wout, wao, wmo,
                           nffn, wfg, wfu, wfd, nfin)

    def from_rows(r):
        r = r.reshape(bp, MEM_LEN, ATTN_HEAD_DIM // V7X_LANES, ATTN_HEADS, V7X_LANES)
        return r.transpose(0, 1, 3, 2, 4).reshape(depth, bp, MEM_LEN, ATTN_HEADS, ATTN_HEAD_DIM)

    state_shape = (SSM_HEADS, SSM_HEAD_DIM, SSM_STATE)
    return (
        y_prompt,
        y_sample.reshape(ns, 1, D_MODEL),
        from_rows(mk),
        from_rows(mv),
        p_conv.reshape(depth, bp, SC_WIDTH - 1, SC_DIM),
        p_ssmc.reshape(depth, bp, SSM_CONV - 1, SSM_CONV_DIM),
        p_ssm.reshape((depth, bp) + state_shape),
        s_conv.reshape(depth, ns, SC_WIDTH - 1, SC_DIM),
        jnp.swapaxes(s_ssmc, 0, 1).reshape(depth, ns, SSM_CONV - 1, SSM_CONV_DIM),
        s_ssm.reshape((depth, ns) + state_shape),
    )
```

```python
import jax
import jax.numpy as jnp
from jax import lax
from jax.experimental import pallas as pl
from jax.experimental.pallas import tpu as pltpu

F32 = jnp.float32
BF16 = jnp.bfloat16

D_MODEL = 1024
RMS_EPS = 1e-6
LOG2_E = 1.4426950408889634
SC_DIM = D_MODEL
SC_WIDTH = 3
SSM_D_INNER = 2 * D_MODEL
SSM_HEAD_DIM = 64
SSM_HEADS = SSM_D_INNER // SSM_HEAD_DIM
SSM_STATE = 128
SSM_GROUPS = 4
SSM_CONV = 4
SSM_CHUNK = 128
SSM_BC = SSM_GROUPS * SSM_STATE
SSM_CONV_DIM = SSM_D_INNER + 2 * SSM_BC
HEADS_PER_GROUP = SSM_HEADS // SSM_GROUPS
GROUP_WIDTH = SSM_D_INNER // SSM_GROUPS
MEM_LEN = 256
ATTN_HEADS = 4
ATTN_HEAD_DIM = D_MODEL // ATTN_HEADS
FFN_HIDDEN = ((8 * D_MODEL // 3 + 255) // 256) * 256

V7X_LANES = 128
V7X_SUBLANES = 8
V7X_VMEM_BYTES = 64 * 1024 * 1024
VMEM_LIMIT_BYTES = V7X_VMEM_BYTES - 8 * 1024 * 1024
CACHE_ROWS = ATTN_HEADS * ATTN_HEAD_DIM // V7X_LANES

SSD_TILE = 512
SSD_SUB = 256
MIX_TILE = 512
FFN_TILE = 512
SAMPLE_BLOCK = 8
CONV_COLS = 512


def _params(n_grid, flags=None):
    return pltpu.CompilerParams(
        dimension_semantics=("arbitrary",) * n_grid,
        vmem_limit_bytes=VMEM_LIMIT_BYTES,
        flags=flags,
    )


def _whole(shape):
    nd = len(shape)
    return pl.BlockSpec(shape, lambda *_: (0,) * nd)


def _mm(a, b):
    return jnp.dot(a, b, preferred_element_type=F32)


def _mm_nt(a, b):
    return lax.dot_general(a, b, (((1,), (1,)), ((), ())), preferred_element_type=F32)


def _split3(x):
    hi = x.astype(BF16)
    r = x - hi.astype(F32)
    mid = r.astype(BF16)
    lo = (r - mid.astype(F32)).astype(BF16)
    return hi, mid, lo


def _mm_sel(sel, x):
    hi, mid, lo = _split3(x)
    return _mm(sel, hi) + _mm(sel, mid) + _mm(sel, lo)


def _mm_expand(x, sel):
    hi, mid, lo = _split3(x)
    return _mm(hi, sel) + _mm(mid, sel) + _mm(lo, sel)


def _rms(x, w):
    return x * lax.rsqrt(jnp.mean(x * x, axis=-1, keepdims=True) + RMS_EPS) * w


def _sigmoid(x):
    return 1.0 / (1.0 + jnp.exp2(x * (-LOG2_E)))


def _silu(x):
    return x * _sigmoid(x)


def _softplus(x):
    return jnp.maximum(x, 0.0) + jnp.log1p(jnp.exp(-jnp.abs(x)))


def _shift_rows(u, prev8, k):
    r = pltpu.roll(u, k, axis=0)
    p = pltpu.roll(prev8, k, axis=0)
    row = lax.broadcasted_iota(jnp.int32, prev8.shape, 0)
    head = jnp.where(row < k, p, r[:V7X_SUBLANES])
    return jnp.concatenate([head, r[V7X_SUBLANES:]], axis=0)


def _softmax_rows(s):
    m = jnp.max(s, axis=-1, keepdims=True)
    p = jnp.exp(s - m)
    return p / jnp.sum(p, axis=-1, keepdims=True)


def _group_norm(y, w):
    outs = []
    for g in range(SSM_GROUPS):
        cols = slice(g * GROUP_WIDTH, (g + 1) * GROUP_WIDTH)
        yg = y[:, cols]
        ms = jnp.mean(yg * yg, axis=-1, keepdims=True)
        outs.append(yg * lax.rsqrt(ms + RMS_EPS) * w[:, cols])
    return jnp.concatenate(outs, axis=1)


def _mem_kernel(m_ref, nw_ref, wk_ref, wv_ref, k_ref, v_ref, kb_ref, vb_ref):
    mn = _rms(m_ref[...], nw_ref[...]).astype(BF16)
    k = _mm(mn, wk_ref[...])
    v = _mm(mn, wv_ref[...])
    kb_ref[...] = k.astype(BF16)
    vb_ref[...] = v.astype(BF16)
    for j in range(ATTN_HEAD_DIM // V7X_LANES):
        for h in range(ATTN_HEADS):
            rows = pl.ds(j * ATTN_HEADS + h, MEM_LEN, stride=CACHE_ROWS)
            lo = h * ATTN_HEAD_DIM + j * V7X_LANES
            k_ref[rows, :] = k[:, lo:lo + V7X_LANES]
            v_ref[rows, :] = v[:, lo:lo + V7X_LANES]


def _mem_call(mem, nw, wk, wv):
    b = mem.shape[0]
    blk = pl.BlockSpec((None, MEM_LEN, D_MODEL), lambda i: (i, 0, 0))
    rows = pl.BlockSpec((None, MEM_LEN * CACHE_ROWS, V7X_LANES), lambda i: (i, 0, 0))
    rows_shape = jax.ShapeDtypeStruct((b, MEM_LEN * CACHE_ROWS, V7X_LANES), F32)
    return pl.pallas_call(
        _mem_kernel,
        grid=(b,),
        in_specs=[blk, _whole(nw.shape), _whole(wk.shape), _whole(wv.shape)],
        out_specs=[rows, rows, blk, blk],
        out_shape=[
            rows_shape,
            rows_shape,
            jax.ShapeDtypeStruct(mem.shape, BF16),
            jax.ShapeDtypeStruct(mem.shape, BF16),
        ],
        compiler_params=_params(1),
        name="mem_kv",
    )(mem, nw, wk, wv)


def _ssd_kernel(x_ref, nw_ref, wz_ref, wxbc_ref, wdt_ref, wgb_ref, cw_ref, cb_ref,
                dtb_ref, alog_ref, dful_ref, gnw_ref, wout_ref, e_ref,
                ybg_ref, cst_ref, sst_ref,
                hist_ref, st_ref, *scratch):
    t = pl.program_id(1)
    tile = x_ref.shape[0]
    q = SSM_CHUNK
    n_sub = tile // SSD_SUB
    sets = [scratch[k * 8:(k + 1) * 8] for k in range(n_sub)]

    @pl.when(t == 0)
    def _():
        hist_ref[...] = jnp.zeros_like(hist_ref)
        st_ref[...] = jnp.zeros_like(st_ref)

    def project(sub):
        xn_s, z_s, xs_s, b_s, c_s, dt_s, da_s, _ = sets[sub]
        xn_s[...] = _rms(x_ref[sub * SSD_SUB:(sub + 1) * SSD_SUB, :], nw_ref[...]).astype(BF16)
        dt = _softplus(_mm(xn_s[...], wdt_ref[...]) + dtb_ref[...])
        dt_s[...] = dt
        da_s[...] = dt * (-jnp.exp(alog_ref[...]))
        n_blocks = SSM_CONV_DIM // CONV_COLS
        first_bc = SSM_D_INNER // CONV_COLS
        for j in list(range(first_bc, n_blocks)) + list(range(first_bc)):
            cols = slice(j * CONV_COLS, (j + 1) * CONV_COLS)
            u = _mm(xn_s[...], wxbc_ref[:, cols])
            prev8 = hist_ref[:, cols]
            hist_ref[:, cols] = u[SSD_SUB - V7X_SUBLANES:]
            cw = cw_ref[:, cols]
            conv = _shift_rows(u, prev8, 3) * cw[0:1]
            conv = conv + _shift_rows(u, prev8, 2) * cw[1:2]
            conv = conv + _shift_rows(u, prev8, 1) * cw[2:3]
            conv = conv + u * cw[3:4]
            act = _silu(conv + cb_ref[:, cols])
            lo = j * CONV_COLS
            if lo < SSM_D_INNER:
                xs_s[:, cols] = act
            elif lo < SSM_D_INNER + SSM_BC:
                b_s[:, lo - SSM_D_INNER:lo - SSM_D_INNER + CONV_COLS] = act
            else:
                off = lo - SSM_D_INNER - SSM_BC
                c_s[:, off:off + CONV_COLS] = act
        z_s[...] = _silu(_mm(xn_s[...], wz_ref[...]))

    ri = lax.broadcasted_iota(jnp.int32, (q, q), 0)
    ci = lax.broadcasted_iota(jnp.int32, (q, q), 1)
    causal = ri >= ci
    tri = jnp.where(causal, 1.0, 0.0).astype(BF16)
    lane_lo = ci < SSM_HEAD_DIM
    keep_lo = jnp.where(lane_lo, 1.0, 0.0).astype(BF16)
    keep_hi = jnp.where(lane_lo, 0.0, 1.0).astype(BF16)
    sub8 =lax.broadcasted_iota(jnp.int32, (V7X_SUBLANES, q), 0)

    groups = range(SSM_GROUPS)
    gsl = [slice(g * GROUP_WIDTH, (g + 1) * GROUP_WIDTH) for g in groups]
    nsl = [slice(g * SSM_STATE, (g + 1) * SSM_STATE) for g in groups]

    def scan(sub, chunks):
        _, z_s, xs_s, b_s, c_s, dt_s, da_s, yn_s = sets[sub]
        rows = {c: pl.ds(c * q, q) for c in chunks}

        acum, row_t, w, cd = {}, {}, {}, {}
        for c in chunks:
            a = _mm_sel(tri, da_s[rows[c], :])
            last = a[q - 1:q, :]
            dtc = dt_s[rows[c], :]
            acum[c] = a
            row_t[c] = a.T - jnp.log(dtc.T)
            w[c] = (dtc * jnp.exp(last - a)).astype(BF16)
            cd[c] = jnp.where(sub8 == 0, jnp.exp(last), 0.0)

        bg = {(c, g): b_s[rows[c], nsl[g]] for c in chunks for g in groups}
        cg = {(c, g): c_s[rows[c], nsl[g]].astype(BF16) for c in chunks for g in groups}
        cb = {k: _mm_nt(cg[k], bg[k].astype(BF16)).astype(BF16) for k in bg}
        wx = {(c, g): _mm(w[c], e_ref[:, gsl[g]]) for c in chunks for g in groups}
        dec = {(c, g): _mm_expand(cd[c], e_ref[:, gsl[g]])[0:1, :] for c in chunks for g in groups}

        yo = {}
        for c in chunks:
            st = [st_ref[:, gsl[g]] for g in groups]
            for g in groups:
                yo[c, g] = _mm(cg[c, g], st[g].astype(BF16))
            for g in groups:
                xw = (xs_s[rows[c], gsl[g]] * wx[c, g]).astype(BF16)
                st_ref[:, gsl[g]] = st[g] * dec[c, g] + _mm(bg[c, g].T.astype(BF16), xw)

        yd = {}
        for c in chunks:
            for g in groups:
                for pq in range(HEADS_PER_GROUP // 2):
                    h0 = g * HEADS_PER_GROUP + 2 * pq
                    h1 = h0 + 1
                    l0 = jnp.where(causal, jnp.exp(acum[c][:, h0:h0 + 1] - row_t[c][h0:h0 + 1, :]), 0.0)
                    l1 = jnp.where(causal, jnp.exp(acum[c][:, h1:h1 + 1] - row_t[c][h1:h1 + 1, :]), 0.0)
                    lhs = jnp.concatenate([cb[c, g] * l0.astype(BF16), cb[c, g] * l1.astype(BF16)], axis=1)
                    xpb = xs_s[rows[c], h0 * SSM_HEAD_DIM:(h1 + 1) * SSM_HEAD_DIM].astype(BF16)
                    rhs = jnp.concatenate([xpb * keep_lo, xpb * keep_hi], axis=0)
                    yd[c, g, pq] = _mm(lhs, rhs)

        for c in chunks:
            for g in groups:
                pairs = []
                for pq in range(HEADS_PER_GROUP // 2):
                    h0 = g * HEADS_PER_GROUP + 2 * pq
                    h1 = h0 + 1
                    pcols = slice(h0 * SSM_HEAD_DIM, (h1 + 1) * SSM_HEAD_DIM)
                    sc = jnp.where(lane_lo, jnp.exp(acum[c][:, h0:h0 + 1]), jnp.exp(acum[c][:, h1:h1 + 1]))
                    lc = slice(2 * pq * SSM_HEAD_DIM, (2 * pq + 2) * SSM_HEAD_DIM)
                    pairs.append(yd[c, g, pq] + sc * yo[c, g][:, lc] + dful_ref[:, pcols] * xs_s[rows[c], pcols])
                yg = jnp.concatenate(pairs, axis=1) * z_s[rows[c], gsl[g]]
                ms = jnp.mean(yg * yg, axis=-1, keepdims=True)
                yn_s[rows[c], gsl[g]] = (yg * lax.rsqrt(ms + RMS_EPS) * gnw_ref[:, gsl[g]]).astype(BF16)

    for sub in range(n_sub):
        project(sub)
    for sub in range(n_sub):
        for c in range(SSD_SUB // q):
            scan(sub, [c])
        xn_s, yn_s = sets[sub][0], sets[sub][7]
        gb = _sigmoid(_mm(xn_s[...], wgb_ref[...]))
        ybg_ref[sub * SSD_SUB:(sub + 1) * SSD_SUB, :] = gb * _mm(yn_s[...], wout_ref[...])

    @pl.when(t == pl.num_programs(1) - 1)
    def _():
        cst_ref[...] = pltpu.roll(hist_ref[...], SSM_CONV - 1, axis=0)[0:SSM_CONV - 1]
        for k in range(SSM_D_INNER // V7X_LANES):
            blk = slice(k * V7X_LANES, (k + 1) * V7X_LANES)
            sst_ref[blk, :] = st_ref[:, blk].T


def _ssd_call(x, nw, wz, wxbc, wdt, wgb, cw, cb, dtb, alog, dful, gnw, wout, e):
    b, s, d = x.shape
    tile = SSD_TILE
    consts = (nw, wz, wxbc, wdt, wgb, cw, cb, dtb, alog, dful, gnw, wout, e)
    return pl.pallas_call(
        _ssd_kernel,
        grid=(b, s // tile),
        in_specs=[pl.BlockSpec((None, tile, d), lambda i, j: (i, j, 0))]
        + [_whole(c.shape) for c in consts],
        out_specs=[
            pl.BlockSpec((None, tile, d), lambda i, j: (i, j, 0)),
            pl.BlockSpec((None, SSM_CONV - 1, SSM_CONV_DIM), lambda i, j: (i, 0, 0)),
            pl.BlockSpec((None, SSM_D_INNER, SSM_STATE), lambda i, j: (i, 0, 0)),
        ],
        out_shape=[
            jax.ShapeDtypeStruct((b, s, d), F32),
            jax.ShapeDtypeStruct((b, SSM_CONV - 1, SSM_CONV_DIM), F32),
            jax.ShapeDtypeStruct((b, SSM_D_INNER, SSM_STATE), F32),
        ],
        scratch_shapes=[
            pltpu.VMEM((V7X_SUBLANES, SSM_CONV_DIM), F32),
            pltpu.VMEM((SSM_STATE, SSM_D_INNER), F32),
        ] + [
            pltpu.VMEM((SSD_SUB, d), BF16),
            pltpu.VMEM((SSD_SUB, SSM_D_INNER), F32),
            pltpu.VMEM((SSD_SUB, SSM_D_INNER), F32),
            pltpu.VMEM((SSD_SUB, SSM_BC), F32),
            pltpu.VMEM((SSD_SUB, SSM_BC), F32),
            pltpu.VMEM((SSD_SUB, V7X_LANES), F32),
            pltpu.VMEM((SSD_SUB, V7X_LANES), F32),
            pltpu.VMEM((SSD_SUB, SSM_D_INNER), BF16),
        ] * (tile // SSD_SUB),
        compiler_params=_params(2),
        name="ssd_prompt",
    )(x, *consts)


def _attention(q, k_ref, v_ref):
    outs = []
    for h in range(ATTN_HEADS):
        cols = slice(h * ATTN_HEAD_DIM, (h + 1) * ATTN_HEAD_DIM)
        s = _mm_nt(q[:, cols].astype(BF16), k_ref[:, cols]) * (ATTN_HEAD_DIM ** -0.5)
        outs.append(_mm(_softmax_rows(s).astype(BF16), v_ref[:, cols]))
    return jnp.concatenate(outs, axis=1)


def _mix_kernel(x_ref, ybg_ref, kb_ref, vb_ref, nw_ref, wa_ref, wq_ref, wg_ref, cw_ref,
                wsc_ref, wao_ref, wmo_ref, x1_ref, cst_ref, hist_ref):
    t = pl.program_id(1)
    tile = x_ref.shape[0]

    @pl.when(t == 0)
    def _():
        hist_ref[...] = jnp.zeros_like(hist_ref)

    x = x_ref[...]
    xn = _rms(x, nw_ref[...]).astype(BF16)

    pa = _mm(xn, wa_ref[...])
    sc_b = pa[:, 0:SC_DIM]
    u = pa[:, SC_DIM:2 * SC_DIM] * pa[:, 2 * SC_DIM:3 * SC_DIM]
    prev8 = hist_ref[...]
    hist_ref[...] = u[tile - V7X_SUBLANES:]
    cw = cw_ref[...]
    conv = _shift_rows(u, prev8, 2) * cw[0:1]
    conv = conv + _shift_rows(u, prev8, 1) * cw[1:2]
    conv = conv + u * cw[2:3]
    y_a = _mm((sc_b * conv).astype(BF16), wsc_ref[...])

    att = _attention(_mm(xn, wq_ref[...]), kb_ref, vb_ref)
    y_c = _mm(att.astype(BF16), wao_ref[...])

    gates = _sigmoid(_mm(xn, wg_ref[...]))
    g_a = gates[:, 0:D_MODEL]
    g_c = gates[:, D_MODEL:2 * D_MODEL]
    merged = g_a * y_a + ybg_ref[...] + g_c * y_c
    x1_ref[...] = x + _mm(merged.astype(BF16), wmo_ref[...])

    @pl.when(t == pl.num_programs(1) - 1)
    def _():
        cst_ref[...] = pltpu.roll(hist_ref[...], SC_WIDTH - 1, axis=0)[0:SC_WIDTH - 1]


def _mix_call(x, ybg, kb, vb, nw, wa, wq, wg, cw, wsc, wao, wmo):
    b, s, d = x.shape
    tile = MIX_TILE
    consts = (nw, wa, wq, wg, cw, wsc, wao, wmo)
    tok = pl.BlockSpec((None, tile, d), lambda i, j: (i, j, 0))
    mem = pl.BlockSpec((None, MEM_LEN, d), lambda i, j: (i, 0, 0))
    return pl.pallas_call(
        _mix_kernel,
        grid=(b, s // tile),
        in_specs=[tok, tok, mem, mem] + [_whole(c.shape) for c in consts],
        out_specs=[tok, pl.BlockSpec((None, SC_WIDTH - 1, SC_DIM), lambda i, j: (i, 0, 0))],
        out_shape=[
            jax.ShapeDtypeStruct((b, s, d), F32),
            jax.ShapeDtypeStruct((b, SC_WIDTH - 1, SC_DIM), F32),
        ],
        scratch_shapes=[pltpu.VMEM((V7X_SUBLANES, SC_DIM), F32)],
        compiler_params=_params(2),
        name="mix_prompt",
    )(x, ybg, kb, vb, *consts)


def _ffn_body(x, nw, wg_ref, wu_ref, wd_ref, fw):
    xn = _rms(x, nw).astype(BF16)
    h = _silu(_mm(xn, wg_ref[...])) * _mm(xn, wu_ref[...])
    x2 = x + _mm(h.astype(BF16), wd_ref[...])
    return _rms(x2, fw)


def _ffn_kernel(x_ref, nw_ref, wg_ref, wu_ref, wd_ref, fw_ref, o_ref):
    o_ref[...] = _ffn_body(x_ref[...], nw_ref[...], wg_ref, wu_ref, wd_ref, fw_ref[...])


def _ffn_call(x, nw, wg, wu, wd, fw):
    b, s, d = x.shape
    tile = FFN_TILE
    consts = (nw, wg, wu, wd, fw)
    tok = pl.BlockSpec((None, tile, d), lambda i, j: (i, j, 0))
    return pl.pallas_call(
        _ffn_kernel,
        grid=(b, s // tile),
        in_specs=[tok] + [_whole(c.shape) for c in consts],
        out_specs=tok,
        out_shape=jax.ShapeDtypeStruct((b, s, d), F32),
        compiler_params=_params(2),
        name="ffn_prompt",
    )(x, *consts)


def _sfront_kernel(x_ref, sconv_ref, ssmc_ref, nw_ref, wa_ref, wz_ref, wxbc_ref, wdt_ref,
                   wq_ref, wg_ref, scw_ref, cw_ref, cb_ref, dtb_ref, alog_ref, wsc_ref, e_ref,
                   gaya_ref, gb_ref, gc_ref, z_ref, xs_ref, dtxt_ref, dect_ref, b_ref, c_ref,
                   q_ref, sconv_o, ssmc_o):
    xn = _rms(x_ref[...], nw_ref[...]).astype(BF16)

    sc_b = _mm(xn, wa_ref[:, 0:SC_DIM])
    u = _mm(xn, wa_ref[:, SC_DIM:2 * SC_DIM]) * _mm(xn, wa_ref[:, 2 * SC_DIM:3 * SC_DIM])
    h0 = sconv_ref[:, 0:SC_DIM]
    h1 = sconv_ref[:, SC_DIM:2 * SC_DIM]
    scw = scw_ref[...]
    conv = h0 * scw[0:1] + h1 * scw[1:2] + u * scw[2:3]
    sconv_o[:, 0:SC_DIM] = h1
    sconv_o[:, SC_DIM:2 * SC_DIM] = u
    y_a = _mm((sc_b * conv).astype(BF16), wsc_ref[...])
    gaya_ref[...] = _sigmoid(_mm(xn, wg_ref[:, 0:D_MODEL])) * y_a
    gb_ref[...] = _sigmoid(_mm(xn, wg_ref[:, D_MODEL:2 * D_MODEL]))
    gc_ref[...] = _sigmoid(_mm(xn, wg_ref[:, 2 * D_MODEL:3 * D_MODEL]))
    z_ref[...] = _mm(xn, wz_ref[...])
    q_ref[...] = _mm(xn, wq_ref[...])

    dt = _softplus(_mm(xn, wdt_ref[...]) + dtb_ref[...])
    dect_ref[...] = jnp.exp(dt * (-jnp.exp(alog_ref[...]))).T
    dtexp = _mm_expand(dt, e_ref[...])

    for j in range(SSM_CONV_DIM // CONV_COLS):
        lo = j * CONV_COLS
        cols = slice(lo, lo + CONV_COLS)
        xbc = _mm(xn, wxbc_ref[:, cols])
        p0 = ssmc_ref[0, :, cols]
        p1 = ssmc_ref[1, :, cols]
        p2 = ssmc_ref[2, :, cols]
        cw = cw_ref[:, cols]
        conv = p0 * cw[0:1] + p1 * cw[1:2] + p2 * cw[2:3] + xbc * cw[3:4]
        ssmc_o[0, :, cols] = p1
        ssmc_o[1, :, cols] = p2
        ssmc_o[2, :, cols] = xbc
        act = _silu(conv + cb_ref[:, cols])
        if lo < SSM_D_INNER:
            xs_ref[:, cols] = act
            dtx = act * dtexp[:, cols]
            for k in range(CONV_COLS // V7X_LANES):
                r0 = lo + k * V7X_LANES
                dtxt_ref[r0:r0 + V7X_LANES, :] = dtx[:, k * V7X_LANES:(k + 1) * V7X_LANES].T
        elif lo < SSM_D_INNER + SSM_BC:
            b_ref[:, lo - SSM_D_INNER:lo - SSM_D_INNER + CONV_COLS] = act
        else:
            off = lo - SSM_D_INNER - SSM_BC
            c_ref[:, off:off + CONV_COLS] = act


def _sfront_call(x, sconv, ssmc, nw, wa, wz, wxbc, wdt, wq, wg, scw, cw, cb, dtb, alog, wsc, e):
    n = x.shape[0]
    args = (x, sconv, ssmc, nw, wa, wz, wxbc, wdt, wq, wg, scw, cw, cb, dtb, alog, wsc, e)
    shapes = [
        (n, D_MODEL), (n, D_MODEL), (n, D_MODEL),
        (n, SSM_D_INNER), (n, SSM_D_INNER),
        (SSM_D_INNER, n), (V7X_LANES, n),
        (n, SSM_BC), (n, SSM_BC),
        (n, D_MODEL),
        sconv.shape, ssmc.shape,
    ]
    return pl.pallas_call(
        _sfront_kernel,
        grid=(1,),
        in_specs=[_whole(a.shape) for a in args],
        out_specs=[_whole(s) for s in shapes],
        out_shape=[jax.ShapeDtypeStruct(s, F32) for s in shapes],
        compiler_params=_params(1),
        name="sample_front",
    )(*args)


def _sssm_kernel(s_ref, dtxt_ref, dect_ref, b_ref, c_ref, so_ref, y_ref):
    i = pl.program_id(0)
    n = dtxt_ref.shape[1]
    lane = lax.broadcasted_iota(jnp.int32, (1, n), 1)
    sub = lax.broadcasted_iota(jnp.int32, (V7X_SUBLANES, SSM_STATE), 0)
    for k in range(SAMPLE_BLOCK):
        onehot = (lane == i * SAMPLE_BLOCK + k).astype(F32)
        xcol = jnp.sum(dtxt_ref[...] * onehot, axis=-1, keepdims=True)
        dcol = jnp.sum(dect_ref[0:SSM_HEADS, :] * onehot, axis=-1, keepdims=True)
        for h in range(SSM_HEADS):
            g = h // HEADS_PER_GROUP
            rows = slice(h * SSM_HEAD_DIM, (h + 1) * SSM_HEAD_DIM)
            brow = b_ref[k:k + 1, g * SSM_STATE:(g + 1) * SSM_STATE]
            so_ref[k, rows, :] = s_ref[k, rows, :] * dcol[h:h + 1, :] + xcol[rows, :] * brow
        c8 = jnp.zeros((V7X_SUBLANES, SSM_STATE), F32)
        for g in range(SSM_GROUPS):
            c8 = jnp.where(sub == g, c_ref[k:k + 1, g * SSM_STATE:(g + 1) * SSM_STATE], c8)
        y8 = _mm_nt(c8.astype(BF16), so_ref[k].astype(BF16))
        y_ref[k:k + 1, :] = jnp.concatenate(
            [y8[g:g + 1, g * GROUP_WIDTH:(g + 1) * GROUP_WIDTH] for g in range(SSM_GROUPS)], axis=1)


def _sssm_call(state, dtxt, dect, bm, cm):
    n = state.shape[0]
    blk = SAMPLE_BLOCK
    st = pl.BlockSpec((blk, SSM_D_INNER, SSM_STATE), lambda i: (i, 0, 0))
    return pl.pallas_call(
        _sssm_kernel,
        grid=(n // blk,),
        in_specs=[st, _whole(dtxt.shape), _whole(dect.shape),
                  pl.BlockSpec((blk, SSM_BC), lambda i: (i, 0)),
                  pl.BlockSpec((blk, SSM_BC), lambda i: (i, 0))],
        out_specs=[st, pl.BlockSpec((blk, SSM_D_INNER), lambda i: (i, 0))],
        out_shape=[jax.ShapeDtypeStruct(state.shape, F32),
                   jax.ShapeDtypeStruct((n, SSM_D_INNER), F32)],
        compiler_params=_params(1),
        name="sample_ssm",
    )(state, dtxt, dect, bm, cm)


def _cache_rows(c):
    n = c.shape[1]
    c = c.reshape(n, MEM_LEN, ATTN_HEADS, ATTN_HEAD_DIM // V7X_LANES, V7X_LANES)
    return c.transpose(0, 1, 3, 2, 4).reshape(n * MEM_LEN * CACHE_ROWS, V7X_LANES)


def _cache_head(ref, k, h):
    base = k * MEM_LEN * CACHE_ROWS
    halves = [ref[pl.ds(base + j * ATTN_HEADS + h, MEM_LEN, stride=CACHE_ROWS), :]
              for j in range(ATTN_HEAD_DIM // V7X_LANES)]
    return jnp.concatenate(halves, axis=1).astype(BF16)


def _sattn_kernel(q_ref, k_ref, v_ref, o_ref):
    sub = lax.broadcasted_iota(jnp.int32, (SAMPLE_BLOCK, MEM_LEN), 0)
    sub_o = lax.broadcasted_iota(jnp.int32, (SAMPLE_BLOCK, ATTN_HEAD_DIM), 0)
    outs = []
    for h in range(ATTN_HEADS):
        qh = q_ref[:, h * ATTN_HEAD_DIM:(h + 1) * ATTN_HEAD_DIM].astype(BF16)
        s = jnp.zeros((SAMPLE_BLOCK, MEM_LEN), F32)
        for k in range(SAMPLE_BLOCK):
            s = jnp.where(sub == k, _mm_nt(qh, _cache_head(k_ref, k, h)), s)
        p = _softmax_rows(s * (ATTN_HEAD_DIM ** -0.5)).astype(BF16)
        o = jnp.zeros((SAMPLE_BLOCK, ATTN_HEAD_DIM), F32)
        for k in range(SAMPLE_BLOCK):
            o = jnp.where(sub_o == k, _mm(p, _cache_head(v_ref, k, h)), o)
        outs.append(o)
    o_ref[...] = jnp.concatenate(outs, axis=1)


def _sattn_call(q, ck, cv):
    n = q.shape[0]
    blk = SAMPLE_BLOCK
    kv = pl.BlockSpec((blk * MEM_LEN * CACHE_ROWS, V7X_LANES), lambda i: (i, 0))
    row = pl.BlockSpec((blk, D_MODEL), lambda i: (i, 0))
    return pl.pallas_call(
        _sattn_kernel,
        grid=(n // blk,),
        in_specs=[row, kv, kv],
        out_specs=row,
        out_shape=jax.ShapeDtypeStruct((n, D_MODEL), F32),
        compiler_params=_params(1),
        name="sample_attn",
    )(q, _cache_rows(ck), _cache_rows(cv))


def _sback_kernel(x_ref, y_ref, xs_ref, z_ref, gaya_ref, gb_ref, gc_ref, att_ref, dful_ref,
                  gnw_ref, wout_ref, wao_ref, wmo_ref, fnw_ref, wg_ref, wu_ref, wd_ref, fw_ref,
                  o_ref):
    y = (y_ref[...] + dful_ref[...] * xs_ref[...]) * _silu(z_ref[...])
    y_b = _mm(_group_norm(y, gnw_ref[...]).astype(BF16), wout_ref[...])
    y_c = _mm(att_ref[...].astype(BF16), wao_ref[...])
    merged = gaya_ref[...] + gb_ref[...] * y_b + gc_ref[...] * y_c
    x1 = x_ref[...] + _mm(merged.astype(BF16), wmo_ref[...])
    o_ref[...] = _ffn_body(x1, fnw_ref[...], wg_ref, wu_ref, wd_ref, fw_ref[...])


def _sback_call(*args):
    n = args[0].shape[0]
    return pl.pallas_call(
        _sback_kernel,
        grid=(1,),
        in_specs=[_whole(a.shape) for a in args],
        out_specs=_whole((n, D_MODEL)),
        out_shape=jax.ShapeDtypeStruct((n, D_MODEL), F32),
        compiler_params=_params(1),
        name="sample_back",
    )(*args)


def kernel(x_prompt, x_sample, mem_prompt, cache_mem_k, cache_mem_v, state_conv, state_ssm_conv, state_ssm, norm_mix_w, w_in, sc_conv_w, w_sc_out, ssm_conv_w, ssm_conv_b, ssm_dt_bias, ssm_a_log, ssm_d, ssm_norm_w, w_ssm_out, norm_mem_w, w_mem_k, w_mem_v, w_attn_o, w_merge_o, norm_ffn_w, w_ffn_gate, w_ffn_up, w_ffn_down, norm_final_w):
    depth = w_in.shape[0]
    assert depth == 1
    bp = x_prompt.shape[0]
    ns = x_sample.shape[0]

    wi = w_in[0]
    o = 0
    wa = wi[:, o:o + 3 * SC_DIM].astype(BF16); o += 3 * SC_DIM
    wz = wi[:, o:o + SSM_D_INNER].astype(BF16); o += SSM_D_INNER
    wxbc = wi[:, o:o + SSM_CONV_DIM].astype(BF16); o += SSM_CONV_DIM
    wdt = jnp.pad(wi[:, o:o + SSM_HEADS], ((0, 0), (0, V7X_LANES - SSM_HEADS))).astype(BF16); o += SSM_HEADS
    wq = wi[:, o:o + D_MODEL].astype(BF16); o += D_MODEL
    wg = wi[:, o:o + 3 * D_MODEL].astype(BF16)
    wg_ac = jnp.concatenate([wg[:, 0:D_MODEL], wg[:, 2 * D_MODEL:]], axis=1)
    wg_b = wg[:, D_MODEL:2 * D_MODEL]
    row = lambda v: v.reshape(1, -1).astype(F32)
    pad_heads = lambda v: jnp.pad(row(v), ((0, 0), (0, V7X_LANES - SSM_HEADS)))
    nmix = row(norm_mix_w[0])
    dtb = pad_heads(ssm_dt_bias[0])
    alog = pad_heads(ssm_a_log[0])
    dful = row(jnp.repeat(ssm_d[0], SSM_HEAD_DIM))
    gnw = row(ssm_norm_w[0])
    cb = row(ssm_conv_b[0])
    cw = ssm_conv_w[0]
    scw = sc_conv_w[0]
    wsc = w_sc_out[0].astype(BF16)
    wout = w_ssm_out[0].astype(BF16)
    wao = w_attn_o[0].astype(BF16)
    wmo = w_merge_o[0].astype(BF16)
    wfg = w_ffn_gate[0].astype(BF16)
    wfu = w_ffn_up[0].astype(BF16)
    wfd = w_ffn_down[0].astype(BF16)
    nffn = row(norm_ffn_w[0])
    nfin = row(norm_final_w)
    expand = (jnp.arange(V7X_LANES)[:, None] == (jnp.arange(SSM_D_INNER)[None, :] // SSM_HEAD_DIM)).astype(BF16)

    mk, mv, mkb, mvb = _mem_call(mem_prompt, row(norm_mem_w[0]), w_mem_k[0].astype(BF16), w_mem_v[0].astype(BF16))
    ybg, p_ssmc, p_ssm = _ssd_call(x_prompt, nmix, wz, wxbc, wdt, wg_b, cw, cb, dtb, alog, dful, gnw, wout, expand)
    x1, p_conv = _mix_call(x_prompt, ybg, mkb, mvb, nmix, wa, wq, wg_ac, scw, wsc, wao, wmo)
    y_prompt = _ffn_call(x1, nffn, wfg, wfu, wfd, nfin)

    xs2 = x_sample.reshape(ns, D_MODEL)
    (gaya, gb, gc, z, xs, dtxt, dect, bm, cm, q, s_conv, s_ssmc) = _sfront_call(
        xs2, state_conv[0].reshape(ns, -1), jnp.swapaxes(state_ssm_conv[0], 0, 1),
        nmix, wa, wz, wxbc, wdt, wq, wg, scw, cw, cb, dtb, alog, wsc, expand)
    s_ssm, y_s = _sssm_call(state_ssm[0].reshape(ns, SSM_D_INNER, SSM_STATE), dtxt, dect, bm, cm)
    att = _sattn_call(q, cache_mem_k, cache_mem_v)
    y_sample = _sback_call(xs2, y_s, xs, z, gaya, gb, gc, att, dful, gnw, wout, wao, wmo,
                           nffn, wfg, wfu, wfd, nfin)

    def from_rows(r):
        r = r.reshape(bp, MEM_LEN, ATTN_HEAD_DIM // V7X_LANES, ATTN_HEADS, V7X_LANES)
        return r.transpose(0, 1, 3, 2, 4).reshape(depth, bp, MEM_LEN, ATTN_HEADS, ATTN_HEAD_DIM)

    state_shape = (SSM_HEADS, SSM_HEAD_DIM, SSM_STATE)
    return (
        y_prompt,
        y_sample.reshape(ns, 1, D_MODEL),
        from_rows(mk),
        from_rows(mv),
        p_conv.reshape(depth, bp, SC_WIDTH - 1, SC_DIM),
        p_ssmc.reshape(depth, bp, SSM_CONV - 1, SSM_CONV_DIM),
        p_ssm.reshape((depth, bp) + state_shape),
        s_conv.reshape(depth, ns, SC_WIDTH - 1, SC_DIM),
        jnp.swapaxes(s_ssmc, 0, 1).reshape(depth, ns, SSM_CONV - 1, SSM_CONV_DIM),
        s_ssm.reshape((depth, ns) + state_shape),
    )
```
